```python
import math
import jax
import jax.numpy as jnp
from jax import lax
import numpy as np

D_MODEL = 2048
BATCH = 16
SEQ = 2048
DEPTH = 4

N_MIXERS = 4
MIX_WIDTH = D_MODEL
RMS_EPS = 1e-6
NEG_INF = -1e30

S5_GROUP = 16
S5_GROUPS = MIX_WIDTH // S5_GROUP
S5_STATE = 64
S5_DT_MIN = 1e-3
S5_DT_MAX = 1e-1

DIFF_HEADS = 16
DIFF_HEAD_DIM = MIX_WIDTH // (2 * DIFF_HEADS)
Q_BLOCK = 128

MOBA_HEADS = 16
MOBA_HEAD_DIM = MIX_WIDTH // MOBA_HEADS
MOBA_BLOCK = 256
MOBA_TOPK = 3
MOBA_Q_CHUNK = 16

SWA_Q_HEADS = 32
SWA_KV_HEADS = 4
SWA_HEAD_DIM = MIX_WIDTH // SWA_Q_HEADS
SWA_WINDOW = 128

kernel_name = 'hybrid_s5_diff_moba_swa_trunk'


def _n_layers_of(kind):
    return len(range(kind, DEPTH, N_MIXERS))


def _rmsnorm(x, gain):
    x32 = x.astype(jnp.float32)
    y = x32 * lax.rsqrt(jnp.mean(x32 * x32, axis=-1, keepdims=True) + RMS_EPS)
    return (y * gain.astype(jnp.float32)).astype(x.dtype)


def _alibi_slopes(n_heads):
    return 2.0 ** (-8.0 * jnp.arange(1, n_heads + 1, dtype=jnp.float32) / n_heads)


def _cdiag_combine(left, right):
    ar1, ai1, br1, bi1 = left
    ar2, ai2, br2, bi2 = right
    return (ar2 * ar1 - ai2 * ai1,
            ar2 * ai1 + ai2 * ar1,
            ar2 * br1 - ai2 * bi1 + br2,
            ar2 * bi1 + ai2 * br1 + bi2)


def _s5_branch(h, w_in, a_re, a_im, log_dt, b_re, b_im, c_re, c_im, d_skip, w_glu, b_glu, w_out):
    f32 = jnp.float32
    bsz, seq, _ = h.shape
    proj = h @ w_in
    u, z = proj[..., :MIX_WIDTH], proj[..., MIX_WIDTH:]
    u32 = u.astype(f32)
    ug = u32.reshape(bsz, seq, S5_GROUPS, S5_GROUP)
    lr, li = a_re.astype(f32), a_im.astype(f32)
    dt = jnp.exp(log_dt.astype(f32))[:, None]
    mag = jnp.exp(lr * dt)
    ab_re, ab_im = mag * jnp.cos(li * dt), mag * jnp.sin(li * dt)
    den = lr * lr + li * li
    f_re = ((ab_re - 1.0) * lr + ab_im * li) / den
    f_im = (ab_im * lr - (ab_re - 1.0) * li) / den
    br, bi = b_re.astype(f32), b_im.astype(f32)
    bb_re = f_re[..., None] * br - f_im[..., None] * bi
    bb_im = f_re[..., None] * bi + f_im[..., None] * br
    bu_re = jnp.einsum('bsgc,gpc->bsgp', ug, bb_re)
    bu_im = jnp.einsum('bsgc,gpc->bsgp', ug, bb_im)
    shape_a = (1, seq, S5_GROUPS, S5_STATE)
    elems = (jnp.broadcast_to(ab_re, shape_a), jnp.broadcast_to(ab_im, shape_a), bu_re, bu_im)
    _, _, st_re, st_im = lax.associative_scan(_cdiag_combine, elems, axis=1)
    y = (jnp.einsum('bsgp,gcp->bsgc', st_re, c_re.astype(f32))
         - jnp.einsum('bsgp,gcp->bsgc', st_im, c_im.astype(f32)))
    y = y.reshape(bsz, seq, MIX_WIDTH) + d_skip.astype(f32) * u32
    g = jax.nn.gelu(y)
    y = g * jax.nn.sigmoid(g @ w_glu.astype(f32) + b_glu.astype(f32))
    y = y.astype(h.dtype) * jax.nn.silu(z)
    return y @ w_out


def _diff_branch(h, layer_idx, w_in, lq1, lk1, lq2, lk2, subln, w_out):
    f32 = jnp.float32
    bsz, seq, _ = h.shape
    nh, d = DIFF_HEADS, DIFF_HEAD_DIM
    q, k, v, z = jnp.split(h @ w_in, 4, axis=-1)
    q = q.reshape(bsz, seq, nh, 2, d).transpose(3, 0, 2, 1, 4).astype(f32)
    k = k.reshape(bsz, seq, nh, 2, d).transpose(3, 0, 2, 1, 4).astype(f32)
    v = v.reshape(bsz, seq, nh, 2 * d).transpose(0, 2, 1, 3).astype(f32)
    lam_init = 0.8 - 0.6 * math.exp(-0.3 * layer_idx)
    lam = (jnp.exp(jnp.sum(lq1.astype(f32) * lk1.astype(f32)))
           - jnp.exp(jnp.sum(lq2.astype(f32) * lk2.astype(f32))) + lam_init)
    slopes = _alibi_slopes(nh)[:, None, None]
    scale = d ** -0.5
    outs = []
    for start in range(0, seq, Q_BLOCK):
        end = start + Q_BLOCK
        dist = (jnp.arange(start, end)[:, None] - jnp.arange(end)[None, :]).astype(f32)
        bias = jnp.where(dist >= 0, -slopes * dist, NEG_INF)
        s = jnp.einsum('ibhqd,ibhkd->ibhqk', q[:, :, :, start:end], k[:, :, :, :end]) * scale + bias
        a = jax.nn.softmax(s, axis=-1)
        attn = a[0] - lam * a[1]
        outs.append(jnp.einsum('bhqk,bhkd->bhqd', attn, v[:, :, :end]))
    o = jnp.concatenate(outs, axis=2)
    o = _rmsnorm(o, subln) * (1.0 - lam_init)
    o = o.transpose(0, 2, 1, 3).reshape(bsz, seq, MIX_WIDTH)
    y = o.astype(h.dtype) * jax.nn.silu(z)
    return y @ w_out


def _moba_attention(q, k, v):
    f32 = jnp.float32
    bsz, seq, nh, dh = q.shape
    blk = MOBA_BLOCK
    nb = -(-seq // blk)
    sp = nb * blk
    pad = ((0, 0), (0, sp - seq), (0, 0), (0, 0))
    qh = jnp.pad(q, pad).astype(f32).transpose(0, 2, 1, 3)
    kh = jnp.pad(k, pad).astype(f32).transpose(0, 2, 1, 3)
    vh = jnp.pad(v, pad).astype(f32).transpose(0, 2, 1, 3)
    kb = kh.reshape(bsz, nh, nb, blk, dh)
    vb = vh.reshape(bsz, nh, nb, blk, dh)
    scale = dh ** -0.5
    slopes = _alibi_slopes(nh)
    loc = jnp.arange(blk)
    dist_own = (loc[:, None] - loc[None, :]).astype(f32)
    s_own = (jnp.einsum('bhnqd,bhnkd->bhnqk', qh.reshape(bsz, nh, nb, blk, dh), kb) * scale
             - slopes[None, :, None, None, None] * dist_own)
    s_own = jnp.where(dist_own >= 0, s_own, NEG_INF)
    m_own = jnp.max(s_own, axis=-1)
    p_own = jnp.exp(s_own - m_own[..., None])
    l_own = jnp.sum(p_own, axis=-1).reshape(bsz, nh, sp)
    o_own = jnp.einsum('bhnqk,bhnkd->bhnqd', p_own, vb).reshape(bsz, nh, sp, dh)
    m_own = m_own.reshape(bsz, nh, sp)
    n_sel = min(MOBA_TOPK, nb - 1)
    if n_sel == 0:
        out = o_own / l_own[..., None]
    else:
        pos = jnp.arange(sp)
        qblk = pos // blk
        gate = jnp.einsum('bhtd,bhnd->bhtn', qh, jnp.mean(kb, axis=3))
        past = jnp.arange(nb)[None, :] < qblk[:, None]
        gate = jnp.where(past, gate, NEG_INF)
        _, idx = lax.top_k(gate, n_sel)
        valid = jnp.arange(n_sel)[None, :] < qblk[:, None]
        n_chunks = sp // MOBA_Q_CHUNK
        bi = jnp.arange(bsz)[:, None, None, None]
        hi = jnp.arange(nh)[None, :, None, None]

        def chunk(args):
            qc, idxc, validc, tc = args
            kg = kb[bi, hi, idxc]
            vg = vb[bi, hi, idxc]
            dist = (tc[None, None, :, None, None] - (idxc[..., None] * blk + loc)).astype(f32)
            s = (jnp.einsum('bhcd,bhcjkd->bhcjk', qc, kg) * scale
                 - slopes[None, :, None, None, None] * dist)
            mask = validc[None, None, :, :, None]
            s = jnp.where(mask, s, NEG_INF)
            m = jnp.max(s, axis=(-2, -1))
            p = jnp.where(mask, jnp.exp(s - m[..., None, None]), 0.0)
            return m, jnp.sum(p, axis=(-2, -1)), jnp.einsum('bhcjk,bhcjkd->bhcd', p, vg)

        def to_chunks(t):
            return jnp.moveaxis(t.reshape(bsz, nh, n_chunks, MOBA_Q_CHUNK, *t.shape[3:]), 2, 0)

        def from_chunks(t):
            t = jnp.moveaxis(t, 0, 2)
            return t.reshape(bsz, nh, sp, *t.shape[4:])

        xs = (to_chunks(qh), to_chunks(idx),
              valid.reshape(n_chunks, MOBA_Q_CHUNK, n_sel), pos.reshape(n_chunks, MOBA_Q_CHUNK))
        m_sel, l_sel, o_sel = lax.map(chunk, xs)
        m_sel, l_sel, o_sel = from_chunks(m_sel), from_chunks(l_sel), from_chunks(o_sel)
        m = jnp.maximum(m_own, m_sel)
        a_own, a_sel = jnp.exp(m_own - m), jnp.exp(m_sel - m)
        out = ((o_own * a_own[..., None] + o_sel * a_sel[..., None])
               / (l_own * a_own + l_sel * a_sel)[..., None])
    return out[:, :, :seq].transpose(0, 2, 1, 3)


def _moba_branch(h, w_in, w_out):
    bsz, seq, _ = h.shape
    q, k, v, z = jnp.split(h @ w_in, 4, axis=-1)
    shp = (bsz, seq, MOBA_HEADS, MOBA_HEAD_DIM)
    o = _moba_attention(q.reshape(shp), k.reshape(shp), v.reshape(shp))
    y = o.reshape(bsz, seq, MIX_WIDTH).astype(h.dtype) * jax.nn.silu(z)
    return y @ w_out


def _swa_branch(h, w_in, sinks, w_out):
    f32 = jnp.float32
    bsz, seq, _ = h.shape
    hq, hk, dh, win = SWA_Q_HEADS, SWA_KV_HEADS, SWA_HEAD_DIM, SWA_WINDOW
    grp = hq // hk
    nqc, nkc = hq * dh, hk * dh
    proj = h @ w_in
    q = proj[..., :nqc]
    k = proj[..., nqc:nqc + nkc]
    v = proj[..., nqc + nkc:nqc + 2 * nkc]
    z = proj[..., nqc + 2 * nkc:]
    nblk = seq // win
    q = q.reshape(bsz, nblk, win, hk, grp, dh).astype(f32)
    k = k.reshape(bsz, seq, hk, dh).astype(f32)
    v = v.reshape(bsz, seq, hk, dh).astype(f32)

    def band(t):
        tp = jnp.pad(t, ((0, 0), (win, 0), (0, 0), (0, 0)))
        prev = tp[:, :seq].reshape(bsz, nblk, win, hk, dh)
        cur = t.reshape(bsz, nblk, win, hk, dh)
        return jnp.concatenate([prev, cur], axis=2)

    kb, vb = band(k), band(v)
    qi = jnp.arange(win)[:, None]
    kj = jnp.arange(2 * win)[None, :]
    dist = qi + win - kj
    kpos = jnp.arange(nblk)[:, None, None] * win + kj[None] - win
    valid = (dist >= 0) & (dist < win) & (kpos >= 0)
    slopes = _alibi_slopes(hq).reshape(hk, grp)
    s = (jnp.einsum('bnqkgd,bnjkd->bnkgqj', q, kb) * dh ** -0.5
         - slopes[:, :, None, None] * dist.astype(f32))
    s = jnp.where(valid[None, :, None, None], s, NEG_INF)
    sink = sinks.astype(f32).reshape(hk, grp)[:, :, None]
    m = jnp.maximum(jnp.max(s, axis=-1), sink)
    p = jnp.exp(s - m[..., None])
    denom = jnp.sum(p, axis=-1) + jnp.exp(sink - m)
    o = jnp.einsum('bnkgqj,bnjkd->bnqkgd', p, vb) / denom.transpose(0, 1, 4, 2, 3)[..., None]
    y = o.reshape(bsz, seq, MIX_WIDTH).astype(h.dtype) * jax.nn.silu(z)
    return y @ w_out


def setup_inputs(seed: int = 0) -> dict:
    key = jax.random.key(seed)
    ks = jax.random.split(key, 32)
    f32 = jnp.float32
    d, e = D_MODEL, MIX_WIDTH
    n_a, n_b, n_c, n_d = (_n_layers_of(0), _n_layers_of(1), _n_layers_of(2), _n_layers_of(3))
    g, p, c = S5_GROUPS, S5_STATE, S5_GROUP

    def nrm(k, shape, scale):
        return jax.random.normal(k, shape, f32) * scale

    swa_cols = SWA_Q_HEADS * SWA_HEAD_DIM + 2 * SWA_KV_HEADS * SWA_HEAD_DIM + e
    n_idx = jnp.arange(p, dtype=f32)
    return {
        'x': nrm(ks[0], (BATCH, SEQ, d), 1.0),
        'pre_norm': 1.0 + nrm(ks[1], (DEPTH, d), 0.02),
        'post_norm': 1.0 + nrm(ks[2], (DEPTH, d), 0.02),
        's5_w_in': nrm(ks[3], (n_a, d, 2 * e), d ** -0.5),
        's5_a_re': -0.5 + nrm(ks[4], (n_a, g, p), 0.01),
        's5_a_im': math.pi * n_idx + nrm(ks[5], (n_a, g, p), 0.01),
        's5_log_dt': jax.random.uniform(ks[6], (n_a, g), f32, math.log(S5_DT_MIN), math.log(S5_DT_MAX)),
        's5_b_re': nrm(ks[7], (n_a, g, p, c), (2 * c) ** -0.5),
        's5_b_im': nrm(ks[8], (n_a, g, p, c), (2 * c) ** -0.5),
        's5_c_re': nrm(ks[9], (n_a, g, c, p), 0.5),
        's5_c_im': nrm(ks[10], (n_a, g, c, p), 0.5),
        's5_d': nrm(ks[11], (n_a, e), 1.0),
        's5_w_glu': nrm(ks[12], (n_a, e, e), e ** -0.5),
        's5_b_glu': nrm(ks[13], (n_a, e), 0.01),
        's5_w_out': nrm(ks[14], (n_a, e, d), e ** -0.5),
        'diff_w_in': nrm(ks[15], (n_b, d, 4 * e), d ** -0.5),
        'diff_lq1': nrm(ks[16], (n_b, DIFF_HEAD_DIM), 0.1),
        'diff_lk1': nrm(ks[17], (n_b, DIFF_HEAD_DIM), 0.1),
        'diff_lq2': nrm(ks[18], (n_b, DIFF_HEAD_DIM), 0.1),
        'diff_lk2': nrm(ks[19], (n_b, DIFF_HEAD_DIM), 0.1),
        'diff_subln': 1.0 + nrm(ks[20], (n_b, 2 * DIFF_HEAD_DIM), 0.02),
        'diff_w_out': nrm(ks[21], (n_b, e, d), e ** -0.5),
        'moba_w_in': nrm(ks[22], (n_c, d, 4 * e), d ** -0.5),
        'moba_w_out': nrm(ks[23], (n_c, e, d), e ** -0.5),
        'swa_w_in': nrm(ks[24], (n_d, d, swa_cols), d ** -0.5),
        'swa_sinks': nrm(ks[25], (n_d, SWA_Q_HEADS), 0.5),
        'swa_w_out': nrm(ks[26], (n_d, e, d), e ** -0.5),
    }


def reference(x, pre_norm, post_norm,
              s5_w_in, s5_a_re, s5_a_im, s5_log_dt, s5_b_re, s5_b_im, s5_c_re, s5_c_im,
              s5_d, s5_w_glu, s5_b_glu, s5_w_out,
              diff_w_in, diff_lq1, diff_lk1, diff_lq2, diff_lk2, diff_subln, diff_w_out,
              moba_w_in, moba_w_out,
              swa_w_in, swa_sinks, swa_w_out):
    for i in range(DEPTH):
        kind, j = i % N_MIXERS, i // N_MIXERS
        h = _rmsnorm(x, pre_norm[i])
        if kind == 0:
            y = _s5_branch(h, s5_w_in[j], s5_a_re[j], s5_a_im[j], s5_log_dt[j], s5_b_re[j],
                           s5_b_im[j], s5_c_re[j], s5_c_im[j], s5_d[j], s5_w_glu[j],
                           s5_b_glu[j], s5_w_out[j])
        elif kind == 1:
            y = _diff_branch(h, i, diff_w_in[j], diff_lq1[j], diff_lk1[j], diff_lq2[j],
                             diff_lk2[j], diff_subln[j], diff_w_out[j])
        elif kind == 2:
            y = _moba_branch(h, moba_w_in[j], moba_w_out[j])
        else:
            y = _swa_branch(h, swa_w_in[j], swa_sinks[j], swa_w_out[j])
        x = x + _rmsnorm(y, post_norm[i])
    return x
```

```python
import functools
import math

import jax
import jax.numpy as jnp
from jax import lax
from jax.experimental import pallas as pl
from jax.experimental.pallas import tpu as pltpu

F32 = jnp.float32
BF16 = jnp.bfloat16

D_MODEL = 2048
MIX_WIDTH = D_MODEL
RMS_EPS = 1e-6
NEG_INF = -1e30
LANES = 128

S5_GROUP = 16
S5_GROUPS = MIX_WIDTH // S5_GROUP
S5_STATE = 64
S5_SLAB = 256
S5_SLAB_STATE = (S5_SLAB // S5_GROUP) * S5_STATE

DIFF_HEADS = 16
DIFF_HEAD_DIM = 64
MOBA_HEADS = 16
MOBA_HEAD_DIM = 128
MOBA_BLOCK = 256
MOBA_TOPK = 3
SWA_Q_HEADS = 32
SWA_KV_HEADS = 4
SWA_HEAD_DIM = 64
SWA_WINDOW = 128

VMEM_LIMIT = 56 * 1024 * 1024


def _params(*sem):
    return pltpu.CompilerParams(dimension_semantics=sem, vmem_limit_bytes=VMEM_LIMIT)


def _alibi_slopes(n_heads):
    return 2.0 ** (-8.0 * jnp.arange(1, n_heads + 1, dtype=F32) / n_heads)


def _norm_matmul_kernel(x_ref, g_ref, w_ref, o_ref, h_ref):
    @pl.when(pl.program_id(1) == 0)
    def _():
        x = x_ref[...]
        ms = jnp.mean(x * x, axis=-1, keepdims=True)
        h_ref[...] = (x * lax.rsqrt(ms + RMS_EPS) * g_ref[...]).astype(BF16)

    o_ref[...] = jnp.dot(h_ref[...], w_ref[...], preferred_element_type=F32).astype(o_ref.dtype)


def _norm_matmul(x2d, gain, w, *, tm=1024, tn=1024):
    t, d = x2d.shape
    n = w.shape[1]
    tm = min(tm, t)
    tn = min(tn, n)
    assert t % tm == 0 and n % tn == 0
    return pl.pallas_call(
        _norm_matmul_kernel,
        grid=(t // tm, n // tn),
        in_specs=[
            pl.BlockSpec((tm, d), lambda i, j: (i, 0)),
            pl.BlockSpec((1, d), lambda i, j: (0, 0)),
            pl.BlockSpec((d, tn), lambda i, j: (0, j)),
        ],
        out_specs=pl.BlockSpec((tm, tn), lambda i, j: (i, j)),
        out_shape=jax.ShapeDtypeStruct((t, n), BF16),
        scratch_shapes=[pltpu.VMEM((tm, d), BF16)],
        compiler_params=_params("parallel", "arbitrary"),
        name="norm_matmul",
    )(x2d, gain.reshape(1, d).astype(F32), w)


def _gate_out_kernel(o_ref, z_ref, x_ref, w_ref, g_ref, out_ref):
    z = z_ref[...].astype(F32)
    y = (o_ref[...].astype(F32) * (z * jax.nn.sigmoid(z))).astype(BF16)
    r = jnp.dot(y, w_ref[...], preferred_element_type=F32)
    ms = jnp.mean(r * r, axis=-1, keepdims=True)
    out_ref[...] = x_ref[...] + r * lax.rsqrt(ms + RMS_EPS) * g_ref[...]


def _gate_out(o2d, proj2d, z_block, x2d, w, gain, *, tm=256):
    t, e = o2d.shape
    d = w.shape[1]
    tm = min(tm, t)
    assert t % tm == 0
    return pl.pallas_call(
        _gate_out_kernel,
        grid=(t // tm,),
        in_specs=[
            pl.BlockSpec((tm, e), lambda i: (i, 0)),
            pl.BlockSpec((tm, e), lambda i: (i, z_block)),
            pl.BlockSpec((tm, d), lambda i: (i, 0)),
            pl.BlockSpec((e, d), lambda i: (0, 0)),
            pl.BlockSpec((1, d), lambda i: (0, 0)),
        ],
        out_specs=pl.BlockSpec((tm, d), lambda i: (i, 0)),
        out_shape=jax.ShapeDtypeStruct((t, d), F32),
        compiler_params=_params("parallel"),
        name="gate_out",
    )(o2d, proj2d, x2d, w, gain.reshape(1, d).astype(F32))


def _softmax_step(s, v, m_ref, l_ref, acc_ref):
    m_old = m_ref[...]
    m_new = jnp.maximum(m_old, jnp.max(s, axis=-1, keepdims=True))
    alpha = jnp.exp(m_old - m_new)
    p = jnp.exp(s - m_new)
    l_ref[...] = alpha * l_ref[...] + jnp.sum(p, axis=-1, keepdims=True)
    acc_ref[...] = alpha * acc_ref[...] + jnp.dot(p.astype(BF16), v, preferred_element_type=F32)
    m_ref[...] = m_new


def _dot_nt(a, b):
    return lax.dot_general(a, b, (((1,), (1,)), ((), ())), preferred_element_type=F32)


def _diff_attn_kernel(slopes_ref, lam_ref, q_ref, k_ref, v_ref, g_ref, o_ref,
                      m_ref, l_ref, acc_ref, *, tq, out_scale):
    h = pl.program_id(1)
    i = pl.program_id(2)
    slope = slopes_ref[h]
    lam = lam_ref[0]
    d = DIFF_HEAD_DIM

    q = q_ref[0].astype(F32) * (d ** -0.5)
    lane = lax.broadcasted_iota(jnp.int32, (tq, 2 * d), 1)
    qq = jnp.concatenate([jnp.where(lane < d, q, 0.0), jnp.where(lane >= d, q, 0.0)],
                         axis=0).astype(BF16)

    m_ref[...] = jnp.full_like(m_ref, NEG_INF)
    l_ref[...] = jnp.zeros_like(l_ref)
    acc_ref[...] = jnp.zeros_like(acc_ref)

    col = lax.broadcasted_iota(jnp.int32, (1, tq), 1)

    def scores(j):
        start = pl.multiple_of(j * tq, tq)
        k = k_ref[0, pl.ds(start, tq), :]
        v = v_ref[0, pl.ds(start, tq), :]
        rel = (col + (j - i) * tq).astype(F32)
        return _dot_nt(qq, k) + slope * rel, v

    def body(j, carry):
        s, v = scores(j)
        _softmax_step(s, v, m_ref, l_ref, acc_ref)
        return carry

    lax.fori_loop(0, i, body, 0)

    s, v = scores(i)
    row = lax.broadcasted_iota(jnp.int32, (2 * tq, 1), 0)
    row = jnp.where(row >= tq, row - tq, row)
    s = jnp.where(row >= col, s, NEG_INF)
    _softmax_step(s, v, m_ref, l_ref, acc_ref)

    a = acc_ref[...] / l_ref[...]
    o = a[:tq] - lam * a[tq:]
    ms = jnp.mean(o * o, axis=-1, keepdims=True)
    o = o * lax.rsqrt(ms + RMS_EPS) * g_ref[...] * out_scale
    o_ref[0] = o.astype(o_ref.dtype)


def _diff_attention(proj, slopes, lam, subln, out_scale, *, tq=256):
    b, s, _ = proj.shape
    nh = DIFF_HEADS
    w = 2 * DIFF_HEAD_DIM
    tq = min(tq, s)
    assert s % tq == 0
    smem = pl.BlockSpec(memory_space=pltpu.SMEM)
    return pl.pallas_call(
        functools.partial(_diff_attn_kernel, tq=tq, out_scale=out_scale),
        grid=(b, nh, s // tq),
        in_specs=[
            smem, smem,
            pl.BlockSpec((1, tq, w), lambda bi, h, i: (bi, i, h)),
            pl.BlockSpec((1, s, w), lambda bi, h, i: (bi, 0, nh + h)),
            pl.BlockSpec((1, s, w), lambda bi, h, i: (bi, 0, 2 * nh + h)),
            pl.BlockSpec((1, w), lambda bi, h, i: (0, 0)),
        ],
        out_specs=pl.BlockSpec((1, tq, w), lambda bi, h, i: (bi, i, h)),
        out_shape=jax.ShapeDtypeStruct((b, s, MIX_WIDTH), BF16),
        scratch_shapes=[pltpu.VMEM((2 * tq, 1), F32), pltpu.VMEM((2 * tq, 1), F32),
                        pltpu.VMEM((2 * tq, w), F32)],
        compiler_params=_params("parallel", "parallel", "arbitrary"),
        name="diff_attn",
    )(slopes, lam, proj, proj, proj, subln.reshape(1, w).astype(F32))


def _moba_attn_kernel(slopes_ref, q_ref, k_ref, v_ref, o_ref, m_ref, l_ref, acc_ref, *, nb):
    h = pl.program_id(1)
    n = pl.program_id(2)
    slope = slopes_ref[h]
    blk = MOBA_BLOCK
    dh = MOBA_HEAD_DIM

    q = q_ref[0]
    qs = (q.astype(F32) * (dh ** -0.5)).astype(BF16)

    kmean = jnp.mean(k_ref[0].astype(F32).reshape(nb, blk, dh), axis=1)
    km_hi = kmean.astype(BF16)
    km_lo = (kmean - km_hi.astype(F32)).astype(BF16)
    gate = _dot_nt(q, km_hi) + _dot_nt(q, km_lo)

    bidx = lax.broadcasted_iota(jnp.int32, (blk, nb), 1)
    rank = jnp.zeros((blk, nb), jnp.int32)
    for jp in range(nb):
        gj = gate[:, jp:jp + 1]
        before = (gj > gate) | ((gj == gate) & (jp < bidx))
        rank = rank + jnp.where(before, 1, 0) * (jp < n).astype(jnp.int32)
    sel = jnp.where((rank < MOBA_TOPK) & (bidx < n), 1.0, 0.0)

    col = lax.broadcasted_iota(jnp.int32, (1, blk), 1)
    row = lax.broadcasted_iota(jnp.int32, (blk, 1), 0)

    def scores(j):
        start = pl.multiple_of(j * blk, blk)
        k = k_ref[0, pl.ds(start, blk), :]
        v = v_ref[0, pl.ds(start, blk), :]
        rel = (col + (j - n) * blk).astype(F32)
        return _dot_nt(qs, k) + slope * rel, v

    s, v = scores(n)
    s = jnp.where(row >= col, s, NEG_INF)
    m = jnp.max(s, axis=-1, keepdims=True)
    p = jnp.exp(s - m)
    m_ref[...] = m
    l_ref[...] = jnp.sum(p, axis=-1, keepdims=True)
    acc_ref[...] = jnp.dot(p.astype(BF16), v, preferred_element_type=F32)

    def body(j, carry):
        s, v = scores(j)
        chosen = jnp.max(jnp.where(bidx == j, sel, 0.0), axis=-1, keepdims=True)
        s = jnp.where(chosen > 0.0, s, NEG_INF)
        _softmax_step(s, v, m_ref, l_ref, acc_ref)
        return carry

    lax.fori_loop(0, n, body, 0)
    o_ref[0] = (acc_ref[...] / l_ref[...]).astype(o_ref.dtype)


def _moba_attention(proj, slopes):
    b, s, _ = proj.shape
    nh = MOBA_HEADS
    dh = MOBA_HEAD_DIM
    blk = MOBA_BLOCK
    assert s % blk == 0
    nb = s // blk
    smem = pl.BlockSpec(memory_space=pltpu.SMEM)
    return pl.pallas_call(
        functools.partial(_moba_attn_kernel, nb=nb),
        grid=(b, nh, nb),
        in_specs=[
            smem,
            pl.BlockSpec((1, blk, dh), lambda bi, h, i: (bi, i, h)),
            pl.BlockSpec((1, s, dh), lambda bi, h, i: (bi, 0, nh + h)),
            pl.BlockSpec((1, s, dh), lambda bi, h, i: (bi, 0, 2 * nh + h)),
        ],
        out_specs=pl.BlockSpec((1, blk, dh), lambda bi, h, i: (bi, i, h)),
        out_shape=jax.ShapeDtypeStruct((b, s, MIX_WIDTH), BF16),
        scratch_shapes=[pltpu.VMEM((blk, 1), F32), pltpu.VMEM((blk, 1), F32),
                        pltpu.VMEM((blk, dh), F32)],
        compiler_params=_params("parallel", "parallel", "arbitrary"),
        name="moba_attn",
    )(slopes, proj, proj, proj)


def _swa_attn_kernel(slopes_ref, sinks_ref, q_ref, k_ref, v_ref, o_ref):
    i = pl.program_id(1)
    win = SWA_WINDOW
    dh = SWA_HEAD_DIM
    grp = SWA_Q_HEADS // SWA_KV_HEADS
    tq = win

    kstart = pl.multiple_of(jnp.maximum(i - 1, 0) * win, win)
    qpos = i * win + lax.broadcasted_iota(jnp.int32, (2 * tq, 1), 0) % tq
    kpos = kstart + lax.broadcasted_iota(jnp.int32, (1, 2 * win), 1)
    dist = qpos - kpos
    valid = (dist >= 0) & (dist < win)
    distf = dist.astype(F32)
    top = lax.broadcasted_iota(jnp.int32, (2 * tq, 1), 0) < tq
    lane = lax.broadcasted_iota(jnp.int32, (tq, 2 * dh), 1)

    for kvh in range(SWA_KV_HEADS):
        k = k_ref[0, pl.ds(kstart, 2 * win), kvh * 2 * dh:(kvh + 1) * 2 * dh]
        v = v_ref[0, pl.ds(kstart, 2 * win), kvh * 2 * dh:(kvh + 1) * 2 * dh]
        for pair in range(grp // 2):
            h0 = kvh * grp + 2 * pair
            q = q_ref[0, :, h0 * dh:(h0 + 2) * dh].astype(F32) * (dh ** -0.5)
            qq = jnp.concatenate([jnp.where(lane < dh, q, 0.0), jnp.where(lane >= dh, q, 0.0)],
                                 axis=0).astype(BF16)
            slope = jnp.where(top, slopes_ref[h0], slopes_ref[h0 + 1])
            sink = jnp.where(top, sinks_ref[h0], sinks_ref[h0 + 1])
            s = _dot_nt(qq, k) - slope * distf
            s = jnp.where(valid, s, NEG_INF)
            m = jnp.maximum(jnp.max(s, axis=-1, keepdims=True), sink)
            p = jnp.exp(s - m)
            denom = jnp.sum(p, axis=-1, keepdims=True) + jnp.exp(sink - m)
            o2 = jnp.dot(p.astype(BF16), v, preferred_element_type=F32) / denom
            o = jnp.where(lane < dh, o2[:tq], o2[tq:])
            o_ref[0, :, h0 * dh:(h0 + 2) * dh] = o.astype(o_ref.dtype)


def _swa_attention(proj, slopes, sinks):
    b, s, _ = proj.shape
    win = SWA_WINDOW
    e = MIX_WIDTH
    kvw = SWA_KV_HEADS * 2 * SWA_HEAD_DIM
    assert s % win == 0 and s >= 2 * win
    smem = pl.BlockSpec(memory_space=pltpu.SMEM)
    return pl.pallas_call(
        _swa_attn_kernel,
        grid=(b, s // win),
        in_specs=[
            smem, smem,
            pl.BlockSpec((1, win, e), lambda bi, i: (bi, i, 0)),
            pl.BlockSpec((1, s, kvw), lambda bi, i: (bi, 0, 2 * e // kvw)),
            pl.BlockSpec((1, s, kvw), lambda bi, i: (bi, 0, 2 * e // kvw + 1)),
        ],
        out_specs=pl.BlockSpec((1, win, e), lambda bi, i: (bi, i, 0)),
        out_shape=jax.ShapeDtypeStruct((b, s, e), BF16),
        compiler_params=_params("parallel", "arbitrary"),
        name="swa_attn",
    )(slopes, sinks, proj, proj, proj)


def _s5_ssm_kernel(u_ref, bb_ref, a_ref, cc_ref, d_ref, y_ref, st_ref, x_ref,
                   *, bsz, tt, chunk_tiles):
    ns = S5_SLAB_STATE
    nt = ns // LANES

    @pl.when(pl.program_id(1) == 0)
    def _():
        st_ref[...] = jnp.zeros_like(st_ref)

    u = u_ref[...].reshape(bsz * tt, S5_SLAB)
    bu = jnp.dot(u, bb_ref[0], preferred_element_type=F32)
    for lt in range(2 * nt):
        x_ref[lt] = bu[:, lt * LANES:(lt + 1) * LANES]

    for c in range(nt // chunk_tiles):
        tiles = range(c * chunk_tiles, (c + 1) * chunk_tiles)
        ar = [jnp.broadcast_to(a_ref[0, 0:1, lt * LANES:(lt + 1) * LANES], (bsz, LANES))
              for lt in tiles]
        ai = [jnp.broadcast_to(a_ref[0, 1:2, lt * LANES:(lt + 1) * LANES], (bsz, LANES))
              for lt in tiles]

        def body(t, carry):
            rows = pl.ds(t, bsz, stride=tt)
            out = []
            for n, lt in enumerate(tiles):
                xr, xi = carry[2 * n], carry[2 * n + 1]
                nxr = ar[n] * xr - ai[n] * xi + x_ref[lt, rows, :]
                nxi = ar[n] * xi + ai[n] * xr + x_ref[nt + lt, rows, :]
                x_ref[lt, rows, :] = nxr
                x_ref[nt + lt, rows, :] = nxi
                out += [nxr, nxi]
            return tuple(out)

        init = []
        for lt in tiles:
            init += [st_ref[lt], st_ref[nt + lt]]
        fin = lax.fori_loop(0, tt, body, tuple(init))
        for n, lt in enumerate(tiles):
            st_ref[lt] = fin[2 * n]
            st_ref[nt + lt] = fin[2 * n + 1]

    xr_all = jnp.concatenate([x_ref[lt].astype(BF16) for lt in range(nt)], axis=1)
    xi_all = jnp.concatenate([x_ref[nt + lt].astype(BF16) for lt in range(nt)], axis=1)
    y = (jnp.dot(xr_all, cc_ref[0, :ns], preferred_element_type=F32)
         + jnp.dot(xi_all, cc_ref[0, ns:], preferred_element_type=F32))
    y = y + d_ref[0] * u.astype(F32)
    y_ref[...] = y.reshape(bsz, tt, S5_SLAB).astype(y_ref.dtype)


def _s5_ssm(proj, bd_b, a_bar, bd_c, d_skip, *, tt=32, chunk_tiles=4):
    b, s, _ = proj.shape
    tt = min(tt, s)
    assert s % tt == 0 and b % 8 == 0 and tt % 16 == 0
    n_slab = MIX_WIDTH // S5_SLAB
    ns = S5_SLAB_STATE
    return pl.pallas_call(
        functools.partial(_s5_ssm_kernel, bsz=b, tt=tt, chunk_tiles=chunk_tiles),
        grid=(n_slab, s // tt),
        in_specs=[
            pl.BlockSpec((b, tt, S5_SLAB), lambda sl, t: (0, t, sl)),
            pl.BlockSpec((1, S5_SLAB, 2 * ns), lambda sl, t: (sl, 0, 0)),
            pl.BlockSpec((1, 2, ns), lambda sl, t: (sl, 0, 0)),
            pl.BlockSpec((1, 2 * ns, S5_SLAB), lambda sl, t: (sl, 0, 0)),
            pl.BlockSpec((1, 1, S5_SLAB), lambda sl, t: (sl, 0, 0)),
        ],
        out_specs=pl.BlockSpec((b, tt, S5_SLAB), lambda sl, t: (0, t, sl)),
        out_shape=jax.ShapeDtypeStruct((b, s, MIX_WIDTH), BF16),
        scratch_shapes=[pltpu.VMEM((2 * ns // LANES, b, LANES), F32),
                        pltpu.VMEM((2 * ns // LANES, b * tt, LANES), F32)],
        compiler_params=_params("parallel", "arbitrary"),
        name="s5_ssm",
    )(proj, bd_b, a_bar, bd_c, d_skip)


def _s5_discretize(a_re, a_im, log_dt, b_re, b_im, c_re, c_im, d_skip):
    lr, li = a_re.astype(F32), a_im.astype(F32)
    dt = jnp.exp(log_dt.astype(F32))[:, None]
    mag = jnp.exp(lr * dt)
    ab_re, ab_im = mag * jnp.cos(li * dt), mag * jnp.sin(li * dt)
    den = lr * lr + li * li
    f_re = ((ab_re - 1.0) * lr + ab_im * li) / den
    f_im = (ab_im * lr - (ab_re - 1.0) * li) / den
    br, bi = b_re.astype(F32), b_im.astype(F32)
    bb_re = f_re[..., None] * br - f_im[..., None] * bi
    bb_im = f_re[..., None] * bi + f_im[..., None] * br
    n_slab = MIX_WIDTH // S5_SLAB
    gs = S5_SLAB // S5_GROUP
    eye = jnp.eye(gs, dtype=F32)

    def pack_b(m):
        m = m.reshape(n_slab, gs, S5_STATE, S5_GROUP)
        return jnp.einsum('sgpc,gh->sgchp', m, eye).reshape(n_slab, S5_SLAB, S5_SLAB_STATE)

    def pack_c(m):
        m = m.reshape(n_slab, gs, S5_GROUP, S5_STATE)
        return jnp.einsum('sgcp,gh->shpgc', m, eye).reshape(n_slab, S5_SLAB_STATE, S5_SLAB)

    bd_b = jnp.concatenate([pack_b(bb_re), pack_b(bb_im)], axis=2).astype(BF16)
    bd_c = jnp.concatenate([pack_c(c_re.astype(F32)), pack_c(-c_im.astype(F32))], axis=1).astype(BF16)
    a_bar = jnp.stack([ab_re.reshape(n_slab, S5_SLAB_STATE), ab_im.reshape(n_slab, S5_SLAB_STATE)],
                      axis=1)
    return bd_b, a_bar, bd_c, d_skip.astype(F32).reshape(n_slab, 1, S5_SLAB)


def _glu_kernel(y_ref, w_ref, b_ref, o_ref, g_ref, *, tn):
    j = pl.program_id(1)

    @pl.when(j == 0)
    def _():
        g_ref[...] = jax.nn.gelu(y_ref[...].astype(F32)).astype(BF16)

    r = jnp.dot(g_ref[...], w_ref[...], preferred_element_type=F32) + b_ref[...]
    g = g_ref[:, pl.ds(pl.multiple_of(j * tn, tn), tn)].astype(F32)
    o_ref[...] = (g * jax.nn.sigmoid(r)).astype(o_ref.dtype)


def _glu(y2d, w, bias, *, tm=1024, tn=1024):
    t, e = y2d.shape
    tm = min(tm, t)
    assert t % tm == 0 and e % tn == 0
    return pl.pallas_call(
        functools.partial(_glu_kernel, tn=tn),
        grid=(t // tm, e // tn),
        in_specs=[
            pl.BlockSpec((tm, e), lambda i, j: (i, 0)),
            pl.BlockSpec((e, tn), lambda i, j: (0, j)),
            pl.BlockSpec((1, tn), lambda i, j: (0, j)),
        ],
        out_specs=pl.BlockSpec((tm, tn), lambda i, j: (i, j)),
        out_shape=jax.ShapeDtypeStruct((t, e), BF16),
        scratch_shapes=[pltpu.VMEM((tm, e), BF16)],
        compiler_params=_params("parallel", "arbitrary"),
        name="s5_glu",
    )(y2d, w, bias.reshape(1, e).astype(F32))


def _s5_layer(x2d, bsz, seq, pre_g, post_g, w_in, a_re, a_im, log_dt, b_re, b_im, c_re, c_im,
              d_skip, w_glu, b_glu, w_out):
    e = MIX_WIDTH
    proj = _norm_matmul(x2d, pre_g, w_in.astype(BF16))
    bd_b, a_bar, bd_c, d3 = _s5_discretize(a_re, a_im, log_dt, b_re, b_im, c_re, c_im, d_skip)
    y = _s5_ssm(proj.reshape(bsz, seq, 2 * e), bd_b, a_bar, bd_c, d3)
    y = _glu(y.reshape(bsz * seq, e), w_glu.astype(BF16), b_glu)
    return _gate_out(y, proj, 1, x2d, w_out.astype(BF16), post_g)


def _diff_layer(x2d, bsz, seq, layer_idx, pre_g, post_g, w_in, lq1, lk1, lq2, lk2, subln, w_out):
    e = MIX_WIDTH
    proj = _norm_matmul(x2d, pre_g, w_in.astype(BF16))
    lam_init = 0.8 - 0.6 * math.exp(-0.3 * layer_idx)
    lam = (jnp.exp(jnp.sum(lq1.astype(F32) * lk1.astype(F32)))
           - jnp.exp(jnp.sum(lq2.astype(F32) * lk2.astype(F32))) + lam_init).reshape(1)
    o = _diff_attention(proj.reshape(bsz, seq, 4 * e), _alibi_slopes(DIFF_HEADS), lam, subln,
                        1.0 - lam_init)
    return _gate_out(o.reshape(bsz * seq, e), proj, 3, x2d, w_out.astype(BF16), post_g)


def _moba_layer(x2d, bsz, seq, pre_g, post_g, w_in, w_out):
    e = MIX_WIDTH
    proj = _norm_matmul(x2d, pre_g, w_in.astype(BF16))
    o = _moba_attention(proj.reshape(bsz, seq, 4 * e), _alibi_slopes(MOBA_HEADS))
    return _gate_out(o.reshape(bsz * seq, e), proj, 3, x2d, w_out.astype(BF16), post_g)


def _swa_layer(x2d, bsz, seq, pre_g, post_g, w_in, sinks, w_out):
    e = MIX_WIDTH
    dh = SWA_HEAD_DIM
    nqc, nkc = SWA_Q_HEADS * dh, SWA_KV_HEADS * dh
    wq, wk = w_in[:, :nqc], w_in[:, nqc:nqc + nkc]
    wv, wz = w_in[:, nqc + nkc:nqc + 2 * nkc], w_in[:, nqc + 2 * nkc:]

    def dup(w):
        w = w.reshape(-1, SWA_KV_HEADS, 1, dh)
        return jnp.broadcast_to(w, (w.shape[0], SWA_KV_HEADS, 2, dh)).reshape(-1, 2 * nkc)

    w_cat = jnp.concatenate([wq, wz, dup(wk), dup(wv)], axis=1).astype(BF16)
    proj = _norm_matmul(x2d, pre_g, w_cat, tn=512)
    o = _swa_attention(proj.reshape(bsz, seq, w_cat.shape[1]), _alibi_slopes(SWA_Q_HEADS),
                       sinks.astype(F32))
    return _gate_out(o.reshape(bsz * seq, e), proj, 1, x2d, w_out.astype(BF16), post_g)


def kernel(x, pre_norm, post_norm, s5_w_in, s5_a_re, s5_a_im, s5_log_dt, s5_b_re, s5_b_im, s5_c_re, s5_c_im, s5_d, s5_w_glu, s5_b_glu, s5_w_out, diff_w_in, diff_lq1, diff_lk1, diff_lq2, diff_lk2, diff_subln, diff_w_out, moba_w_in, moba_w_out, swa_w_in, swa_sinks, swa_w_out):
    bsz, seq, d = x.shape
    depth = pre_norm.shape[0]
    x2d = x.reshape(bsz * seq, d)
    for i in range(depth):
        kind, j = i % 4, i // 4
        if kind == 0:
            x2d = _s5_layer(x2d, bsz, seq, pre_norm[i], post_norm[i], s5_w_in[j], s5_a_re[j],
                            s5_a_im[j], s5_log_dt[j], s5_b_re[j], s5_b_im[j], s5_c_re[j],
                            s5_c_im[j], s5_d[j], s5_w_glu[j], s5_b_glu[j], s5_w_out[j])
        elif kind == 1:
            x2d = _diff_layer(x2d, bsz, seq, i, pre_norm[i], post_norm[i], diff_w_in[j],
                              diff_lq1[j], diff_lk1[j], diff_lq2[j], diff_lk2[j], diff_subln[j],
                              diff_w_out[j])
        elif kind == 2:
            x2d = _moba_layer(x2d, bsz, seq, pre_norm[i], post_norm[i], moba_w_in[j], moba_w_out[j])
        else:
            x2d = _swa_layer(x2d, bsz, seq, pre_norm[i], post_norm[i], swa_w_in[j], swa_sinks[j],
                             swa_w_out[j])
    return x2d.reshape(bsz, seq, d)
```

```python
import functools
import math

import jax
import jax.numpy as jnp
from jax import lax
from jax.experimental import pallas as pl
from jax.experimental.pallas import tpu as pltpu

F32 = jnp.float32
BF16 = jnp.bfloat16

D_MODEL = 2048
MIX_WIDTH = D_MODEL
RMS_EPS = 1e-6
NEG_INF = -1e30
LOG2E = math.log2(math.e)
LANES = 128

S5_GROUP = 16
S5_GROUPS = MIX_WIDTH // S5_GROUP
S5_STATE = 64
S5_SLAB = 256
S5_SLAB_STATE = (S5_SLAB // S5_GROUP) * S5_STATE

DIFF_HEADS = 16
DIFF_HEAD_DIM = 64
MOBA_HEADS = 16
MOBA_HEAD_DIM = 128
MOBA_BLOCK = 256
MOBA_TOPK = 3
SWA_Q_HEADS = 32
SWA_KV_HEADS = 4
SWA_HEAD_DIM = 64
SWA_WINDOW = 128

VMEM_LIMIT = 56 * 1024 * 1024


def _params(*sem):
    return pltpu.CompilerParams(dimension_semantics=sem, vmem_limit_bytes=VMEM_LIMIT)


def _alibi_slopes(n_heads):
    return 2.0 ** (-8.0 * jnp.arange(1, n_heads + 1, dtype=F32) / n_heads)


def _dot_nt(a, b):
    return lax.dot_general(a, b, (((1,), (1,)), ((), ())), preferred_element_type=F32)


def _rms_normalize(x, gain):
    ms = jnp.mean(x * x, axis=-1, keepdims=True)
    return x * lax.rsqrt(ms + RMS_EPS) * gain


def _norm_matmul_kernel(x_ref, g_ref, w_ref, o_ref, h_ref):
    @pl.when(pl.program_id(1) == 0)
    def _():
        h_ref[...] = _rms_normalize(x_ref[...], g_ref[...]).astype(BF16)

    o_ref[...] = jnp.dot(h_ref[...], w_ref[...], preferred_element_type=F32).astype(o_ref.dtype)


def _norm_matmul(x2d, gain, w, *, seq=None, tm=1024, tn=1024):
    t, d = x2d.shape
    n = w.shape[1]
    tm = min(tm, t if seq is None else seq)
    tn = min(tn, n)
    assert t % tm == 0 and n % tn == 0
    if seq is None:
        out_shape = (t, n)
        out_map = lambda i, j: (i, j)
    else:
        assert seq % tm == 0
        tps, npt = seq // tm, n // tn
        out_shape = (seq, (t // seq) * n)
        out_map = lambda i, j: (i % tps, (i // tps) * npt + j)
    return pl.pallas_call(
        _norm_matmul_kernel,
        grid=(t // tm, n // tn),
        in_specs=[
            pl.BlockSpec((tm, d), lambda i, j: (i, 0)),
            pl.BlockSpec((1, d), lambda i, j: (0, 0)),
            pl.BlockSpec((d, tn), lambda i, j: (0, j)),
        ],
        out_specs=pl.BlockSpec((tm, tn), out_map),
        out_shape=jax.ShapeDtypeStruct(out_shape, BF16),
        scratch_shapes=[pltpu.VMEM((tm, d), BF16)],
        compiler_params=_params("parallel", "arbitrary"),
        name="norm_matmul",
    )(x2d, gain.reshape(1, d).astype(F32), w)


def _norm_matmul_t_kernel(x_ref, g_ref, wt_ref, o_ref, h_ref):
    @pl.when(pl.program_id(1) == 0)
    def _():
        h_ref[...] = _rms_normalize(x_ref[...], g_ref[...]).astype(BF16)

    o_ref[0] = _dot_nt(wt_ref[...], h_ref[...]).astype(o_ref.dtype)


def _norm_matmul_t(x2d, gain, wt, seq, *, tm=1024, tn=1024):
    t, d = x2d.shape
    n = wt.shape[0]
    tm = min(tm, seq)
    tn = min(tn, n)
    assert seq % tm == 0 and n % tn == 0
    tps = seq // tm
    return pl.pallas_call(
        _norm_matmul_t_kernel,
        grid=(t // tm, n // tn),
        in_specs=[
            pl.BlockSpec((tm, d), lambda i, j: (i, 0)),
            pl.BlockSpec((1, d), lambda i, j: (0, 0)),
            pl.BlockSpec((tn, d), lambda i, j: (j, 0)),
        ],
        out_specs=pl.BlockSpec((1, tn, tm), lambda i, j: (i // tps, j, i % tps)),
        out_shape=jax.ShapeDtypeStruct((t // seq, n, seq), BF16),
        scratch_shapes=[pltpu.VMEM((tm, d), BF16)],
        compiler_params=_params("parallel", "arbitrary"),
        name="norm_matmul_t",
    )(x2d, gain.reshape(1, d).astype(F32), wt)


def _gate_out_kernel(o_ref, z_ref, x_ref, w_ref, g_ref, out_ref):
    z = z_ref[...].astype(F32)
    y = (o_ref[...].astype(F32) * (z * jax.nn.sigmoid(z))).astype(BF16)
    r = jnp.dot(y, w_ref[...], preferred_element_type=F32)
    out_ref[...] = x_ref[...] + _rms_normalize(r, g_ref[...])


def _gate_out(o2d, zsrc, z_block, x2d, w, gain, bsz, seq, *, z_time_major=False, tm=256):
    t, e = o2d.shape
    d = w.shape[1]
    tm = min(tm, seq)
    assert seq % tm == 0
    tps = seq // tm
    row_map = lambda b, i: (b * tps + i, 0)
    if z_time_major:
        blocks_per_batch = zsrc.shape[1] // bsz // e
        z_map = lambda b, i: (i, b * blocks_per_batch + z_block)
    else:
        z_map = lambda b, i: (b * tps + i, z_block)
    return pl.pallas_call(
        _gate_out_kernel,
        grid=(bsz, tps),
        in_specs=[
            pl.BlockSpec((tm, e), row_map),
            pl.BlockSpec((tm, e), z_map),
            pl.BlockSpec((tm, d), row_map),
            pl.BlockSpec((e, d), lambda b, i: (0, 0)),
            pl.BlockSpec((1, d), lambda b, i: (0, 0)),
        ],
        out_specs=pl.BlockSpec((tm, d), row_map),
        out_shape=jax.ShapeDtypeStruct((t, d), F32),
        compiler_params=_params("parallel", "parallel"),
        name="gate_out",
    )(o2d, zsrc, x2d, w, gain.reshape(1, d).astype(F32))


BIG = 1e30


def _softmax_step_t(s, c, ok, vt, m_ref, l_ref, acc_ref, hh):
    m_old = m_ref[hh]
    mt = jnp.max(s, axis=0, keepdims=True) + c
    if ok is not None:
        mt = jnp.where(ok, mt, NEG_INF)
    m_new = jnp.maximum(m_old, mt)
    alpha = jnp.exp2(m_old - m_new)
    r = m_new - c
    if ok is not None:
        r = jnp.where(ok, r, BIG)
    p = jnp.exp2(s - r)
    l_ref[hh] = alpha * l_ref[hh] + jnp.sum(p, axis=0, keepdims=True)
    acc_ref[hh] = alpha * acc_ref[hh] + jnp.dot(vt, p.astype(BF16), preferred_element_type=F32)
    m_ref[hh] = m_new


def _pipelined_tiles(n_past, qk_all, fin_all):
    qk_all(0, 0)

    def pair(p, carry):
        j = 2 * p
        qk_all(j + 1, 1)
        fin_all(j, 0, False)
        qk_all(j + 2, 0)
        fin_all(j + 1, 1, False)
        return carry

    lax.fori_loop(0, n_past // 2, pair, 0)

    @pl.when(n_past % 2 == 1)
    def _():
        qk_all(n_past, 1)
        fin_all(n_past - 1, 0, False)
        fin_all(n_past, 1, True)

    @pl.when(n_past % 2 == 0)
    def _():
        fin_all(n_past, 0, True)


def _diff_attn_kernel(slopes_ref, lam_ref, q_ref, k_ref, vt_ref, g_ref, o_ref,
                      m_ref, l_ref, acc_ref, bias_ref, qq_ref, s_ref, *, tq, hp, out_scale):
    hg = pl.program_id(1)
    i = pl.program_id(2)
    lam = lam_ref[0]
    d = DIFF_HEAD_DIM
    w = 2 * d
    reps = 2 * tq // LANES

    m_ref[...] = jnp.full_like(m_ref, NEG_INF)
    l_ref[...] = jnp.zeros_like(l_ref)
    acc_ref[...] = jnp.zeros_like(acc_ref)

    lane = lax.broadcasted_iota(jnp.int32, (tq, w), 1)
    krow = lax.broadcasted_iota(jnp.int32, (tq, LANES), 0).astype(F32)
    slope2 = [slopes_ref[hg * hp + hh] * LOG2E for hh in range(hp)]
    for hh in range(hp):
        q = q_ref[0, :, hh * w:(hh + 1) * w].astype(F32) * (d ** -0.5 * LOG2E)
        qq_ref[hh, :tq] = jnp.where(lane < d, q, 0.0).astype(BF16)
        qq_ref[hh, tq:] = jnp.where(lane >= d, q, 0.0).astype(BF16)
        bias_ref[hh] = slope2[hh] * krow

    def qk_all(j, buf):
        start = pl.multiple_of(j * tq, tq)
        for hh in range(hp):
            k = k_ref[0, pl.ds(start, tq), hh * w:(hh + 1) * w]
            s_ref[buf, hh] = (_dot_nt(k, qq_ref[hh])
                              + jnp.concatenate([bias_ref[hh]] * reps, axis=1))

    def fin_all(j, buf, diagonal):
        start = pl.multiple_of(j * tq, tq)
        for hh in range(hp):
            vt = vt_ref[0, hh * w:(hh + 1) * w, pl.ds(start, tq)]
            s = s_ref[buf, hh]
            if diagonal:
                kr = lax.broadcasted_iota(jnp.int32, (tq, 2 * tq), 0)
                qc = lax.broadcasted_iota(jnp.int32, (tq, 2 * tq), 1)
                qc = jnp.where(qc >= tq, qc - tq, qc)
                s = jnp.where(qc >= kr, s, NEG_INF)
            c = slope2[hh] * ((j - i) * tq).astype(F32)
            _softmax_step_t(s, c, None, vt, m_ref, l_ref, acc_ref, hh)

    _pipelined_tiles(i, qk_all, fin_all)

    g = jnp.concatenate([g_ref[...]] * (tq // LANES), axis=1)
    for hh in range(hp):
        a = acc_ref[hh] / l_ref[hh]
        o = a[:, :tq] - lam * a[:, tq:]
        ms = jnp.mean(o * o, axis=0, keepdims=True)
        o = o * lax.rsqrt(ms + RMS_EPS) * g * out_scale
        o_ref[0, :, hh * w:(hh + 1) * w] = o.T.astype(o_ref.dtype)


def _diff_attention(proj, vt, slopes, lam, subln, out_scale, *, tq=256, hp=2):
    b, s, _ = proj.shape
    nh = DIFF_HEADS
    w = 2 * DIFF_HEAD_DIM
    tq = min(tq, s)
    assert s % tq == 0 and tq % LANES == 0 and nh % hp == 0
    ng = nh // hp
    smem = pl.BlockSpec(memory_space=pltpu.SMEM)
    g = jnp.broadcast_to(subln.astype(F32)[:, None], (w, LANES))
    return pl.pallas_call(
        functools.partial(_diff_attn_kernel, tq=tq, hp=hp, out_scale=out_scale),
        grid=(b, ng, s // tq),
        in_specs=[
            smem, smem,
            pl.BlockSpec((1, tq, hp * w), lambda bi, h, i: (bi, i, h)),
            pl.BlockSpec((1, s, hp * w), lambda bi, h, i: (bi, 0, ng + h)),
            pl.BlockSpec((1, hp * w, s), lambda bi, h, i: (bi, h, 0)),
            pl.BlockSpec((w, LANES), lambda bi, h, i: (0, 0)),
        ],
        out_specs=pl.BlockSpec((1, tq, hp * w), lambda bi, h, i: (bi, i, h)),
        out_shape=jax.ShapeDtypeStruct((b, s, MIX_WIDTH), BF16),
        scratch_shapes=[pltpu.VMEM((hp, 1, 2 * tq), F32), pltpu.VMEM((hp, 1, 2 * tq), F32),
                        pltpu.VMEM((hp, w, 2 * tq), F32), pltpu.VMEM((hp, tq, LANES), F32),
                        pltpu.VMEM((hp, 2 * tq, w), BF16), pltpu.VMEM((2, hp, tq, 2 * tq), F32)],
        compiler_params=_params("parallel", "parallel", "arbitrary"),
        name="diff_attn",
    )(slopes, lam, proj, proj, vt, g)


def _moba_attn_kernel(slopes_ref, q_ref, k_ref, vt_ref, o_ref, m_ref, l_ref, acc_ref, bias_ref,
                      km_ref, qs_ref, sel_ref, s_ref, *, nb, hp):
    hg = pl.program_id(1)
    n = pl.program_id(2)
    blk = MOBA_BLOCK
    dh = MOBA_HEAD_DIM
    reps = blk // LANES

    @pl.when(n == 0)
    def _():
        for hh in range(hp):
            kh = k_ref[0, :, hh * dh:(hh + 1) * dh].astype(F32)
            km_ref[hh] = jnp.mean(kh.reshape(nb, blk, dh), axis=1)

    m_ref[...] = jnp.full_like(m_ref, NEG_INF)
    l_ref[...] = jnp.zeros_like(l_ref)
    acc_ref[...] = jnp.zeros_like(acc_ref)

    bidx = lax.broadcasted_iota(jnp.int32, (nb, blk), 0)
    krow = lax.broadcasted_iota(jnp.int32, (blk, LANES), 0).astype(F32)
    slope2 = [slopes_ref[hg * hp + hh] * LOG2E for hh in range(hp)]
    for hh in range(hp):
        q = q_ref[0, :, hh * dh:(hh + 1) * dh]
        qs_ref[hh] = (q.astype(F32) * (dh ** -0.5 * LOG2E)).astype(BF16)
        bias_ref[hh] = slope2[hh] * krow

        kmean = km_ref[hh]
        km_hi = kmean.astype(BF16)
        km_lo = (kmean - km_hi.astype(F32)).astype(BF16)
        gate = _dot_nt(km_hi, q) + _dot_nt(km_lo, q)

        rank = jnp.zeros((nb, blk), jnp.int32)
        for jp in range(nb):
            gj = gate[jp:jp + 1, :]
            before = (gj > gate) | ((gj == gate) & (jp < bidx))
            rank = rank + jnp.where(before, 1, 0) * (jp < n).astype(jnp.int32)
        sel_ref[hh] = jnp.where((rank < MOBA_TOPK) & (bidx < n), 1.0, 0.0)

    def qk_all(j, buf):
        start = pl.multiple_of(j * blk, blk)
        for hh in range(hp):
            k = k_ref[0, pl.ds(start, blk), hh * dh:(hh + 1) * dh]
            s_ref[buf, hh] = (_dot_nt(k, qs_ref[hh])
                              + jnp.concatenate([bias_ref[hh]] * reps, axis=1))

    def fin_all(j, buf, diagonal):
        start = pl.multiple_of(j * blk, blk)
        for hh in range(hp):
            vt = vt_ref[0, hh * dh:(hh + 1) * dh, pl.ds(start, blk)]
            s = s_ref[buf, hh]
            c = slope2[hh] * ((j - n) * blk).astype(F32)
            if diagonal:
                kr = lax.broadcasted_iota(jnp.int32, (blk, blk), 0)
                qc = lax.broadcasted_iota(jnp.int32, (blk, blk), 1)
                s = jnp.where(qc >= kr, s, NEG_INF)
                ok = None
            else:
                ok = jnp.max(jnp.where(bidx == j, sel_ref[hh], 0.0), axis=0, keepdims=True) > 0.0
            _softmax_step_t(s, c, ok, vt, m_ref, l_ref, acc_ref, hh)

    _pipelined_tiles(n, qk_all, fin_all)

    for hh in range(hp):
        o_ref[0, :, hh * dh:(hh + 1) * dh] = (acc_ref[hh] / l_ref[hh]).T.astype(o_ref.dtype)


def _moba_attention(proj, vt, slopes, *, hp=4):
    b, s, _ = proj.shape
    nh = MOBA_HEADS
    dh = MOBA_HEAD_DIM
    blk = MOBA_BLOCK
    assert s % blk == 0 and nh % hp == 0
    nb = s // blk
    ng = nh // hp
    smem = pl.BlockSpec(memory_space=pltpu.SMEM)
    return pl.pallas_call(
        functools.partial(_moba_attn_kernel, nb=nb, hp=hp),
        grid=(b, ng, nb),
        in_specs=[
            smem,
            pl.BlockSpec((1, blk, hp * dh), lambda bi, h, i: (bi, i, h)),
            pl.BlockSpec((1, s, hp * dh), lambda bi, h, i: (bi, 0, ng + h)),
            pl.BlockSpec((1, hp * dh, s), lambda bi, h, i: (bi, h, 0)),
        ],
        out_specs=pl.BlockSpec((1, blk, hp * dh), lambda bi, h, i: (bi, i, h)),
        out_shape=jax.ShapeDtypeStruct((b, s, MIX_WIDTH), BF16),
        scratch_shapes=[pltpu.VMEM((hp, 1, blk), F32), pltpu.VMEM((hp, 1, blk), F32),
                        pltpu.VMEM((hp, dh, blk), F32), pltpu.VMEM((hp, blk, LANES), F32),
                        pltpu.VMEM((hp, nb, dh), F32), pltpu.VMEM((hp, blk, dh), BF16),
                        pltpu.VMEM((hp, nb, blk), F32), pltpu.VMEM((2, hp, blk, blk), F32)],
        compiler_params=_params("parallel", "parallel", "arbitrary"),
        name="moba_attn",
    )(slopes, proj, proj, vt)


def _swa_attn_kernel(slopes_ref, sinks_ref, q_ref, k_ref, v_ref, o_ref):
    i = pl.program_id(1)
    win = SWA_WINDOW
    dh = SWA_HEAD_DIM
    grp = SWA_Q_HEADS // SWA_KV_HEADS
    tq = win

    kstart = pl.multiple_of(jnp.maximum(i - 1, 0) * win, win)
    qpos = i * win + lax.broadcasted_iota(jnp.int32, (2 * tq, 1), 0) % tq
    kpos = kstart + lax.broadcasted_iota(jnp.int32, (1, 2 * win), 1)
    dist = qpos - kpos
    valid = (dist >= 0) & (dist < win)
    distf = dist.astype(F32)
    top = lax.broadcasted_iota(jnp.int32, (2 * tq, 1), 0) < tq
    lane = lax.broadcasted_iota(jnp.int32, (tq, 2 * dh), 1)

    for kvh in range(SWA_KV_HEADS):
        k = k_ref[0, pl.ds(kstart, 2 * win), kvh * 2 * dh:(kvh + 1) * 2 * dh]
        v = v_ref[0, pl.ds(kstart, 2 * win), kvh * 2 * dh:(kvh + 1) * 2 * dh]
        for pair in range(grp // 2):
            h0 = kvh * grp + 2 * pair
            q = q_ref[0, :, h0 * dh:(h0 + 2) * dh].astype(F32) * (dh ** -0.5)
            qq = jnp.concatenate([jnp.where(lane < dh, q, 0.0), jnp.where(lane >= dh, q, 0.0)],
                                 axis=0).astype(BF16)
            slope = jnp.where(top, slopes_ref[h0], slopes_ref[h0 + 1])
            sink = jnp.where(top, sinks_ref[h0], sinks_ref[h0 + 1])
            s = _dot_nt(qq, k) - slope * distf
            s = jnp.where(valid, s, NEG_INF)
            m = jnp.maximum(jnp.max(s, axis=-1, keepdims=True), sink)
            p = jnp.exp(s - m)
            denom = jnp.sum(p, axis=-1, keepdims=True) + jnp.exp(sink - m)
            o2 = jnp.dot(p.astype(BF16), v, preferred_element_type=F32) / denom
            o = jnp.where(lane < dh, o2[:tq], o2[tq:])
            o_ref[0, :, h0 * dh:(h0 + 2) * dh] = o.astype(o_ref.dtype)


def _swa_attention(proj, slopes, sinks):
    b, s, _ = proj.shape
    win = SWA_WINDOW
    e = MIX_WIDTH
    kvw = SWA_KV_HEADS * 2 * SWA_HEAD_DIM
    assert s % win == 0 and s >= 2 * win
    smem = pl.BlockSpec(memory_space=pltpu.SMEM)
    return pl.pallas_call(
        _swa_attn_kernel,
        grid=(b, s // win),
        in_specs=[
            smem, smem,
            pl.BlockSpec((1, win, e), lambda bi, i: (bi, i, 0)),
            pl.BlockSpec((1, s, kvw), lambda bi, i: (bi, 0, 2 * e // kvw)),
            pl.BlockSpec((1, s, kvw), lambda bi, i: (bi, 0, 2 * e // kvw + 1)),
        ],
        out_specs=pl.BlockSpec((1, win, e), lambda bi, i: (bi, i, 0)),
        out_shape=jax.ShapeDtypeStruct((b, s, e), BF16),
        compiler_params=_params("parallel", "arbitrary"),
        name="swa_attn",
    )(slopes, sinks, proj, proj, proj)


def _s5_ssm_kernel(u_ref, bb_ref, a_ref, cc_ref, d_ref, y_ref, st_ref, bu_ref, xb_ref,
                   *, bsz, tt, chunk_lanes):
    ns = S5_SLAB_STATE
    cw = chunk_lanes

    @pl.when(pl.program_id(1) == 0)
    def _():
        st_ref[...] = jnp.zeros_like(st_ref)

    u = u_ref[...].reshape(tt * bsz, S5_SLAB)
    y = d_ref[0] * u.astype(F32)
    for c in range(ns // cw):
        lo = c * cw
        bu_ref[2 * c] = jnp.dot(u, bb_ref[0, :, lo:lo + cw], preferred_element_type=F32)
        bu_ref[2 * c + 1] = jnp.dot(u, bb_ref[0, :, ns + lo:ns + lo + cw],
                                    preferred_element_type=F32)
        ar = jnp.broadcast_to(a_ref[0, 0:1, lo:lo + cw], (bsz, cw))
        ai = jnp.broadcast_to(a_ref[0, 1:2, lo:lo + cw], (bsz, cw))
        xr = st_ref[0, :, lo:lo + cw]
        xi = st_ref[1, :, lo:lo + cw]
        for t in range(tt):
            rows = slice(t * bsz, (t + 1) * bsz)
            xr, xi = (ar * xr - ai * xi + bu_ref[2 * c, rows, :],
                      ar * xi + ai * xr + bu_ref[2 * c + 1, rows, :])
            xb_ref[2 * c, rows, :] = xr.astype(BF16)
            xb_ref[2 * c + 1, rows, :] = xi.astype(BF16)
        st_ref[0, :, lo:lo + cw] = xr
        st_ref[1, :, lo:lo + cw] = xi
        y = (y + jnp.dot(xb_ref[2 * c], cc_ref[0, lo:lo + cw], preferred_element_type=F32)
             + jnp.dot(xb_ref[2 * c + 1], cc_ref[0, ns + lo:ns + lo + cw],
                       preferred_element_type=F32))
    y_ref[...] = y.reshape(tt, bsz, S5_SLAB).astype(y_ref.dtype)


def _s5_ssm(proj_tm, bd_b, a_bar, bd_c, d_skip, *, tt=32, chunk_lanes=512):
    s, b, _ = proj_tm.shape
    tt = min(tt, s)
    assert s % tt == 0 and b % 16 == 0
    n_slab = MIX_WIDTH // S5_SLAB
    ns = S5_SLAB_STATE
    n_buf = 2 * ns // chunk_lanes
    return pl.pallas_call(
        functools.partial(_s5_ssm_kernel, bsz=b, tt=tt, chunk_lanes=chunk_lanes),
        grid=(n_slab, s // tt),
        in_specs=[
            pl.BlockSpec((tt, b, S5_SLAB), lambda sl, t: (t, 0, sl)),
            pl.BlockSpec((1, S5_SLAB, 2 * ns), lambda sl, t: (sl, 0, 0)),
            pl.BlockSpec((1, 2, ns), lambda sl, t: (sl, 0, 0)),
            pl.BlockSpec((1, 2 * ns, S5_SLAB), lambda sl, t: (sl, 0, 0)),
            pl.BlockSpec((1, 1, S5_SLAB), lambda sl, t: (sl, 0, 0)),
        ],
        out_specs=pl.BlockSpec((tt, b, S5_SLAB), lambda sl, t: (t, 0, sl)),
        out_shape=jax.ShapeDtypeStruct((s, b, MIX_WIDTH), BF16),
        scratch_shapes=[pltpu.VMEM((2, b, ns), F32),
                        pltpu.VMEM((n_buf, b * tt, chunk_lanes), F32),
                        pltpu.VMEM((n_buf, b * tt, chunk_lanes), BF16)],
        compiler_params=_params("parallel", "arbitrary"),
        name="s5_ssm",
    )(proj_tm, bd_b, a_bar, bd_c, d_skip)


def _s5_discretize(a_re, a_im, log_dt, b_re, b_im, c_re, c_im, d_skip):
    lr, li = a_re.astype(F32), a_im.astype(F32)
    dt = jnp.exp(log_dt.astype(F32))[:, None]
    mag = jnp.exp(lr * dt)
    ab_re, ab_im = mag * jnp.cos(li * dt), mag * jnp.sin(li * dt)
    den = lr * lr + li * li
    f_re = ((ab_re - 1.0) * lr + ab_im * li) / den
    f_im = (ab_im * lr - (ab_re - 1.0) * li) / den
    br, bi = b_re.astype(F32), b_im.astype(F32)
    bb_re = f_re[..., None] * br - f_im[..., None] * bi
    bb_im = f_re[..., None] * bi + f_im[..., None] * br
    n_slab = MIX_WIDTH // S5_SLAB
    gs = S5_SLAB // S5_GROUP
    eye = jnp.eye(gs, dtype=F32)

    def pack_b(m):
        m = m.reshape(n_slab, gs, S5_STATE, S5_GROUP)
        return jnp.einsum('sgpc,gh->sgchp', m, eye).reshape(n_slab, S5_SLAB, S5_SLAB_STATE)

    def pack_c(m):
        m = m.reshape(n_slab, gs, S5_GROUP, S5_STATE)
        return jnp.einsum('sgcp,gh->shpgc', m, eye).reshape(n_slab, S5_SLAB_STATE, S5_SLAB)

    bd_b = jnp.concatenate([pack_b(bb_re), pack_b(bb_im)], axis=2).astype(BF16)
    bd_c = jnp.concatenate([pack_c(c_re.astype(F32)), pack_c(-c_im.astype(F32))], axis=1).astype(BF16)
    a_bar = jnp.stack([ab_re.reshape(n_slab, S5_SLAB_STATE), ab_im.reshape(n_slab, S5_SLAB_STATE)],
                      axis=1)
    return bd_b, a_bar, bd_c, d_skip.astype(F32).reshape(n_slab, 1, S5_SLAB)


def _glu_kernel(y_ref, w_ref, b_ref, o_ref, g_ref, *, tn):
    j = pl.program_id(2)

    @pl.when(j == 0)
    def _():
        g_ref[...] = jax.nn.gelu(y_ref[...].astype(F32)).astype(BF16)

    r = jnp.dot(g_ref[...], w_ref[...], preferred_element_type=F32) + b_ref[...]
    g = g_ref[:, pl.ds(pl.multiple_of(j * tn, tn), tn)].astype(F32)
    o_ref[...] = (g * jax.nn.sigmoid(r)).astype(o_ref.dtype)


def _glu(y_tm, w, bias, bsz, *, tm=1024, tn=1024):
    s = y_tm.shape[0]
    e = w.shape[0]
    tm = min(tm, s)
    assert s % tm == 0 and e % tn == 0
    tps = s // tm
    return pl.pallas_call(
        functools.partial(_glu_kernel, tn=tn),
        grid=(bsz, tps, e // tn),
        in_specs=[
            pl.BlockSpec((tm, e), lambda b, i, j: (i, b)),
            pl.BlockSpec((e, tn), lambda b, i, j: (0, j)),
            pl.BlockSpec((1, tn), lambda b, i, j: (0, j)),
        ],
        out_specs=pl.BlockSpec((tm, tn), lambda b, i, j: (b * tps + i, j)),
        out_shape=jax.ShapeDtypeStruct((bsz * s, e), BF16),
        scratch_shapes=[pltpu.VMEM((tm, e), BF16)],
        compiler_params=_params("parallel", "parallel", "arbitrary"),
        name="s5_glu",
    )(y_tm, w, bias.reshape(1, e).astype(F32))


def _s5_layer(x2d, bsz, seq, pre_g, post_g, w_in, a_re, a_im, log_dt, b_re, b_im, c_re, c_im,
              d_skip, w_glu, b_glu, w_out):
    e = MIX_WIDTH
    proj = _norm_matmul(x2d, pre_g, w_in.astype(BF16), seq=seq)
    bd_b, a_bar, bd_c, d3 = _s5_discretize(a_re, a_im, log_dt, b_re, b_im, c_re, c_im, d_skip)
    y = _s5_ssm(proj.reshape(seq, bsz, 2 * e), bd_b, a_bar, bd_c, d3)
    y = _glu(y.reshape(seq, bsz * e), w_glu.astype(BF16), b_glu, bsz)
    return _gate_out(y, proj, 1, x2d, w_out.astype(BF16), post_g, bsz, seq, z_time_major=True)


def _attn_projections(x2d, seq, pre_g, w_in):
    e = MIX_WIDTH
    w_qkz = jnp.concatenate([w_in[:, :2 * e], w_in[:, 3 * e:]], axis=1).astype(BF16)
    proj = _norm_matmul(x2d, pre_g, w_qkz)
    vt = _norm_matmul_t(x2d, pre_g, w_in[:, 2 * e:3 * e].T.astype(BF16), seq)
    return proj, vt


def _diff_layer(x2d, bsz, seq, layer_idx, pre_g, post_g, w_in, lq1, lk1, lq2, lk2, subln, w_out):
    e = MIX_WIDTH
    proj, vt = _attn_projections(x2d, seq, pre_g, w_in)
    lam_init = 0.8 - 0.6 * math.exp(-0.3 * layer_idx)
    lam = (jnp.exp(jnp.sum(lq1.astype(F32) * lk1.astype(F32)))
           - jnp.exp(jnp.sum(lq2.astype(F32) * lk2.astype(F32))) + lam_init).reshape(1)
    o = _diff_attention(proj.reshape(bsz, seq, 3 * e), vt, _alibi_slopes(DIFF_HEADS), lam, subln,
                        1.0 - lam_init)
    return _gate_out(o.reshape(bsz * seq, e), proj, 2, x2d, w_out.astype(BF16), post_g, bsz, seq)


def _moba_layer(x2d, bsz, seq, pre_g, post_g, w_in, w_out):
    e = MIX_WIDTH
    proj, vt = _attn_projections(x2d, seq, pre_g, w_in)
    o = _moba_attention(proj.reshape(bsz, seq, 3 * e), vt, _alibi_slopes(MOBA_HEADS))
    return _gate_out(o.reshape(bsz * seq, e), proj, 2, x2d, w_out.astype(BF16), post_g, bsz, seq)


def _swa_layer(x2d, bsz, seq, pre_g, post_g, w_in, sinks, w_out):
    e = MIX_WIDTH
    dh = SWA_HEAD_DIM
    nqc, nkc = SWA_Q_HEADS * dh, SWA_KV_HEADS * dh
    wq, wk = w_in[:, :nqc], w_in[:, nqc:nqc + nkc]
    wv, wz = w_in[:, nqc + nkc:nqc + 2 * nkc], w_in[:, nqc + 2 * nkc:]

    def dup(w):
        w = w.reshape(-1, SWA_KV_HEADS, 1, dh)
        return jnp.broadcast_to(w, (w.shape[0], SWA_KV_HEADS, 2, dh)).reshape(-1, 2 * nkc)

    w_cat = jnp.concatenate([wq, wz, dup(wk), dup(wv)], axis=1).astype(BF16)
    proj = _norm_matmul(x2d, pre_g, w_cat, tn=512)
    o = _swa_attention(proj.reshape(bsz, seq, w_cat.shape[1]), _alibi_slopes(SWA_Q_HEADS),
                       sinks.astype(F32))
    return _gate_out(o.reshape(bsz * seq, e), proj, 1, x2d, w_out.astype(BF16), post_g, bsz, seq)


def kernel(x, pre_norm, post_norm, s5_w_in, s5_a_re, s5_a_im, s5_log_dt, s5_b_re, s5_b_im, s5_c_re, s5_c_im, s5_d, s5_w_glu, s5_b_glu, s5_w_out, diff_w_in, diff_lq1, diff_lk1, diff_lq2, diff_lk2, diff_subln, diff_w_out, moba_w_in, moba_w_out, swa_w_in, swa_sinks, swa_w_out):
    bsz, seq, d = x.shape
    depth = pre_norm.shape[0]
    x2d = x.reshape(bsz * seq, d)
    for i in range(depth):
        kind, j = i % 4, i // 4
        if kind == 0:
            x2d = _s5_layer(x2d, bsz, seq, pre_norm[i], post_norm[i], s5_w_in[j], s5_a_re[j],
                            s5_a_im[j], s5_log_dt[j], s5_b_re[j], s5_b_im[j], s5_c_re[j],
                            s5_c_im[j], s5_d[j], s5_w_glu[j], s5_b_glu[j], s5_w_out[j])
        elif kind == 1:
            x2d = _diff_layer(x2d, bsz, seq, i, pre_norm[i], post_norm[i], diff_w_in[j],
                              diff_lq1[j], diff_lk1[j], diff_lq2[j], diff_lk2[j], diff_subln[j],
                              diff_w_out[j])
        elif kind == 2:
            x2d = _moba_layer(x2d, bsz, seq, pre_norm[i], post_norm[i], moba_w_in[j], moba_w_out[j])
        else:
            x2d = _swa_layer(x2d, bsz, seq, pre_norm[i], post_norm[i], swa_w_in[j], swa_sinks[j],
                             swa_w_out[j])
    return x2d.reshape(bsz, seq, d)
```

```python
import functools
import math

import jax
import jax.numpy as jnp
from jax import lax
from jax.experimental import pallas as pl
from jax.experimental.pallas import tpu as pltpu

F32 = jnp.float32
BF16 = jnp.bfloat16

D_MODEL = 2048
MIX_WIDTH = D_MODEL
RMS_EPS = 1e-6
NEG_INF = -1e30
LOG2E = math.log2(math.e)
LANES = 128

S5_GROUP = 16
S5_GROUPS = MIX_WIDTH // S5_GROUP
S5_STATE = 64
S5_SLAB = 256
S5_SLAB_STATE = (S5_SLAB // S5_GROUP) * S5_STATE

DIFF_HEADS = 16
DIFF_HEAD_DIM = 64
MOBA_HEADS = 16
MOBA_HEAD_DIM = 128
MOBA_BLOCK = 256
MOBA_TOPK = 3
SWA_Q_HEADS = 32
SWA_KV_HEADS = 4
SWA_HEAD_DIM = 64
SWA_WINDOW = 128

VMEM_LIMIT = 56 * 1024 * 1024


def _params(*sem):
    return pltpu.CompilerParams(dimension_semantics=sem, vmem_limit_bytes=VMEM_LIMIT)


def _alibi_slopes(n_heads):
    return 2.0 ** (-8.0 * jnp.arange(1, n_heads + 1, dtype=F32) / n_heads)


def _dot_nt(a, b):
    return lax.dot_general(a, b, (((1,), (1,)), ((), ())), preferred_element_type=F32)


def _rms_normalize(x, gain):
    ms = jnp.mean(x * x, axis=-1, keepdims=True)
    return x * lax.rsqrt(ms + RMS_EPS) * gain


def _prenorm_kernel(x_ref, g_ref, h_ref):
    h_ref[...] = _rms_normalize(x_ref[...], g_ref[...]).astype(BF16)


def _prenorm(x2d, gain, *, tm=512):
    t, d = x2d.shape
    tm = min(tm, t)
    assert t % tm == 0
    return pl.pallas_call(
        _prenorm_kernel,
        grid=(t // tm,),
        in_specs=[pl.BlockSpec((tm, d), lambda i: (i, 0)), pl.BlockSpec((1, d), lambda i: (0, 0))],
        out_specs=pl.BlockSpec((tm, d), lambda i: (i, 0)),
        out_shape=jax.ShapeDtypeStruct((t, d), BF16),
        compiler_params=_params("parallel"),
        name="prenorm",
    )(x2d, gain.reshape(1, d).astype(F32))


def _col_tile(n, cap=2048):
    return max(t for t in range(256, cap + 1, 256) if n % t == 0)


def _matmul_kernel(h_ref, w_ref, o_ref):
    o_ref[...] = jnp.dot(h_ref[...], w_ref[...], preferred_element_type=F32).astype(o_ref.dtype)


def _in_proj(h2d, w, *, seq=None, tm=1024):
    t, d = h2d.shape
    n = w.shape[1]
    tm = min(tm, t if seq is None else seq)
    tn = _col_tile(n)
    assert t % tm == 0
    if seq is None:
        out_shape = (t, n)
        out_map = lambda j, i: (i, j)
    else:
        assert seq % tm == 0
        tps, npt = seq // tm, n // tn
        out_shape = (seq, (t // seq) * n)
        out_map = lambda j, i: (i % tps, (i // tps) * npt + j)
    return pl.pallas_call(
        _matmul_kernel,
        grid=(n // tn, t // tm),
        in_specs=[
            pl.BlockSpec((tm, d), lambda j, i: (i, 0)),
            pl.BlockSpec((d, tn), lambda j, i: (0, j)),
        ],
        out_specs=pl.BlockSpec((tm, tn), out_map),
        out_shape=jax.ShapeDtypeStruct(out_shape, BF16),
        compiler_params=_params("parallel", "parallel"),
        name="in_proj",
    )(h2d, w)


def _matmul_t_kernel(h_ref, wt_ref, o_ref):
    o_ref[0] = _dot_nt(wt_ref[...], h_ref[...]).astype(o_ref.dtype)


def _in_proj_t(h2d, wt, seq, *, tm=1024):
    t, d = h2d.shape
    n = wt.shape[0]
    tm = min(tm, seq)
    tn = _col_tile(n)
    assert seq % tm == 0
    tps = seq // tm
    return pl.pallas_call(
        _matmul_t_kernel,
        grid=(n // tn, t // tm),
        in_specs=[
            pl.BlockSpec((tm, d), lambda j, i: (i, 0)),
            pl.BlockSpec((tn, d), lambda j, i: (j, 0)),
        ],
        out_specs=pl.BlockSpec((1, tn, tm), lambda j, i: (i // tps, j, i % tps)),
        out_shape=jax.ShapeDtypeStruct((t // seq, n, seq), BF16),
        compiler_params=_params("parallel", "parallel"),
        name="in_proj_t",
    )(h2d, wt)


def _gate_out_kernel(o_ref, z_ref, x_ref, w_ref, g_ref, gn_ref, out_ref, *h_ref, chunks):
    cm = o_ref.shape[0] // chunks
    for c in range(chunks):
        rows = slice(c * cm, (c + 1) * cm)
        z = z_ref[rows, :].astype(F32)
        y = (o_ref[rows, :].astype(F32) * (z * jax.nn.sigmoid(z))).astype(BF16)
        r = jnp.dot(y, w_ref[...], preferred_element_type=F32)
        xn = x_ref[rows, :] + _rms_normalize(r, g_ref[...])
        out_ref[rows, :] = xn
        if h_ref:
            h_ref[0][rows, :] = _rms_normalize(xn, gn_ref[...]).astype(BF16)


def _gate_out(o2d, zsrc, z_block, x2d, w, gain, next_gain, bsz, seq, *, z_time_major=False,
              tm=512, chunks=4):
    t, e = o2d.shape
    d = w.shape[1]
    tm = min(tm, seq)
    assert seq % tm == 0 and tm % chunks == 0
    tps = seq // tm
    row_map = lambda b, i: (b * tps + i, 0)
    const = lambda b, i: (0, 0)
    if z_time_major:
        blocks_per_batch = zsrc.shape[1] // bsz // e
        z_map = lambda b, i: (i, b * blocks_per_batch + z_block)
    else:
        z_map = lambda b, i: (b * tps + i, z_block)
    with_h = next_gain is not None
    gn = (next_gain if with_h else gain).reshape(1, d).astype(F32)
    out_specs = [pl.BlockSpec((tm, d), row_map)]
    out_shape = [jax.ShapeDtypeStruct((t, d), F32)]
    if with_h:
        out_specs.append(pl.BlockSpec((tm, d), row_map))
        out_shape.append(jax.ShapeDtypeStruct((t, d), BF16))
    res = pl.pallas_call(
        functools.partial(_gate_out_kernel, chunks=chunks),
        grid=(bsz, tps),
        in_specs=[
            pl.BlockSpec((tm, e), row_map),
            pl.BlockSpec((tm, e), z_map),
            pl.BlockSpec((tm, d), row_map),
            pl.BlockSpec((e, d), const, pipeline_mode=pl.Buffered(1)),
            pl.BlockSpec((1, d), const),
            pl.BlockSpec((1, d), const),
        ],
        out_specs=out_specs,
        out_shape=out_shape,
        compiler_params=_params("parallel", "parallel"),
        name="gate_out",
    )(o2d, zsrc, x2d, w, gain.reshape(1, d).astype(F32), gn)
    return (res[0], res[1]) if with_h else (res[0], None)


BIG = 1e30
SUM_ROWS = 16
BIAS_TERMS = 3


def _alibi_split(slopes):
    rest = slopes * LOG2E
    terms = []
    for _ in range(BIAS_TERMS):
        t = rest.astype(BF16).astype(F32)
        terms.append(t)
        rest = rest - t
    return jnp.stack(terms)


def _query_bias_lanes(sl_ref, h, rows):
    lane = lax.broadcasted_iota(jnp.int32, (rows, LANES), 1)
    out = jnp.zeros((rows, LANES), F32)
    for n in range(BIAS_TERMS):
        out = jnp.where(lane == n, sl_ref[n, h], out)
    return out.astype(BF16)


def _key_bias_lanes(rows):
    lane = lax.broadcasted_iota(jnp.int32, (rows, LANES), 1)
    krow = lax.broadcasted_iota(jnp.int32, (rows, LANES), 0).astype(F32)
    return jnp.where(lane < BIAS_TERMS, krow, 0.0).astype(BF16)


def _softmax_step_t(s, c, ok, vt, m_ref, acc_ref, hh):
    m_old = m_ref[hh]
    mt = jnp.max(s, axis=0, keepdims=True) + c
    if ok is not None:
        mt = jnp.where(ok, mt, NEG_INF)
    m_new = jnp.maximum(m_old, mt)
    alpha = jnp.exp2(m_old - m_new)
    r = m_new - c
    if ok is not None:
        r = jnp.where(ok, r, BIG)
    p = jnp.exp2(s - r).astype(BF16)
    vt1 = jnp.concatenate([vt, jnp.ones((SUM_ROWS, vt.shape[1]), BF16)], axis=0)
    acc_ref[hh] = alpha * acc_ref[hh] + jnp.dot(vt1, p, preferred_element_type=F32)
    m_ref[hh] = m_new


def _pipelined_tiles(n_past, qk_all, fin_all):
    qk_all(0, 0)

    def pair(p, carry):
        j = 2 * p
        qk_all(j + 1, 1)
        fin_all(j, 0, False)
        qk_all(j + 2, 0)
        fin_all(j + 1, 1, False)
        return carry

    lax.fori_loop(0, n_past // 2, pair, 0)

    @pl.when(n_past % 2 == 1)
    def _():
        qk_all(n_past, 1)
        fin_all(n_past - 1, 0, False)
        fin_all(n_past, 1, True)

    @pl.when(n_past % 2 == 0)
    def _():
        fin_all(n_past, 0, True)


def _diff_attn_kernel(sl_ref, lam_ref, q_ref, k_ref, vt_ref, g_ref, o_ref,
                      m_ref, acc_ref, kb_ref, qq_ref, s_ref, *, tq, hp, out_scale):
    hg = pl.program_id(1)
    i = pl.program_id(2)
    lam = lam_ref[0]
    d = DIFF_HEAD_DIM
    w = 2 * d

    m_ref[...] = jnp.full_like(m_ref, NEG_INF)
    acc_ref[...] = jnp.zeros_like(acc_ref)
    kb_ref[...] = _key_bias_lanes(tq)

    lane = lax.broadcasted_iota(jnp.int32, (tq, w), 1)
    slope2 = []
    for hh in range(hp):
        h = hg * hp + hh
        slope2.append(sum(sl_ref[n, h] for n in range(BIAS_TERMS)))
        q = q_ref[0, :, hh * w:(hh + 1) * w].astype(F32) * (d ** -0.5 * LOG2E)
        qq_ref[hh, :tq, :w] = jnp.where(lane < d, q, 0.0).astype(BF16)
        qq_ref[hh, tq:, :w] = jnp.where(lane >= d, q, 0.0).astype(BF16)
        qq_ref[hh, :, w:] = _query_bias_lanes(sl_ref, h, 2 * tq)

    def qk_all(j, buf):
        start = pl.multiple_of(j * tq, tq)
        for hh in range(hp):
            k = k_ref[0, pl.ds(start, tq), hh * w:(hh + 1) * w]
            s_ref[buf, hh] = _dot_nt(jnp.concatenate([k, kb_ref[...]], axis=1), qq_ref[hh])

    def fin_all(j, buf, diagonal):
        start = pl.multiple_of(j * tq, tq)
        for hh in range(hp):
            vt = vt_ref[0, hh * w:(hh + 1) * w, pl.ds(start, tq)]
            s = s_ref[buf, hh]
            if diagonal:
                kr = lax.broadcasted_iota(jnp.int32, (tq, 2 * tq), 0)
                qc = lax.broadcasted_iota(jnp.int32, (tq, 2 * tq), 1)
                qc = jnp.where(qc >= tq, qc - tq, qc)
                s = jnp.where(qc >= kr, s, NEG_INF)
            c = slope2[hh] * ((j - i) * tq).astype(F32)
            _softmax_step_t(s, c, None, vt, m_ref, acc_ref, hh)

    _pipelined_tiles(i, qk_all, fin_all)

    g = jnp.concatenate([g_ref[...]] * (tq // LANES), axis=1)
    for hh in range(hp):
        a = acc_ref[hh, :w] / acc_ref[hh, w:w + 1]
        o = a[:, :tq] - lam * a[:, tq:]
        ms = jnp.mean(o * o, axis=0, keepdims=True)
        o = o * lax.rsqrt(ms + RMS_EPS) * g * out_scale
        o_ref[0, :, hh * w:(hh + 1) * w] = o.T.astype(o_ref.dtype)


def _diff_attention(proj, vt, slopes, lam, subln, out_scale, *, tq=256, hp=2):
    b, s, _ = proj.shape
    nh = DIFF_HEADS
    w = 2 * DIFF_HEAD_DIM
    tq = min(tq, s)
    assert s % tq == 0 and tq % LANES == 0 and nh % hp == 0
    ng = nh // hp
    smem = pl.BlockSpec(memory_space=pltpu.SMEM)
    g = jnp.broadcast_to(subln.astype(F32)[:, None], (w, LANES))
    return pl.pallas_call(
        functools.partial(_diff_attn_kernel, tq=tq, hp=hp, out_scale=out_scale),
        grid=(b, ng, s // tq),
        in_specs=[
            smem, smem,
            pl.BlockSpec((1, tq, hp * w), lambda bi, h, i: (bi, i, h)),
            pl.BlockSpec((1, s, hp * w), lambda bi, h, i: (bi, 0, ng + h)),
            pl.BlockSpec((1, hp * w, s), lambda bi, h, i: (bi, h, 0)),
            pl.BlockSpec((w, LANES), lambda bi, h, i: (0, 0)),
        ],
        out_specs=pl.BlockSpec((1, tq, hp * w), lambda bi, h, i: (bi, i, h)),
        out_shape=jax.ShapeDtypeStruct((b, s, MIX_WIDTH), BF16),
        scratch_shapes=[pltpu.VMEM((hp, 1, 2 * tq), F32),
                        pltpu.VMEM((hp, w + SUM_ROWS, 2 * tq), F32),
                        pltpu.VMEM((tq, LANES), BF16),
                        pltpu.VMEM((hp, 2 * tq, w + LANES), BF16),
                        pltpu.VMEM((2, hp, tq, 2 * tq), F32)],
        compiler_params=_params("parallel", "parallel", "arbitrary"),
        name="diff_attn",
    )(_alibi_split(slopes), lam, proj, proj, vt, g)


def _moba_attn_kernel(sl_ref, q_ref, k_ref, vt_ref, o_ref, m_ref, acc_ref, kb_ref,
                      km_ref, qs_ref, sel_ref, s_ref, *, nb, hp):
    hg = pl.program_id(1)
    n = pl.program_id(2)
    blk = MOBA_BLOCK
    dh = MOBA_HEAD_DIM

    @pl.when(n == 0)
    def _():
        for hh in range(hp):
            kh = k_ref[0, :, hh * dh:(hh + 1) * dh].astype(F32)
            km_ref[hh] = jnp.mean(kh.reshape(nb, blk, dh), axis=1)

    m_ref[...] = jnp.full_like(m_ref, NEG_INF)
    acc_ref[...] = jnp.zeros_like(acc_ref)
    kb_ref[...] = _key_bias_lanes(blk)

    bidx = lax.broadcasted_iota(jnp.int32, (nb, blk), 0)
    slope2 = []
    for hh in range(hp):
        h = hg * hp + hh
        slope2.append(sum(sl_ref[t, h] for t in range(BIAS_TERMS)))
        q = q_ref[0, :, hh * dh:(hh + 1) * dh]
        qs_ref[hh, :, :dh] = (q.astype(F32) * (dh ** -0.5 * LOG2E)).astype(BF16)
        qs_ref[hh, :, dh:] = _query_bias_lanes(sl_ref, h, blk)

        kmean = km_ref[hh]
        km_hi = kmean.astype(BF16)
        km_lo = (kmean - km_hi.astype(F32)).astype(BF16)
        gate = _dot_nt(km_hi, q) + _dot_nt(km_lo, q)

        rank = jnp.zeros((nb, blk), jnp.int32)
        for jp in range(nb):
            gj = gate[jp:jp + 1, :]
            before = (gj > gate) | ((gj == gate) & (jp < bidx))
            rank = rank + jnp.where(before, 1, 0) * (jp < n).astype(jnp.int32)
        sel_ref[hh] = jnp.where((rank < MOBA_TOPK) & (bidx < n), 1.0, 0.0)

    def qk_all(j, buf):
        start = pl.multiple_of(j * blk, blk)
        for hh in range(hp):
            k = k_ref[0, pl.ds(start, blk), hh * dh:(hh + 1) * dh]
            s_ref[buf, hh] = _dot_nt(jnp.concatenate([k, kb_ref[...]], axis=1), qs_ref[hh])

    def fin_all(j, buf, diagonal):
        start = pl.multiple_of(j * blk, blk)
        for hh in range(hp):
            vt = vt_ref[0, hh * dh:(hh + 1) * dh, pl.ds(start, blk)]
            s = s_ref[buf, hh]
            c = slope2[hh] * ((j - n) * blk).astype(F32)
            if diagonal:
                kr = lax.broadcasted_iota(jnp.int32, (blk, blk), 0)
                qc = lax.broadcasted_iota(jnp.int32, (blk, blk), 1)
                s = jnp.where(qc >= kr, s, NEG_INF)
                ok = None
            else:
                ok = jnp.max(jnp.where(bidx == j, sel_ref[hh], 0.0), axis=0, keepdims=True) > 0.0
            _softmax_step_t(s, c, ok, vt, m_ref, acc_ref, hh)

    _pipelined_tiles(n, qk_all, fin_all)

    for hh in range(hp):
        o = acc_ref[hh, :dh] / acc_ref[hh, dh:dh + 1]
        o_ref[0, :, hh * dh:(hh + 1) * dh] = o.T.astype(o_ref.dtype)


def _moba_attention(proj, vt, slopes, *, hp=4):
    b, s, _ = proj.shape
    nh = MOBA_HEADS
    dh = MOBA_HEAD_DIM
    blk = MOBA_BLOCK
    assert s % blk == 0 and nh % hp == 0
    nb = s // blk
    ng = nh // hp
    smem = pl.BlockSpec(memory_space=pltpu.SMEM)
    return pl.pallas_call(
        functools.partial(_moba_attn_kernel, nb=nb, hp=hp),
        grid=(b, ng, nb),
        in_specs=[
            smem,
            pl.BlockSpec((1, blk, hp * dh), lambda bi, h, i: (bi, i, h)),
            pl.BlockSpec((1, s, hp * dh), lambda bi, h, i: (bi, 0, ng + h)),
            pl.BlockSpec((1, hp * dh, s), lambda bi, h, i: (bi, h, 0)),
        ],
        out_specs=pl.BlockSpec((1, blk, hp * dh), lambda bi, h, i: (bi, i, h)),
        out_shape=jax.ShapeDtypeStruct((b, s, MIX_WIDTH), BF16),
        scratch_shapes=[pltpu.VMEM((hp, 1, blk), F32),
                        pltpu.VMEM((hp, dh + SUM_ROWS, blk), F32),
                        pltpu.VMEM((blk, LANES), BF16),
                        pltpu.VMEM((hp, nb, dh), F32),
                        pltpu.VMEM((hp, blk, dh + LANES), BF16),
                        pltpu.VMEM((hp, nb, blk), F32),
                        pltpu.VMEM((2, hp, blk, blk), F32)],
        compiler_params=_params("parallel", "parallel", "arbitrary"),
        name="moba_attn",
    )(_alibi_split(slopes), proj, proj, vt)


def _swa_attn_kernel(slopes_ref, sinks_ref, q_ref, k_ref, v_ref, o_ref):
    i = pl.program_id(1)
    win = SWA_WINDOW
    dh = SWA_HEAD_DIM
    grp = SWA_Q_HEADS // SWA_KV_HEADS
    tq = win

    kstart = pl.multiple_of(jnp.maximum(i - 1, 0) * win, win)
    qpos = i * win + lax.broadcasted_iota(jnp.int32, (2 * tq, 1), 0) % tq
    kpos = kstart + lax.broadcasted_iota(jnp.int32, (1, 2 * win), 1)
    dist = qpos - kpos
    valid = (dist >= 0) & (dist < win)
    distf = dist.astype(F32)
    top = lax.broadcasted_iota(jnp.int32, (2 * tq, 1), 0) < tq
    lane = lax.broadcasted_iota(jnp.int32, (tq, 2 * dh), 1)

    for kvh in range(SWA_KV_HEADS):
        k = k_ref[0, pl.ds(kstart, 2 * win), kvh * 2 * dh:(kvh + 1) * 2 * dh]
        v = v_ref[0, pl.ds(kstart, 2 * win), kvh * 2 * dh:(kvh + 1) * 2 * dh]
        for pair in range(grp // 2):
            h0 = kvh * grp + 2 * pair
            q = q_ref[0, :, h0 * dh:(h0 + 2) * dh].astype(F32) * (dh ** -0.5)
            qq = jnp.concatenate([jnp.where(lane < dh, q, 0.0), jnp.where(lane >= dh, q, 0.0)],
                                 axis=0).astype(BF16)
            slope = jnp.where(top, slopes_ref[h0], slopes_ref[h0 + 1])
            sink = jnp.where(top, sinks_ref[h0], sinks_ref[h0 + 1])
            s = _dot_nt(qq, k) - slope * distf
            s = jnp.where(valid, s, NEG_INF)
            m = jnp.maximum(jnp.max(s, axis=-1, keepdims=True), sink)
            p = jnp.exp(s - m)
            denom = jnp.sum(p, axis=-1, keepdims=True) + jnp.exp(sink - m)
            o2 = jnp.dot(p.astype(BF16), v, preferred_element_type=F32) / denom
            o = jnp.where(lane < dh, o2[:tq], o2[tq:])
            o_ref[0, :, h0 * dh:(h0 + 2) * dh] = o.astype(o_ref.dtype)


def _swa_attention(proj, slopes, sinks):
    b, s, _ = proj.shape
    win = SWA_WINDOW
    e = MIX_WIDTH
    kvw = SWA_KV_HEADS * 2 * SWA_HEAD_DIM
    assert s % win == 0 and s >= 2 * win
    smem = pl.BlockSpec(memory_space=pltpu.SMEM)
    return pl.pallas_call(
        _swa_attn_kernel,
        grid=(b, s // win),
        in_specs=[
            smem, smem,
            pl.BlockSpec((1, win, e), lambda bi, i: (bi, i, 0)),
            pl.BlockSpec((1, s, kvw), lambda bi, i: (bi, 0, 2 * e // kvw)),
            pl.BlockSpec((1, s, kvw), lambda bi, i: (bi, 0, 2 * e // kvw + 1)),
        ],
        out_specs=pl.BlockSpec((1, win, e), lambda bi, i: (bi, i, 0)),
        out_shape=jax.ShapeDtypeStruct((b, s, e), BF16),
        compiler_params=_params("parallel", "arbitrary"),
        name="swa_attn",
    )(slopes, sinks, proj, proj, proj)


def _s5_ssm_kernel(u_ref, bb_ref, a_ref, cc_ref, d_ref, y_ref, st_ref, bu_ref, xb_ref,
                   *, bsz, tt, chunk_lanes):
    ns = S5_SLAB_STATE
    cw = chunk_lanes

    @pl.when(pl.program_id(1) == 0)
    def _():
        st_ref[...] = jnp.zeros_like(st_ref)

    u = u_ref[...].reshape(tt * bsz, S5_SLAB)
    y = d_ref[0] * u.astype(F32)
    for c in range(ns // cw):
        lo = c * cw
        bu_ref[2 * c] = jnp.dot(u, bb_ref[0, :, lo:lo + cw], preferred_element_type=F32)
        bu_ref[2 * c + 1] = jnp.dot(u, bb_ref[0, :, ns + lo:ns + lo + cw],
                                    preferred_element_type=F32)
        ar = jnp.broadcast_to(a_ref[0, 0:1, lo:lo + cw], (bsz, cw))
        ai = jnp.broadcast_to(a_ref[0, 1:2, lo:lo + cw], (bsz, cw))
        xr = st_ref[0, :, lo:lo + cw]
        xi = st_ref[1, :, lo:lo + cw]
        for t in range(tt):
            rows = slice(t * bsz, (t + 1) * bsz)
            xr, xi = (ar * xr - ai * xi + bu_ref[2 * c, rows, :],
                      ar * xi + ai * xr + bu_ref[2 * c + 1, rows, :])
            xb_ref[2 * c, rows, :] = xr.astype(BF16)
            xb_ref[2 * c + 1, rows, :] = xi.astype(BF16)
        st_ref[0, :, lo:lo + cw] = xr
        st_ref[1, :, lo:lo + cw] = xi
        y = (y + jnp.dot(xb_ref[2 * c], cc_ref[0, lo:lo + cw], preferred_element_type=F32)
             + jnp.dot(xb_ref[2 * c + 1], cc_ref[0, ns + lo:ns + lo + cw],
                       preferred_element_type=F32))
    y_ref[...] = y.reshape(tt, bsz, S5_SLAB).astype(y_ref.dtype)


def _s5_ssm(proj_tm, bd_b, a_bar, bd_c, d_skip, *, tt=32, chunk_lanes=512):
    s, b, _ = proj_tm.shape
    tt = min(tt, s)
    assert s % tt == 0 and b % 16 == 0
    n_slab = MIX_WIDTH // S5_SLAB
    ns = S5_SLAB_STATE
    n_buf = 2 * ns // chunk_lanes
    return pl.pallas_call(
        functools.partial(_s5_ssm_kernel, bsz=b, tt=tt, chunk_lanes=chunk_lanes),
        grid=(n_slab, s // tt),
        in_specs=[
            pl.BlockSpec((tt, b, S5_SLAB), lambda sl, t: (t, 0, sl)),
            pl.BlockSpec((1, S5_SLAB, 2 * ns), lambda sl, t: (sl, 0, 0)),
            pl.BlockSpec((1, 2, ns), lambda sl, t: (sl, 0, 0)),
            pl.BlockSpec((1, 2 * ns, S5_SLAB), lambda sl, t: (sl, 0, 0)),
            pl.BlockSpec((1, 1, S5_SLAB), lambda sl, t: (sl, 0, 0)),
        ],
        out_specs=pl.BlockSpec((tt, b, S5_SLAB), lambda sl, t: (t, 0, sl)),
        out_shape=jax.ShapeDtypeStruct((s, b, MIX_WIDTH), BF16),
        scratch_shapes=[pltpu.VMEM((2, b, ns), F32),
                        pltpu.VMEM((n_buf, b * tt, chunk_lanes), F32),
                        pltpu.VMEM((n_buf, b * tt, chunk_lanes), BF16)],
        compiler_params=_params("parallel", "arbitrary"),
        name="s5_ssm",
    )(proj_tm, bd_b, a_bar, bd_c, d_skip)


def _s5_discretize(a_re, a_im, log_dt, b_re, b_im, c_re, c_im, d_skip):
    lr, li = a_re.astype(F32), a_im.astype(F32)
    dt = jnp.exp(log_dt.astype(F32))[:, None]
    mag = jnp.exp(lr * dt)
    ab_re, ab_im = mag * jnp.cos(li * dt), mag * jnp.sin(li * dt)
    den = lr * lr + li * li
    f_re = ((ab_re - 1.0) * lr + ab_im * li) / den
    f_im = (ab_im * lr - (ab_re - 1.0) * li) / den
    br, bi = b_re.astype(F32), b_im.astype(F32)
    bb_re = f_re[..., None] * br - f_im[..., None] * bi
    bb_im = f_re[..., None] * bi + f_im[..., None] * br
    n_slab = MIX_WIDTH // S5_SLAB
    gs = S5_SLAB // S5_GROUP
    eye = jnp.eye(gs, dtype=F32)

    def pack_b(m):
        m = m.reshape(n_slab, gs, S5_STATE, S5_GROUP)
        return jnp.einsum('sgpc,gh->sgchp', m, eye).reshape(n_slab, S5_SLAB, S5_SLAB_STATE)

    def pack_c(m):
        m = m.reshape(n_slab, gs, S5_GROUP, S5_STATE)
        return jnp.einsum('sgcp,gh->shpgc', m, eye).reshape(n_slab, S5_SLAB_STATE, S5_SLAB)

    bd_b = jnp.concatenate([pack_b(bb_re), pack_b(bb_im)], axis=2).astype(BF16)
    bd_c = jnp.concatenate([pack_c(c_re.astype(F32)), pack_c(-c_im.astype(F32))], axis=1).astype(BF16)
    a_bar = jnp.stack([ab_re.reshape(n_slab, S5_SLAB_STATE), ab_im.reshape(n_slab, S5_SLAB_STATE)],
                      axis=1)
    return bd_b, a_bar, bd_c, d_skip.astype(F32).reshape(n_slab, 1, S5_SLAB)


def _glu_kernel(y_ref, w_ref, b_ref, o_ref, g_ref, *, tn):
    j = pl.program_id(2)

    @pl.when(j == 0)
    def _():
        g_ref[...] = jax.nn.gelu(y_ref[...].astype(F32)).astype(BF16)

    r = jnp.dot(g_ref[...], w_ref[...], preferred_element_type=F32) + b_ref[...]
    g = g_ref[:, pl.ds(pl.multiple_of(j * tn, tn), tn)].astype(F32)
    o_ref[...] = (g * jax.nn.sigmoid(r)).astype(o_ref.dtype)


def _glu(y_tm, w, bias, bsz, *, tm=1024, tn=1024):
    s = y_tm.shape[0]
    e = w.shape[0]
    tm = min(tm, s)
    assert s % tm == 0 and e % tn == 0
    tps = s // tm
    return pl.pallas_call(
        functools.partial(_glu_kernel, tn=tn),
        grid=(bsz, tps, e // tn),
        in_specs=[
            pl.BlockSpec((tm, e), lambda b, i, j: (i, b)),
            pl.BlockSpec((e, tn), lambda b, i, j: (0, j)),
            pl.BlockSpec((1, tn), lambda b, i, j: (0, j)),
        ],
        out_specs=pl.BlockSpec((tm, tn), lambda b, i, j: (b * tps + i, j)),
        out_shape=jax.ShapeDtypeStruct((bsz * s, e), BF16),
        scratch_shapes=[pltpu.VMEM((tm, e), BF16)],
        compiler_params=_params("parallel", "parallel", "arbitrary"),
        name="s5_glu",
    )(y_tm, w, bias.reshape(1, e).astype(F32))


def _s5_layer(x2d, h2d, bsz, seq, post_g, next_g, w_in, a_re, a_im, log_dt, b_re, b_im, c_re,
              c_im, d_skip, w_glu, b_glu, w_out):
    e = MIX_WIDTH
    proj = _in_proj(h2d, w_in.astype(BF16), seq=seq)
    bd_b, a_bar, bd_c, d3 = _s5_discretize(a_re, a_im, log_dt, b_re, b_im, c_re, c_im, d_skip)
    y = _s5_ssm(proj.reshape(seq, bsz, 2 * e), bd_b, a_bar, bd_c, d3)
    y = _glu(y.reshape(seq, bsz * e), w_glu.astype(BF16), b_glu, bsz)
    return _gate_out(y, proj, 1, x2d, w_out.astype(BF16), post_g, next_g, bsz, seq,
                     z_time_major=True)


def _attn_projections(h2d, seq, w_in):
    e = MIX_WIDTH
    w_qkz = jnp.concatenate([w_in[:, :2 * e], w_in[:, 3 * e:]], axis=1).astype(BF16)
    proj = _in_proj(h2d, w_qkz)
    vt = _in_proj_t(h2d, w_in[:, 2 * e:3 * e].T.astype(BF16), seq)
    return proj, vt


def _diff_layer(x2d, h2d, bsz, seq, layer_idx, post_g, next_g, w_in, lq1, lk1, lq2, lk2, subln,
                w_out):
    e = MIX_WIDTH
    proj, vt = _attn_projections(h2d, seq, w_in)
    lam_init = 0.8 - 0.6 * math.exp(-0.3 * layer_idx)
    lam = (jnp.exp(jnp.sum(lq1.astype(F32) * lk1.astype(F32)))
           - jnp.exp(jnp.sum(lq2.astype(F32) * lk2.astype(F32))) + lam_init).reshape(1)
    o = _diff_attention(proj.reshape(bsz, seq, 3 * e), vt, _alibi_slopes(DIFF_HEADS), lam, subln,
                        1.0 - lam_init)
    return _gate_out(o.reshape(bsz * seq, e), proj, 2, x2d, w_out.astype(BF16), post_g, next_g,
                     bsz, seq)


def _moba_layer(x2d, h2d, bsz, seq, post_g, next_g, w_in, w_out):
    e = MIX_WIDTH
    proj, vt = _attn_projections(h2d, seq, w_in)
    o = _moba_attention(proj.reshape(bsz, seq, 3 * e), vt, _alibi_slopes(MOBA_HEADS))
    return _gate_out(o.reshape(bsz * seq, e), proj, 2, x2d, w_out.astype(BF16), post_g, next_g,
                     bsz, seq)


def _swa_layer(x2d, h2d, bsz, seq, post_g, next_g, w_in, sinks, w_out):
    e = MIX_WIDTH
    dh = SWA_HEAD_DIM
    nqc, nkc = SWA_Q_HEADS * dh, SWA_KV_HEADS * dh
    wq, wk = w_in[:, :nqc], w_in[:, nqc:nqc + nkc]
    wv, wz = w_in[:, nqc + nkc:nqc + 2 * nkc], w_in[:, nqc + 2 * nkc:]

    def dup(w):
        w = w.reshape(-1, SWA_KV_HEADS, 1, dh)
        return jnp.broadcast_to(w, (w.shape[0], SWA_KV_HEADS, 2, dh)).reshape(-1, 2 * nkc)

    w_cat = jnp.concatenate([wq, wz, dup(wk), dup(wv)], axis=1).astype(BF16)
    proj = _in_proj(h2d, w_cat)
    o = _swa_attention(proj.reshape(bsz, seq, w_cat.shape[1]), _alibi_slopes(SWA_Q_HEADS),
                       sinks.astype(F32))
    return _gate_out(o.reshape(bsz * seq, e), proj, 1, x2d, w_out.astype(BF16), post_g, next_g,
                     bsz, seq)


def kernel(x, pre_norm, post_norm, s5_w_in, s5_a_re, s5_a_im, s5_log_dt, s5_b_re, s5_b_im, s5_c_re, s5_c_im, s5_d, s5_w_glu, s5_b_glu, s5_w_out, diff_w_in, diff_lq1, diff_lk1, diff_lq2, diff_lk2, diff_subln, diff_w_out, moba_w_in, moba_w_out, swa_w_in, swa_sinks, swa_w_out):
    bsz, seq, d = x.shape
    depth = pre_norm.shape[0]
    x2d = x.reshape(bsz * seq, d)
    h2d = _prenorm(x2d, pre_norm[0])
    for i in range(depth):
        kind, j = i % 4, i // 4
        post_g = post_norm[i]
        next_g = pre_norm[i + 1] if i + 1 < depth else None
        if kind == 0:
            x2d, h2d = _s5_layer(x2d, h2d, bsz, seq, post_g, next_g, s5_w_in[j], s5_a_re[j],
                                 s5_a_im[j], s5_log_dt[j], s5_b_re[j], s5_b_im[j], s5_c_re[j],
                                 s5_c_im[j], s5_d[j], s5_w_glu[j], s5_b_glu[j], s5_w_out[j])
        elif kind == 1:
            x2d, h2d = _diff_layer(x2d, h2d, bsz, seq, i, post_g, next_g, diff_w_in[j],
                                   diff_lq1[j], diff_lk1[j], diff_lq2[j], diff_lk2[j],
                                   diff_subln[j], diff_w_out[j])
        elif kind == 2:
            x2d, h2d = _moba_layer(x2d, h2d, bsz, seq, post_g, next_g, moba_w_in[j],
                                   moba_w_out[j])
        else:
            x2d, h2d = _swa_layer(x2d, h2d, bsz, seq, post_g, next_g, swa_w_in[j], swa_sinks[j],
                                  swa_w_out[j])
    return x2d.reshape(bsz, seq, d)
```

```python
import functools
import math

import jax
import jax.numpy as jnp
from jax import lax
from jax.experimental import pallas as pl
from jax.experimental.pallas import tpu as pltpu

F32 = jnp.float32
BF16 = jnp.bfloat16

D_MODEL = 2048
MIX_WIDTH = D_MODEL
RMS_EPS = 1e-6
NEG_INF = -1e30
LOG2E = math.log2(math.e)
LANES = 128

S5_GROUP = 16
S5_GROUPS = MIX_WIDTH // S5_GROUP
S5_STATE = 64
S5_SLAB = 256
S5_SLAB_STATE = (S5_SLAB // S5_GROUP) * S5_STATE

DIFF_HEADS = 16
DIFF_HEAD_DIM = 64
MOBA_HEADS = 16
MOBA_HEAD_DIM = 128
MOBA_BLOCK = 256
MOBA_TOPK = 3
SWA_Q_HEADS = 32
SWA_KV_HEADS = 4
SWA_HEAD_DIM = 64
SWA_WINDOW = 128

VMEM_LIMIT = 56 * 1024 * 1024


def _params(*sem):
    return pltpu.CompilerParams(dimension_semantics=sem, vmem_limit_bytes=VMEM_LIMIT)


def _alibi_slopes(n_heads):
    return 2.0 ** (-8.0 * jnp.arange(1, n_heads + 1, dtype=F32) / n_heads)


def _dot_nt(a, b):
    return lax.dot_general(a, b, (((1,), (1,)), ((), ())), preferred_element_type=F32)


def _sigmoid(x):
    return 0.5 * jnp.tanh(0.5 * x) + 0.5


def _rms_normalize(x, gain):
    ms = jnp.mean(x * x, axis=-1, keepdims=True)
    return x * lax.rsqrt(ms + RMS_EPS) * gain


def _prenorm_kernel(x_ref, g_ref, h_ref):
    h_ref[...] = _rms_normalize(x_ref[...], g_ref[...]).astype(BF16)


def _prenorm(x2d, gain, *, tm=512):
    t, d = x2d.shape
    tm = min(tm, t)
    assert t % tm == 0
    return pl.pallas_call(
        _prenorm_kernel,
        grid=(t // tm,),
        in_specs=[pl.BlockSpec((tm, d), lambda i: (i, 0)), pl.BlockSpec((1, d), lambda i: (0, 0))],
        out_specs=pl.BlockSpec((tm, d), lambda i: (i, 0)),
        out_shape=jax.ShapeDtypeStruct((t, d), BF16),
        compiler_params=_params("parallel"),
        name="prenorm",
    )(x2d, gain.reshape(1, d).astype(F32))


def _col_tile(n, cap=2048):
    return max(t for t in range(256, cap + 1, 256) if n % t == 0)


def _matmul_kernel(h_ref, w_ref, o_ref):
    o_ref[...] = jnp.dot(h_ref[...], w_ref[...], preferred_element_type=F32).astype(o_ref.dtype)


def _in_proj(h2d, w, *, seq=None, tm=1024):
    t, d = h2d.shape
    n = w.shape[1]
    tm = min(tm, t if seq is None else seq)
    tn = _col_tile(n)
    assert t % tm == 0
    if seq is None:
        out_shape = (t, n)
        out_map = lambda j, i: (i, j)
    else:
        assert seq % tm == 0
        tps, npt = seq // tm, n // tn
        out_shape = (seq, (t // seq) * n)
        out_map = lambda j, i: (i % tps, (i // tps) * npt + j)
    return pl.pallas_call(
        _matmul_kernel,
        grid=(n // tn, t // tm),
        in_specs=[
            pl.BlockSpec((tm, d), lambda j, i: (i, 0)),
            pl.BlockSpec((d, tn), lambda j, i: (0, j)),
        ],
        out_specs=pl.BlockSpec((tm, tn), out_map),
        out_shape=jax.ShapeDtypeStruct(out_shape, BF16),
        compiler_params=_params("parallel", "parallel"),
        name="in_proj",
    )(h2d, w)


def _matmul_t_kernel(h_ref, wt_ref, o_ref):
    o_ref[0] = _dot_nt(wt_ref[...], h_ref[...]).astype(o_ref.dtype)


def _in_proj_t(h2d, wt, seq, *, tm=1024):
    t, d = h2d.shape
    n = wt.shape[0]
    tm = min(tm, seq)
    tn = _col_tile(n)
    assert seq % tm == 0
    tps = seq // tm
    return pl.pallas_call(
        _matmul_t_kernel,
        grid=(n // tn, t // tm),
        in_specs=[
            pl.BlockSpec((tm, d), lambda j, i: (i, 0)),
            pl.BlockSpec((tn, d), lambda j, i: (j, 0)),
        ],
        out_specs=pl.BlockSpec((1, tn, tm), lambda j, i: (i // tps, j, i % tps)),
        out_shape=jax.ShapeDtypeStruct((t // seq, n, seq), BF16),
        compiler_params=_params("parallel", "parallel"),
        name="in_proj_t",
    )(h2d, wt)


def _gate_out_kernel(o_ref, z_ref, x_ref, w_ref, g_ref, gn_ref, out_ref, *h_ref, chunks):
    cm = o_ref.shape[0] // chunks
    for c in range(chunks):
        rows = slice(c * cm, (c + 1) * cm)
        z = z_ref[rows, :].astype(F32)
        y = (o_ref[rows, :].astype(F32) * (z * _sigmoid(z))).astype(BF16)
        r = jnp.dot(y, w_ref[...], preferred_element_type=F32)
        xn = x_ref[rows, :] + _rms_normalize(r, g_ref[...])
        out_ref[rows, :] = xn
        if h_ref:
            h_ref[0][rows, :] = _rms_normalize(xn, gn_ref[...]).astype(BF16)


def _gate_out(o2d, zsrc, z_block, x2d, w, gain, next_gain, bsz, seq, *, z_time_major=False,
              tm=512, chunks=4):
    t, e = o2d.shape
    d = w.shape[1]
    tm = min(tm, seq)
    assert seq % tm == 0 and tm % chunks == 0
    tps = seq // tm
    row_map = lambda b, i: (b * tps + i, 0)
    const = lambda b, i: (0, 0)
    if z_time_major:
        blocks_per_batch = zsrc.shape[1] // bsz // e
        z_map = lambda b, i: (i, b * blocks_per_batch + z_block)
    else:
        z_map = lambda b, i: (b * tps + i, z_block)
    with_h = next_gain is not None
    gn = (next_gain if with_h else gain).reshape(1, d).astype(F32)
    out_specs = [pl.BlockSpec((tm, d), row_map)]
    out_shape = [jax.ShapeDtypeStruct((t, d), F32)]
    if with_h:
        out_specs.append(pl.BlockSpec((tm, d), row_map))
        out_shape.append(jax.ShapeDtypeStruct((t, d), BF16))
    res = pl.pallas_call(
        functools.partial(_gate_out_kernel, chunks=chunks),
        grid=(bsz, tps),
        in_specs=[
            pl.BlockSpec((tm, e), row_map),
            pl.BlockSpec((tm, e), z_map),
            pl.BlockSpec((tm, d), row_map),
            pl.BlockSpec((e, d), const, pipeline_mode=pl.Buffered(1)),
            pl.BlockSpec((1, d), const),
            pl.BlockSpec((1, d), const),
        ],
        out_specs=out_specs,
        out_shape=out_shape,
        compiler_params=_params("parallel", "parallel"),
        name="gate_out",
    )(o2d, zsrc, x2d, w, gain.reshape(1, d).astype(F32), gn)
    return (res[0], res[1]) if with_h else (res[0], None)


BIG = 1e30
SUM_ROWS = 16


def _softmax_step_t(s, c, ok, vt, m_ref, acc_ref):
    m_old = m_ref[...]
    mt = jnp.max(s, axis=0, keepdims=True) + c
    if ok is not None:
        mt = jnp.where(ok, mt, NEG_INF)
    m_new = jnp.maximum(m_old, mt)
    alpha = jnp.exp2(m_old - m_new)
    r = m_new - c
    if ok is not None:
        r = jnp.where(ok, r, BIG)
    p = jnp.exp2(s - r).astype(BF16)
    vt1 = jnp.concatenate([vt, jnp.ones((SUM_ROWS, vt.shape[1]), BF16)], axis=0)
    acc_ref[...] = alpha * acc_ref[...] + jnp.dot(vt1, p, preferred_element_type=F32)
    m_ref[...] = m_new


def _paired_causal_sweep(n_blocks, init, scores, finish, finalize):
    n_past = n_blocks - 1

    def pair(i, carry):
        blk_a, blk_b = i, n_blocks - 1 - i
        init()
        tiles = []
        for p in range(n_past):
            in_a = p < i
            tiles.append((jnp.where(in_a, blk_a, blk_b), jnp.where(in_a, p, p - i),
                          jnp.where(in_a, 0, 1), False))
        tiles += [(blk_a, blk_a, 0, True), (blk_b, blk_b, 1, True)]
        scores(tiles[0][0], tiles[0][1], 0)
        for p, (qb, j, slot, diagonal) in enumerate(tiles):
            if p + 1 < len(tiles):
                scores(tiles[p + 1][0], tiles[p + 1][1], (p + 1) % 2)
            finish(qb, j, slot, p % 2, diagonal)
        finalize(blk_a, 0)
        finalize(blk_b, 1)
        return carry

    lax.fori_loop(0, n_blocks // 2, pair, 0)


def _diff_attn_kernel(slopes_ref, lam_ref, q_ref, k_ref, vt_ref, g_ref, o_ref,
                      m_ref, acc_ref, bias_ref, qq_ref, s_ref, *, tq, hp, nqb, out_scale):
    hg = pl.program_id(1)
    lam = lam_ref[0]
    d = DIFF_HEAD_DIM
    w = 2 * d
    reps = 2 * tq // LANES

    lane = lax.broadcasted_iota(jnp.int32, (tq, w), 1)
    krow = lax.broadcasted_iota(jnp.int32, (tq, LANES), 0).astype(F32)
    slope2 = [slopes_ref[hg * hp + hh] * LOG2E for hh in range(hp)]
    for hh in range(hp):
        bias_ref[hh] = slope2[hh] * krow

    def prep(qb, carry):
        rows = pl.ds(pl.multiple_of(qb * tq, tq), tq)
        for hh in range(hp):
            q = q_ref[0, rows, hh * w:(hh + 1) * w].astype(F32) * (d ** -0.5 * LOG2E)
            qq_ref[qb, hh, :tq] = jnp.where(lane < d, q, 0.0).astype(BF16)
            qq_ref[qb, hh, tq:] = jnp.where(lane >= d, q, 0.0).astype(BF16)
        return carry

    lax.fori_loop(0, nqb, prep, 0)

    def init():
        m_ref[...] = jnp.full_like(m_ref, NEG_INF)
        acc_ref[...] = jnp.zeros_like(acc_ref)

    def scores(qb, j, buf):
        keys = pl.ds(pl.multiple_of(j * tq, tq), tq)
        for hh in range(hp):
            k = k_ref[0, keys, hh * w:(hh + 1) * w]
            s_ref[buf, hh] = (_dot_nt(k, qq_ref[qb, hh])
                              + jnp.concatenate([bias_ref[hh]] * reps, axis=1))

    def finish(qb, j, slot, buf, diagonal):
        keys = pl.ds(pl.multiple_of(j * tq, tq), tq)
        for hh in range(hp):
            vt = vt_ref[0, hh * w:(hh + 1) * w, keys]
            s = s_ref[buf, hh]
            if diagonal:
                kr = lax.broadcasted_iota(jnp.int32, (tq, 2 * tq), 0)
                qc = lax.broadcasted_iota(jnp.int32, (tq, 2 * tq), 1)
                qc = jnp.where(qc >= tq, qc - tq, qc)
                s = jnp.where(qc >= kr, s, NEG_INF)
            c = slope2[hh] * ((j - qb) * tq).astype(F32)
            _softmax_step_t(s, c, None, vt, m_ref.at[slot, hh], acc_ref.at[slot, hh])

    g = jnp.concatenate([g_ref[...]] * (tq // LANES), axis=1)

    def finalize(qb, slot):
        rows = pl.ds(pl.multiple_of(qb * tq, tq), tq)
        for hh in range(hp):
            a = acc_ref[slot, hh, :w] / acc_ref[slot, hh, w:w + 1]
            o = a[:, :tq] - lam * a[:, tq:]
            ms = jnp.mean(o * o, axis=0, keepdims=True)
            o = o * lax.rsqrt(ms + RMS_EPS) * g * out_scale
            o_ref[0, rows, hh * w:(hh + 1) * w] = o.T.astype(o_ref.dtype)

    _paired_causal_sweep(nqb, init, scores, finish, finalize)


def _diff_attention(proj, vt, slopes, lam, subln, out_scale, *, tq=256, hp=2):
    b, s, _ = proj.shape
    nh = DIFF_HEADS
    w = 2 * DIFF_HEAD_DIM
    tq = min(tq, s // 2)
    nqb = s // tq
    assert s % tq == 0 and tq % LANES == 0 and nh % hp == 0 and nqb % 2 == 0
    ng = nh // hp
    smem = pl.BlockSpec(memory_space=pltpu.SMEM)
    g = jnp.broadcast_to(subln.astype(F32)[:, None], (w, LANES))
    return pl.pallas_call(
        functools.partial(_diff_attn_kernel, tq=tq, hp=hp, nqb=nqb, out_scale=out_scale),
        grid=(b, ng),
        in_specs=[
            smem, smem,
            pl.BlockSpec((1, s, hp * w), lambda bi, h: (bi, 0, h)),
            pl.BlockSpec((1, s, hp * w), lambda bi, h: (bi, 0, ng + h)),
            pl.BlockSpec((1, hp * w, s), lambda bi, h: (bi, h, 0)),
            pl.BlockSpec((w, LANES), lambda bi, h: (0, 0)),
        ],
        out_specs=pl.BlockSpec((1, s, hp * w), lambda bi, h: (bi, 0, h)),
        out_shape=jax.ShapeDtypeStruct((b, s, MIX_WIDTH), BF16),
        scratch_shapes=[pltpu.VMEM((2, hp, 1, 2 * tq), F32),
                        pltpu.VMEM((2, hp, w + SUM_ROWS, 2 * tq), F32),
                        pltpu.VMEM((hp, tq, LANES), F32),
                        pltpu.VMEM((nqb, hp, 2 * tq, w), BF16),
                        pltpu.VMEM((2, hp, tq, 2 * tq), F32)],
        compiler_params=_params("parallel", "parallel"),
        name="diff_attn",
    )(slopes, lam, proj, proj, vt, g)


def _moba_attn_kernel(slopes_ref, q_ref, k_ref, vt_ref, o_ref, m_ref, acc_ref, bias_ref,
                      qs_ref, sel_ref, s_ref, *, nb, hp):
    hg = pl.program_id(1)
    blk = MOBA_BLOCK
    dh = MOBA_HEAD_DIM
    seq = nb * blk
    reps = blk // LANES

    krow = lax.broadcasted_iota(jnp.int32, (blk, LANES), 0).astype(F32)
    bidx = lax.broadcasted_iota(jnp.int32, (nb, seq), 0)
    qblk = lax.broadcasted_iota(jnp.int32, (nb, seq), 1) // blk
    slope2 = [slopes_ref[hg * hp + hh] * LOG2E for hh in range(hp)]
    for hh in range(hp):
        bias_ref[hh] = slope2[hh] * krow
        q = q_ref[0, :, hh * dh:(hh + 1) * dh]
        qs_ref[hh] = (q.astype(F32) * (dh ** -0.5 * LOG2E)).astype(BF16)

        kh = k_ref[0, :, hh * dh:(hh + 1) * dh].astype(F32)
        kmean = jnp.mean(kh.reshape(nb, blk, dh), axis=1)
        km_hi = kmean.astype(BF16)
        km_lo = (kmean - km_hi.astype(F32)).astype(BF16)
        gate = _dot_nt(km_hi, q) + _dot_nt(km_lo, q)

        rank = jnp.zeros((nb, seq), jnp.int32)
        for jp in range(nb):
            gj = gate[jp:jp + 1, :]
            before = (gj > gate) | ((gj == gate) & (jp < bidx))
            rank = rank + jnp.where(before, 1, 0) * jnp.where(jp < qblk, 1, 0)
        sel_ref[hh] = jnp.where((rank < MOBA_TOPK) & (bidx < qblk), 1.0, 0.0)

    def init():
        m_ref[...] = jnp.full_like(m_ref, NEG_INF)
        acc_ref[...] = jnp.zeros_like(acc_ref)

    def scores(qb, j, buf):
        keys = pl.ds(pl.multiple_of(j * blk, blk), blk)
        rows = pl.ds(pl.multiple_of(qb * blk, blk), blk)
        for hh in range(hp):
            k = k_ref[0, keys, hh * dh:(hh + 1) * dh]
            s_ref[buf, hh] = (_dot_nt(k, qs_ref[hh, rows, :])
                              + jnp.concatenate([bias_ref[hh]] * reps, axis=1))

    def finish(qb, j, slot, buf, diagonal):
        keys = pl.ds(pl.multiple_of(j * blk, blk), blk)
        cols = pl.ds(pl.multiple_of(qb * blk, blk), blk)
        for hh in range(hp):
            vt = vt_ref[0, hh * dh:(hh + 1) * dh, keys]
            s = s_ref[buf, hh]
            c = slope2[hh] * ((j - qb) * blk).astype(F32)
            if diagonal:
                kr = lax.broadcasted_iota(jnp.int32, (blk, blk), 0)
                qc = lax.broadcasted_iota(jnp.int32, (blk, blk), 1)
                s = jnp.where(qc >= kr, s, NEG_INF)
                ok = None
            else:
                tile_idx = lax.broadcasted_iota(jnp.int32, (nb, blk), 0)
                chosen = jnp.where(tile_idx == j, sel_ref[hh, :, cols], 0.0)
                ok = jnp.max(chosen, axis=0, keepdims=True) > 0.0
            _softmax_step_t(s, c, ok, vt, m_ref.at[slot, hh], acc_ref.at[slot, hh])

    def finalize(qb, slot):
        rows = pl.ds(pl.multiple_of(qb * blk, blk), blk)
        for hh in range(hp):
            o = acc_ref[slot, hh, :dh] / acc_ref[slot, hh, dh:dh + 1]
            o_ref[0, rows, hh * dh:(hh + 1) * dh] = o.T.astype(o_ref.dtype)

    _paired_causal_sweep(nb, init, scores, finish, finalize)


def _moba_attention(proj, vt, slopes, *, hp=4):
    b, s, _ = proj.shape
    nh = MOBA_HEADS
    dh = MOBA_HEAD_DIM
    blk = MOBA_BLOCK
    nb = s // blk
    assert s % blk == 0 and nh % hp == 0 and nb % 2 == 0
    ng = nh // hp
    smem = pl.BlockSpec(memory_space=pltpu.SMEM)
    return pl.pallas_call(
        functools.partial(_moba_attn_kernel, nb=nb, hp=hp),
        grid=(b, ng),
        in_specs=[
            smem,
            pl.BlockSpec((1, s, hp * dh), lambda bi, h: (bi, 0, h)),
            pl.BlockSpec((1, s, hp * dh), lambda bi, h: (bi, 0, ng + h)),
            pl.BlockSpec((1, hp * dh, s), lambda bi, h: (bi, h, 0)),
        ],
        out_specs=pl.BlockSpec((1, s, hp * dh), lambda bi, h: (bi, 0, h)),
        out_shape=jax.ShapeDtypeStruct((b, s, MIX_WIDTH), BF16),
        scratch_shapes=[pltpu.VMEM((2, hp, 1, blk), F32),
                        pltpu.VMEM((2, hp, dh + SUM_ROWS, blk), F32),
                        pltpu.VMEM((hp, blk, LANES), F32),
                        pltpu.VMEM((hp, s, dh), BF16),
                        pltpu.VMEM((hp, nb, s), F32),
                        pltpu.VMEM((2, hp, blk, blk), F32)],
        compiler_params=_params("parallel", "parallel"),
        name="moba_attn",
    )(slopes, proj, proj, vt)


def _swa_attn_kernel(slopes_ref, sinks_ref, q_ref, k_ref, v_ref, o_ref):
    i = pl.program_id(1)
    win = SWA_WINDOW
    dh = SWA_HEAD_DIM
    grp = SWA_Q_HEADS // SWA_KV_HEADS
    tq = win

    kstart = pl.multiple_of(jnp.maximum(i - 1, 0) * win, win)
    qpos = i * win + lax.broadcasted_iota(jnp.int32, (2 * tq, 1), 0) % tq
    kpos = kstart + lax.broadcasted_iota(jnp.int32, (1, 2 * win), 1)
    dist = qpos - kpos
    valid = (dist >= 0) & (dist < win)
    distf = dist.astype(F32)
    top = lax.broadcasted_iota(jnp.int32, (2 * tq, 1), 0) < tq
    lane = lax.broadcasted_iota(jnp.int32, (tq, 2 * dh), 1)

    for kvh in range(SWA_KV_HEADS):
        k = k_ref[0, pl.ds(kstart, 2 * win), kvh * 2 * dh:(kvh + 1) * 2 * dh]
        v = v_ref[0, pl.ds(kstart, 2 * win), kvh * 2 * dh:(kvh + 1) * 2 * dh]
        for pair in range(grp // 2):
            h0 = kvh * grp + 2 * pair
            q = q_ref[0, :, h0 * dh:(h0 + 2) * dh].astype(F32) * (dh ** -0.5)
            qq = jnp.concatenate([jnp.where(lane < dh, q, 0.0), jnp.where(lane >= dh, q, 0.0)],
                                 axis=0).astype(BF16)
            slope = jnp.where(top, slopes_ref[h0], slopes_ref[h0 + 1])
            sink = jnp.where(top, sinks_ref[h0], sinks_ref[h0 + 1])
            s = _dot_nt(qq, k) - slope * distf
            s = jnp.where(valid, s, NEG_INF)
            m = jnp.maximum(jnp.max(s, axis=-1, keepdims=True), sink)
            p = jnp.exp(s - m)
            denom = jnp.sum(p, axis=-1, keepdims=True) + jnp.exp(sink - m)
            o2 = jnp.dot(p.astype(BF16), v, preferred_element_type=F32) / denom
            o = jnp.where(lane < dh, o2[:tq], o2[tq:])
            o_ref[0, :, h0 * dh:(h0 + 2) * dh] = o.astype(o_ref.dtype)


def _swa_attention(proj, slopes, sinks):
    b, s, _ = proj.shape
    win = SWA_WINDOW
    e = MIX_WIDTH
    kvw = SWA_KV_HEADS * 2 * SWA_HEAD_DIM
    assert s % win == 0 and s >= 2 * win
    smem = pl.BlockSpec(memory_space=pltpu.SMEM)
    return pl.pallas_call(
        _swa_attn_kernel,
        grid=(b, s // win),
        in_specs=[
            smem, smem,
            pl.BlockSpec((1, win, e), lambda bi, i: (bi, i, 0)),
            pl.BlockSpec((1, s, kvw), lambda bi, i: (bi, 0, 2 * e // kvw)),
            pl.BlockSpec((1, s, kvw), lambda bi, i: (bi, 0, 2 * e // kvw + 1)),
        ],
        out_specs=pl.BlockSpec((1, win, e), lambda bi, i: (bi, i, 0)),
        out_shape=jax.ShapeDtypeStruct((b, s, e), BF16),
        compiler_params=_params("parallel", "arbitrary"),
        name="swa_attn",
    )(slopes, sinks, proj, proj, proj)


def _s5_ssm_kernel(u_ref, bb_ref, a_ref, cc_ref, d_ref, y_ref, st_ref, bu_ref, xb_ref,
                   *, bsz, tt, chunk_lanes):
    ns = S5_SLAB_STATE
    cw = chunk_lanes

    @pl.when(pl.program_id(1) == 0)
    def _():
        st_ref[...] = jnp.zeros_like(st_ref)

    u = u_ref[...].reshape(tt * bsz, S5_SLAB)
    y = d_ref[0] * u.astype(F32)
    for c in range(ns // cw):
        lo = c * cw
        bu_ref[2 * c] = jnp.dot(u, bb_ref[0, :, lo:lo + cw], preferred_element_type=F32)
        bu_ref[2 * c + 1] = jnp.dot(u, bb_ref[0, :, ns + lo:ns + lo + cw],
                                    preferred_element_type=F32)
        ar = jnp.broadcast_to(a_ref[0, 0:1, lo:lo + cw], (bsz, cw))
        ai = jnp.broadcast_to(a_ref[0, 1:2, lo:lo + cw], (bsz, cw))
        xr = st_ref[0, :, lo:lo + cw]
        xi = st_ref[1, :, lo:lo + cw]
        for t in range(tt):
            rows = slice(t * bsz, (t + 1) * bsz)
            xr, xi = (ar * xr - ai * xi + bu_ref[2 * c, rows, :],
                      ar * xi + ai * xr + bu_ref[2 * c + 1, rows, :])
            xb_ref[2 * c, rows, :] = xr.astype(BF16)
            xb_ref[2 * c + 1, rows, :] = xi.astype(BF16)
        st_ref[0, :, lo:lo + cw] = xr
        st_ref[1, :, lo:lo + cw] = xi
        y = (y + jnp.dot(xb_ref[2 * c], cc_ref[0, lo:lo + cw], preferred_element_type=F32)
             + jnp.dot(xb_ref[2 * c + 1], cc_ref[0, ns + lo:ns + lo + cw],
                       preferred_element_type=F32))
    y_ref[...] = y.reshape(tt, bsz, S5_SLAB).astype(y_ref.dtype)


def _s5_ssm(proj_tm, bd_b, a_bar, bd_c, d_skip, *, tt=32, chunk_lanes=512):
    s, b, _ = proj_tm.shape
    tt = min(tt, s)
    assert s % tt == 0 and b % 16 == 0
    n_slab = MIX_WIDTH // S5_SLAB
    ns = S5_SLAB_STATE
    n_buf = 2 * ns // chunk_lanes
    return pl.pallas_call(
        functools.partial(_s5_ssm_kernel, bsz=b, tt=tt, chunk_lanes=chunk_lanes),
        grid=(n_slab, s // tt),
        in_specs=[
            pl.BlockSpec((tt, b, S5_SLAB), lambda sl, t: (t, 0, sl)),
            pl.BlockSpec((1, S5_SLAB, 2 * ns), lambda sl, t: (sl, 0, 0)),
            pl.BlockSpec((1, 2, ns), lambda sl, t: (sl, 0, 0)),
            pl.BlockSpec((1, 2 * ns, S5_SLAB), lambda sl, t: (sl, 0, 0)),
            pl.BlockSpec((1, 1, S5_SLAB), lambda sl, t: (sl, 0, 0)),
        ],
        out_specs=pl.BlockSpec((tt, b, S5_SLAB), lambda sl, t: (t, 0, sl)),
        out_shape=jax.ShapeDtypeStruct((s, b, MIX_WIDTH), BF16),
        scratch_shapes=[pltpu.VMEM((2, b, ns), F32),
                        pltpu.VMEM((n_buf, b * tt, chunk_lanes), F32),
                        pltpu.VMEM((n_buf, b * tt, chunk_lanes), BF16)],
        compiler_params=_params("parallel", "arbitrary"),
        name="s5_ssm",
    )(proj_tm, bd_b, a_bar, bd_c, d_skip)


def _s5_discretize(a_re, a_im, log_dt, b_re, b_im, c_re, c_im, d_skip):
    lr, li = a_re.astype(F32), a_im.astype(F32)
    dt = jnp.exp(log_dt.astype(F32))[:, None]
    mag = jnp.exp(lr * dt)
    ab_re, ab_im = mag * jnp.cos(li * dt), mag * jnp.sin(li * dt)
    den = lr * lr + li * li
    f_re = ((ab_re - 1.0) * lr + ab_im * li) / den
    f_im = (ab_im * lr - (ab_re - 1.0) * li) / den
    br, bi = b_re.astype(F32), b_im.astype(F32)
    bb_re = f_re[..., None] * br - f_im[..., None] * bi
    bb_im = f_re[..., None] * bi + f_im[..., None] * br
    n_slab = MIX_WIDTH // S5_SLAB
    gs = S5_SLAB // S5_GROUP
    eye = jnp.eye(gs, dtype=F32)

    def pack_b(m):
        m = m.reshape(n_slab, gs, S5_STATE, S5_GROUP)
        return jnp.einsum('sgpc,gh->sgchp', m, eye).reshape(n_slab, S5_SLAB, S5_SLAB_STATE)

    def pack_c(m):
        m = m.reshape(n_slab, gs, S5_GROUP, S5_STATE)
        return jnp.einsum('sgcp,gh->shpgc', m, eye).reshape(n_slab, S5_SLAB_STATE, S5_SLAB)

    bd_b = jnp.concatenate([pack_b(bb_re), pack_b(bb_im)], axis=2).astype(BF16)
    bd_c = jnp.concatenate([pack_c(c_re.astype(F32)), pack_c(-c_im.astype(F32))], axis=1).astype(BF16)
    a_bar = jnp.stack([ab_re.reshape(n_slab, S5_SLAB_STATE), ab_im.reshape(n_slab, S5_SLAB_STATE)],
                      axis=1)
    return bd_b, a_bar, bd_c, d_skip.astype(F32).reshape(n_slab, 1, S5_SLAB)


def _glu_kernel(y_ref, w_ref, b_ref, o_ref, *, chunks):
    cm = y_ref.shape[0] // chunks
    for c in range(chunks):
        rows = slice(c * cm, (c + 1) * cm)
        g = jax.nn.gelu(y_ref[rows, :].astype(F32))
        r = jnp.dot(g.astype(BF16), w_ref[...], preferred_element_type=F32) + b_ref[...]
        o_ref[rows, :] = (g * _sigmoid(r)).astype(o_ref.dtype)


def _glu(y_tm, w, bias, bsz, *, tm=512, chunks=4):
    s = y_tm.shape[0]
    e = w.shape[0]
    tm = min(tm, s)
    assert s % tm == 0 and tm % chunks == 0
    tps = s // tm
    return pl.pallas_call(
        functools.partial(_glu_kernel, chunks=chunks),
        grid=(bsz, tps),
        in_specs=[
            pl.BlockSpec((tm, e), lambda b, i: (i, b)),
            pl.BlockSpec((e, e), lambda b, i: (0, 0), pipeline_mode=pl.Buffered(1)),
            pl.BlockSpec((1, e), lambda b, i: (0, 0)),
        ],
        out_specs=pl.BlockSpec((tm, e), lambda b, i: (b * tps + i, 0)),
        out_shape=jax.ShapeDtypeStruct((bsz * s, e), BF16),
        compiler_params=_params("parallel", "parallel"),
        name="s5_glu",
    )(y_tm, w, bias.reshape(1, e).astype(F32))


def _s5_layer(x2d, h2d, bsz, seq, post_g, next_g, w_in, a_re, a_im, log_dt, b_re, b_im, c_re,
              c_im, d_skip, w_glu, b_glu, w_out):
    e = MIX_WIDTH
    proj = _in_proj(h2d, w_in.astype(BF16), seq=seq)
    bd_b, a_bar, bd_c, d3 = _s5_discretize(a_re, a_im, log_dt, b_re, b_im, c_re, c_im, d_skip)
    y = _s5_ssm(proj.reshape(seq, bsz, 2 * e), bd_b, a_bar, bd_c, d3)
    y = _glu(y.reshape(seq, bsz * e), w_glu.astype(BF16), b_glu, bsz)
    return _gate_out(y, proj, 1, x2d, w_out.astype(BF16), post_g, next_g, bsz, seq,
                     z_time_major=True)


def _attn_projections(h2d, seq, w_in):
    e = MIX_WIDTH
    w_qkz = jnp.concatenate([w_in[:, :2 * e], w_in[:, 3 * e:]], axis=1).astype(BF16)
    proj = _in_proj(h2d, w_qkz)
    vt = _in_proj_t(h2d, w_in[:, 2 * e:3 * e].T.astype(BF16), seq)
    return proj, vt


def _diff_layer(x2d, h2d, bsz, seq, layer_idx, post_g, next_g, w_in, lq1, lk1, lq2, lk2, subln,
                w_out):
    e = MIX_WIDTH
    proj, vt = _attn_projections(h2d, seq, w_in)
    lam_init = 0.8 - 0.6 * math.exp(-0.3 * layer_idx)
    lam = (jnp.exp(jnp.sum(lq1.astype(F32) * lk1.astype(F32)))
           - jnp.exp(jnp.sum(lq2.astype(F32) * lk2.astype(F32))) + lam_init).reshape(1)
    o = _diff_attention(proj.reshape(bsz, seq, 3 * e), vt, _alibi_slopes(DIFF_HEADS), lam, subln,
                        1.0 - lam_init)
    return _gate_out(o.reshape(bsz * seq, e), proj, 2, x2d, w_out.astype(BF16), post_g, next_g,
                     bsz, seq)


def _moba_layer(x2d, h2d, bsz, seq, post_g, next_g, w_in, w_out):
    e = MIX_WIDTH
    proj, vt = _attn_projections(h2d, seq, w_in)
    o = _moba_attention(proj.reshape(bsz, seq, 3 * e), vt, _alibi_slopes(MOBA_HEADS))
    return _gate_out(o.reshape(bsz * seq, e), proj, 2, x2d, w_out.astype(BF16), post_g, next_g,
                     bsz, seq)


def _swa_layer(x2d, h2d, bsz, seq, post_g, next_g, w_in, sinks, w_out):
    e = MIX_WIDTH
    dh = SWA_HEAD_DIM
    nqc, nkc = SWA_Q_HEADS * dh, SWA_KV_HEADS * dh
    wq, wk = w_in[:, :nqc], w_in[:, nqc:nqc + nkc]
    wv, wz = w_in[:, nqc + nkc:nqc + 2 * nkc], w_in[:, nqc + 2 * nkc:]

    def dup(w):
        w = w.reshape(-1, SWA_KV_HEADS, 1, dh)
        return jnp.broadcast_to(w, (w.shape[0], SWA_KV_HEADS, 2, dh)).reshape(-1, 2 * nkc)

    w_cat = jnp.concatenate([wq, wz, dup(wk), dup(wv)], axis=1).astype(BF16)
    proj = _in_proj(h2d, w_cat)
    o = _swa_attention(proj.reshape(bsz, seq, w_cat.shape[1]), _alibi_slopes(SWA_Q_HEADS),
                       sinks.astype(F32))
    return _gate_out(o.reshape(bsz * seq, e), proj, 1, x2d, w_out.astype(BF16), post_g, next_g,
                     bsz, seq)


def kernel(x, pre_norm, post_norm, s5_w_in, s5_a_re, s5_a_im, s5_log_dt, s5_b_re, s5_b_im, s5_c_re, s5_c_im, s5_d, s5_w_glu, s5_b_glu, s5_w_out, diff_w_in, diff_lq1, diff_lk1, diff_lq2, diff_lk2, diff_subln, diff_w_out, moba_w_in, moba_w_out, swa_w_in, swa_sinks, swa_w_out):
    bsz, seq, d = x.shape
    depth = pre_norm.shape[0]
    x2d = x.reshape(bsz * seq, d)
    h2d = _prenorm(x2d, pre_norm[0])
    for i in range(depth):
        kind, j = i % 4, i // 4
        post_g = post_norm[i]
        next_g = pre_norm[i + 1] if i + 1 < depth else None
        if kind == 0:
            x2d, h2d = _s5_layer(x2d, h2d, bsz, seq, post_g, next_g, s5_w_in[j], s5_a_re[j],
                                 s5_a_im[j], s5_log_dt[j], s5_b_re[j], s5_b_im[j], s5_c_re[j],
                                 s5_c_im[j], s5_d[j], s5_w_glu[j], s5_b_glu[j], s5_w_out[j])
        elif kind == 1:
            x2d, h2d = _diff_layer(x2d, h2d, bsz, seq, i, post_g, next_g, diff_w_in[j],
                                   diff_lq1[j], diff_lk1[j], diff_lq2[j], diff_lk2[j],
                                   diff_subln[j], diff_w_out[j])
        elif kind == 2:
            x2d, h2d = _moba_layer(x2d, h2d, bsz, seq, post_g, next_g, moba_w_in[j],
                                   moba_w_out[j])
        else:
            x2d, h2d = _swa_layer(x2d, h2d, bsz, seq, post_g, next_g, swa_w_in[j], swa_sinks[j],
                                  swa_w_out[j])
    return x2d.reshape(bsz, seq, d)
```

```python
import functools
import math

import jax
import jax.numpy as jnp
from jax import lax
from jax.experimental import pallas as pl
from jax.experimental.pallas import tpu as pltpu

F32 = jnp.float32
BF16 = jnp.bfloat16

D_MODEL = 2048
MIX_WIDTH = D_MODEL
RMS_EPS = 1e-6
NEG_INF = -1e30
LOG2E = math.log2(math.e)
LANES = 128

S5_GROUP = 16
S5_GROUPS = MIX_WIDTH // S5_GROUP
S5_STATE = 64
S5_SLAB = 256
S5_SLAB_STATE = (S5_SLAB // S5_GROUP) * S5_STATE

DIFF_HEADS = 16
DIFF_HEAD_DIM = 64
MOBA_HEADS = 16
MOBA_HEAD_DIM = 128
MOBA_BLOCK = 256
MOBA_TOPK = 3
SWA_Q_HEADS = 32
SWA_KV_HEADS = 4
SWA_HEAD_DIM = 64
SWA_WINDOW = 128

VMEM_LIMIT = 56 * 1024 * 1024


def _params(*sem):
    return pltpu.CompilerParams(dimension_semantics=sem, vmem_limit_bytes=VMEM_LIMIT)


def _alibi_slopes(n_heads):
    return 2.0 ** (-8.0 * jnp.arange(1, n_heads + 1, dtype=F32) / n_heads)


def _dot_nt(a, b):
    return lax.dot_general(a, b, (((1,), (1,)), ((), ())), preferred_element_type=F32)


def _sigmoid(x):
    return 0.5 * jnp.tanh(0.5 * x) + 0.5


def _rms_normalize(x, gain):
    ms = jnp.mean(x * x, axis=-1, keepdims=True)
    return x * lax.rsqrt(ms + RMS_EPS) * gain


def _prenorm_kernel(x_ref, g_ref, h_ref):
    h_ref[...] = _rms_normalize(x_ref[...], g_ref[...]).astype(BF16)


def _prenorm(x2d, gain, *, tm=512):
    t, d = x2d.shape
    tm = min(tm, t)
    assert t % tm == 0
    return pl.pallas_call(
        _prenorm_kernel,
        grid=(t // tm,),
        in_specs=[pl.BlockSpec((tm, d), lambda i: (i, 0)), pl.BlockSpec((1, d), lambda i: (0, 0))],
        out_specs=pl.BlockSpec((tm, d), lambda i: (i, 0)),
        out_shape=jax.ShapeDtypeStruct((t, d), BF16),
        compiler_params=_params("parallel"),
        name="prenorm",
    )(x2d, gain.reshape(1, d).astype(F32))


def _col_tile(n, cap=2048):
    return max(t for t in range(256, cap + 1, 256) if n % t == 0)


def _matmul_kernel(h_ref, w_ref, o_ref):
    o_ref[...] = jnp.dot(h_ref[...], w_ref[...], preferred_element_type=F32).astype(o_ref.dtype)


def _in_proj(h2d, w, *, skip_block=None, tm=1024):
    t, d = h2d.shape
    tm = min(tm, t)
    tn = _col_tile(w.shape[1])
    assert t % tm == 0
    if skip_block is None:
        n = w.shape[1]
        w_map = lambda j, i: (0, j)
    else:
        n = w.shape[1] - tn
        w_map = lambda j, i: (0, j + (j >= skip_block).astype(jnp.int32))
    return pl.pallas_call(
        _matmul_kernel,
        grid=(n // tn, t // tm),
        in_specs=[
            pl.BlockSpec((tm, d), lambda j, i: (i, 0)),
            pl.BlockSpec((d, tn), w_map),
        ],
        out_specs=pl.BlockSpec((tm, tn), lambda j, i: (i, j)),
        out_shape=jax.ShapeDtypeStruct((t, n), BF16),
        compiler_params=_params("parallel", "parallel"),
        name="in_proj",
    )(h2d, w)


def _matmul_t_kernel(h_ref, wt_ref, o_ref):
    o_ref[0] = _dot_nt(wt_ref[...], h_ref[...]).astype(o_ref.dtype)


def _in_proj_t(h2d, wt, seq, *, tm=1024):
    t, d = h2d.shape
    n = wt.shape[0]
    tm = min(tm, seq)
    tn = _col_tile(n)
    assert seq % tm == 0
    tps = seq // tm
    return pl.pallas_call(
        _matmul_t_kernel,
        grid=(n // tn, t // tm),
        in_specs=[
            pl.BlockSpec((tm, d), lambda j, i: (i, 0)),
            pl.BlockSpec((tn, d), lambda j, i: (j, 0)),
        ],
        out_specs=pl.BlockSpec((1, tn, tm), lambda j, i: (i // tps, j, i % tps)),
        out_shape=jax.ShapeDtypeStruct((t // seq, n, seq), BF16),
        compiler_params=_params("parallel", "parallel"),
        name="in_proj_t",
    )(h2d, wt)


def _gate_out_kernel(o_ref, z_ref, x_ref, w_ref, g_ref, gn_ref, out_ref, *h_ref, chunks):
    cm = o_ref.shape[0] // chunks
    for c in range(chunks):
        rows = slice(c * cm, (c + 1) * cm)
        z = z_ref[rows, :].astype(F32)
        y = (o_ref[rows, :].astype(F32) * (z * _sigmoid(z))).astype(BF16)
        r = jnp.dot(y, w_ref[...], preferred_element_type=F32)
        xn = x_ref[rows, :] + _rms_normalize(r, g_ref[...])
        out_ref[rows, :] = xn
        if h_ref:
            h_ref[0][rows, :] = _rms_normalize(xn, gn_ref[...]).astype(BF16)


def _gate_out(o2d, zsrc, z_block, x2d, w, gain, next_gain, *, tm=512, chunks=4):
    t, e = o2d.shape
    d = w.shape[1]
    tm = min(tm, t)
    assert t % tm == 0 and tm % chunks == 0
    row_map = lambda i: (i, 0)
    const = lambda i: (0, 0)
    z_map = lambda i: (i, z_block)
    with_h = next_gain is not None
    gn = (next_gain if with_h else gain).reshape(1, d).astype(F32)
    out_specs = [pl.BlockSpec((tm, d), row_map)]
    out_shape = [jax.ShapeDtypeStruct((t, d), F32)]
    if with_h:
        out_specs.append(pl.BlockSpec((tm, d), row_map))
        out_shape.append(jax.ShapeDtypeStruct((t, d), BF16))
    res = pl.pallas_call(
        functools.partial(_gate_out_kernel, chunks=chunks),
        grid=(t // tm,),
        in_specs=[
            pl.BlockSpec((tm, e), row_map),
            pl.BlockSpec((tm, e), z_map),
            pl.BlockSpec((tm, d), row_map),
            pl.BlockSpec((e, d), const, pipeline_mode=pl.Buffered(1)),
            pl.BlockSpec((1, d), const),
            pl.BlockSpec((1, d), const),
        ],
        out_specs=out_specs,
        out_shape=out_shape,
        compiler_params=_params("parallel"),
        name="gate_out",
    )(o2d, zsrc, x2d, w, gain.reshape(1, d).astype(F32), gn)
    return (res[0], res[1]) if with_h else (res[0], None)


BIG = 1e30
SUM_ROWS = 16


def _softmax_step_t(s_ref, c, ok, vt, m_ref, acc_ref):
    m_old = m_ref[...]
    mt = jnp.max(s_ref[...], axis=0, keepdims=True) + c
    if ok is not None:
        mt = jnp.where(ok, mt, NEG_INF)
    m_new = jnp.maximum(m_old, mt)
    alpha = jnp.exp2(m_old - m_new)
    r = m_new - c
    if ok is not None:
        r = jnp.where(ok, r, BIG)
    p = jnp.exp2(s_ref[...] - r).astype(BF16)
    vt1 = jnp.concatenate([vt, jnp.ones((SUM_ROWS, vt.shape[1]), BF16)], axis=0)
    acc_ref[...] = alpha * acc_ref[...] + jnp.dot(vt1, p, preferred_element_type=F32)
    m_ref[...] = m_new


def _paired_causal_sweep(n_blocks, init, scores, finish, finalize):
    n_past = n_blocks - 1

    def pair(i, carry):
        blk_a, blk_b = i, n_blocks - 1 - i
        init()
        tiles = []
        for p in range(n_past):
            in_a = p < i
            tiles.append((jnp.where(in_a, blk_a, blk_b), jnp.where(in_a, p, p - i),
                          jnp.where(in_a, 0, 1), False))
        tiles += [(blk_a, blk_a, 0, True), (blk_b, blk_b, 1, True)]
        scores(tiles[0][0], tiles[0][1], 0)
        for p, (qb, j, slot, diagonal) in enumerate(tiles):
            if p + 1 < len(tiles):
                scores(tiles[p + 1][0], tiles[p + 1][1], (p + 1) % 2)
            finish(qb, j, slot, p % 2, diagonal)
        finalize(blk_a, 0)
        finalize(blk_b, 1)
        return carry

    lax.fori_loop(0, n_blocks // 2, pair, 0)


def _diff_attn_kernel(slopes_ref, lam_ref, q_ref, k_ref, vt_ref, g_ref, o_ref,
                      m_ref, acc_ref, bias_ref, qq_ref, s_ref, *, tq, hp, nqb, out_scale):
    hg = pl.program_id(1)
    lam = lam_ref[0]
    d = DIFF_HEAD_DIM
    w = 2 * d
    reps = 2 * tq // LANES

    lane = lax.broadcasted_iota(jnp.int32, (tq, w), 1)
    krow = lax.broadcasted_iota(jnp.int32, (tq, LANES), 0).astype(F32)
    slope2 = [slopes_ref[hg * hp + hh] * LOG2E for hh in range(hp)]
    for hh in range(hp):
        bias_ref[hh] = slope2[hh] * krow

    def prep(qb, carry):
        rows = pl.ds(pl.multiple_of(qb * tq, tq), tq)
        for hh in range(hp):
            q = q_ref[0, rows, hh * w:(hh + 1) * w].astype(F32) * (d ** -0.5 * LOG2E)
            qq_ref[qb, hh, :tq] = jnp.where(lane < d, q, 0.0).astype(BF16)
            qq_ref[qb, hh, tq:] = jnp.where(lane >= d, q, 0.0).astype(BF16)
        return carry

    lax.fori_loop(0, nqb, prep, 0)

    def init():
        m_ref[...] = jnp.full_like(m_ref, NEG_INF)
        acc_ref[...] = jnp.zeros_like(acc_ref)

    def scores(qb, j, buf):
        keys = pl.ds(pl.multiple_of(j * tq, tq), tq)
        for hh in range(hp):
            k = k_ref[0, keys, hh * w:(hh + 1) * w]
            s_ref[buf, hh] = (_dot_nt(k, qq_ref[qb, hh])
                              + jnp.concatenate([bias_ref[hh]] * reps, axis=1))

    def finish(qb, j, slot, buf, diagonal):
        keys = pl.ds(pl.multiple_of(j * tq, tq), tq)
        for hh in range(hp):
            vt = vt_ref[0, hh * w:(hh + 1) * w, keys]
            if diagonal:
                kr = lax.broadcasted_iota(jnp.int32, (tq, 2 * tq), 0)
                qc = lax.broadcasted_iota(jnp.int32, (tq, 2 * tq), 1)
                qc = jnp.where(qc >= tq, qc - tq, qc)
                s_ref[buf, hh] = jnp.where(qc >= kr, s_ref[buf, hh], NEG_INF)
            c = slope2[hh] * ((j - qb) * tq).astype(F32)
            _softmax_step_t(s_ref.at[buf, hh], c, None, vt, m_ref.at[slot, hh],
                            acc_ref.at[slot, hh])

    g = jnp.concatenate([g_ref[...]] * (tq // LANES), axis=1)

    def finalize(qb, slot):
        rows = pl.ds(pl.multiple_of(qb * tq, tq), tq)
        for hh in range(hp):
            a = acc_ref[slot, hh, :w] / acc_ref[slot, hh, w:w + 1]
            o = a[:, :tq] - lam * a[:, tq:]
            ms = jnp.mean(o * o, axis=0, keepdims=True)
            o = o * lax.rsqrt(ms + RMS_EPS) * g * out_scale
            o_ref[0, rows, hh * w:(hh + 1) * w] = o.T.astype(o_ref.dtype)

    _paired_causal_sweep(nqb, init, scores, finish, finalize)


def _diff_attention(proj, vt, slopes, lam, subln, out_scale, *, tq=256, hp=2):
    b, s, _ = proj.shape
    nh = DIFF_HEADS
    w = 2 * DIFF_HEAD_DIM
    tq = min(tq, s // 2)
    nqb = s // tq
    assert s % tq == 0 and tq % LANES == 0 and nh % hp == 0 and nqb % 2 == 0
    ng = nh // hp
    smem = pl.BlockSpec(memory_space=pltpu.SMEM)
    g = jnp.broadcast_to(subln.astype(F32)[:, None], (w, LANES))
    return pl.pallas_call(
        functools.partial(_diff_attn_kernel, tq=tq, hp=hp, nqb=nqb, out_scale=out_scale),
        grid=(b, ng),
        in_specs=[
            smem, smem,
            pl.BlockSpec((1, s, hp * w), lambda bi, h: (bi, 0, h)),
            pl.BlockSpec((1, s, hp * w), lambda bi, h: (bi, 0, ng + h)),
            pl.BlockSpec((1, hp * w, s), lambda bi, h: (bi, h, 0)),
            pl.BlockSpec((w, LANES), lambda bi, h: (0, 0)),
        ],
        out_specs=pl.BlockSpec((1, s, hp * w), lambda bi, h: (bi, 0, h)),
        out_shape=jax.ShapeDtypeStruct((b, s, MIX_WIDTH), BF16),
        scratch_shapes=[pltpu.VMEM((2, hp, 1, 2 * tq), F32),
                        pltpu.VMEM((2, hp, w + SUM_ROWS, 2 * tq), F32),
                        pltpu.VMEM((hp, tq, LANES), F32),
                        pltpu.VMEM((nqb, hp, 2 * tq, w), BF16),
                        pltpu.VMEM((2, hp, tq, 2 * tq), F32)],
        compiler_params=_params("parallel", "parallel"),
        name="diff_attn",
    )(slopes, lam, proj, proj, vt, g)


def _moba_attn_kernel(slopes_ref, q_ref, k_ref, vt_ref, o_ref, m_ref, acc_ref, bias_ref,
                      qs_ref, sel_ref, s_ref, *, nb, hp):
    hg = pl.program_id(1)
    blk = MOBA_BLOCK
    dh = MOBA_HEAD_DIM
    seq = nb * blk
    reps = blk // LANES

    krow = lax.broadcasted_iota(jnp.int32, (blk, LANES), 0).astype(F32)
    bidx = lax.broadcasted_iota(jnp.int32, (nb, seq), 0)
    qblk = lax.broadcasted_iota(jnp.int32, (nb, seq), 1) // blk
    slope2 = [slopes_ref[hg * hp + hh] * LOG2E for hh in range(hp)]
    for hh in range(hp):
        bias_ref[hh] = slope2[hh] * krow
        q = q_ref[0, :, hh * dh:(hh + 1) * dh]
        qs_ref[hh] = (q.astype(F32) * (dh ** -0.5 * LOG2E)).astype(BF16)

        kh = k_ref[0, :, hh * dh:(hh + 1) * dh].astype(F32)
        kmean = jnp.mean(kh.reshape(nb, blk, dh), axis=1)
        km_hi = kmean.astype(BF16)
        km_lo = (kmean - km_hi.astype(F32)).astype(BF16)
        gate = _dot_nt(km_hi, q) + _dot_nt(km_lo, q)

        rank = jnp.zeros((nb, seq), jnp.int32)
        for jp in range(nb):
            gj = gate[jp:jp + 1, :]
            before = (gj > gate) | ((gj == gate) & (jp < bidx))
            rank = rank + jnp.where(before, 1, 0) * jnp.where(jp < qblk, 1, 0)
        sel_ref[hh] = jnp.where((rank < MOBA_TOPK) & (bidx < qblk), 1.0, 0.0)

    def init():
        m_ref[...] = jnp.full_like(m_ref, NEG_INF)
        acc_ref[...] = jnp.zeros_like(acc_ref)

    def scores(qb, j, buf):
        keys = pl.ds(pl.multiple_of(j * blk, blk), blk)
        rows = pl.ds(pl.multiple_of(qb * blk, blk), blk)
        for hh in range(hp):
            k = k_ref[0, keys, hh * dh:(hh + 1) * dh]
            s_ref[buf, hh] = (_dot_nt(k, qs_ref[hh, rows, :])
                              + jnp.concatenate([bias_ref[hh]] * reps, axis=1))

    def finish(qb, j, slot, buf, diagonal):
        keys = pl.ds(pl.multiple_of(j * blk, blk), blk)
        cols = pl.ds(pl.multiple_of(qb * blk, blk), blk)
        for hh in range(hp):
            vt = vt_ref[0, hh * dh:(hh + 1) * dh, keys]
            c = slope2[hh] * ((j - qb) * blk).astype(F32)
            if diagonal:
                kr = lax.broadcasted_iota(jnp.int32, (blk, blk), 0)
                qc = lax.broadcasted_iota(jnp.int32, (blk, blk), 1)
                s_ref[buf, hh] = jnp.where(qc >= kr, s_ref[buf, hh], NEG_INF)
                ok = None
            else:
                tile_idx = lax.broadcasted_iota(jnp.int32, (nb, blk), 0)
                chosen = jnp.where(tile_idx == j, sel_ref[hh, :, cols], 0.0)
                ok = jnp.max(chosen, axis=0, keepdims=True) > 0.0
            _softmax_step_t(s_ref.at[buf, hh], c, ok, vt, m_ref.at[slot, hh],
                            acc_ref.at[slot, hh])

    def finalize(qb, slot):
        rows = pl.ds(pl.multiple_of(qb * blk, blk), blk)
        for hh in range(hp):
            o = acc_ref[slot, hh, :dh] / acc_ref[slot, hh, dh:dh + 1]
            o_ref[0, rows, hh * dh:(hh + 1) * dh] = o.T.astype(o_ref.dtype)

    _paired_causal_sweep(nb, init, scores, finish, finalize)


def _moba_attention(proj, vt, slopes, *, hp=4):
    b, s, _ = proj.shape
    nh = MOBA_HEADS
    dh = MOBA_HEAD_DIM
    blk = MOBA_BLOCK
    nb = s // blk
    assert s % blk == 0 and nh % hp == 0 and nb % 2 == 0
    ng = nh // hp
    smem = pl.BlockSpec(memory_space=pltpu.SMEM)
    return pl.pallas_call(
        functools.partial(_moba_attn_kernel, nb=nb, hp=hp),
        grid=(b, ng),
        in_specs=[
            smem,
            pl.BlockSpec((1, s, hp * dh), lambda bi, h: (bi, 0, h)),
            pl.BlockSpec((1, s, hp * dh), lambda bi, h: (bi, 0, ng + h)),
            pl.BlockSpec((1, hp * dh, s), lambda bi, h: (bi, h, 0)),
        ],
        out_specs=pl.BlockSpec((1, s, hp * dh), lambda bi, h: (bi, 0, h)),
        out_shape=jax.ShapeDtypeStruct((b, s, MIX_WIDTH), BF16),
        scratch_shapes=[pltpu.VMEM((2, hp, 1, blk), F32),
                        pltpu.VMEM((2, hp, dh + SUM_ROWS, blk), F32),
                        pltpu.VMEM((hp, blk, LANES), F32),
                        pltpu.VMEM((hp, s, dh), BF16),
                        pltpu.VMEM((hp, nb, s), F32),
                        pltpu.VMEM((2, hp, blk, blk), F32)],
        compiler_params=_params("parallel", "parallel"),
        name="moba_attn",
    )(slopes, proj, proj, vt)


def _swa_attn_kernel(slopes_ref, sinks_ref, q_ref, k_ref, vt_ref, o_ref, bm_ref, *, sub_blocks):
    i = pl.program_id(1)
    win = SWA_WINDOW
    dh = SWA_HEAD_DIM
    grp = SWA_Q_HEADS // SWA_KV_HEADS
    n_pairs = SWA_Q_HEADS // 2
    tile = (2 * win, 2 * win)
    left = lax.broadcasted_iota(jnp.int32, tile, 1) < win

    @pl.when(i == 0)
    def _():
        kr = lax.broadcasted_iota(jnp.int32, tile, 0)
        delta = lax.broadcasted_iota(jnp.int32, tile, 1) % win - kr
        own_only = delta >= 0
        prev_own = (delta >= -win) & (delta < 0)
        dist0 = delta.astype(F32)
        for p in range(n_pairs):
            slope2 = jnp.where(left, slopes_ref[2 * p], slopes_ref[2 * p + 1]) * LOG2E
            bm_ref[0, p] = jnp.where(own_only, -slope2 * dist0, NEG_INF)
            bm_ref[1, p] = jnp.where(prev_own, -slope2 * (dist0 + win), NEG_INF)

    lane = lax.broadcasted_iota(jnp.int32, (win, 2 * dh), 1)
    col_left = lax.broadcasted_iota(jnp.int32, (1, 2 * win), 1) < win
    ones = jnp.ones((SUM_ROWS, 2 * win), BF16)

    def key_rows(sb):
        r0 = (i * sub_blocks + sb) * win
        return r0, pl.ds(pl.multiple_of(jnp.maximum(r0 - win, 0), win), 2 * win)

    def scores(sb, kb, half, pr):
        r0, keys = key_rows(sb)
        h0 = (2 * kb + half) * grp + 2 * pr
        k = k_ref[0, keys, kb * 2 * dh:(kb + 1) * 2 * dh]
        q = q_ref[0, sb * win:(sb + 1) * win, h0 * dh:(h0 + 2) * dh]
        q = q.astype(F32) * (dh ** -0.5 * LOG2E)
        qr = pltpu.roll(q, dh, axis=1)
        if half == 0:
            top, bot = jnp.where(lane < dh, q, 0.0), jnp.where(lane < dh, qr, 0.0)
        else:
            top, bot = jnp.where(lane >= dh, qr, 0.0), jnp.where(lane >= dh, q, 0.0)
        qq = jnp.concatenate([top, bot], axis=0).astype(BF16)
        return _dot_nt(k, qq) + bm_ref[jnp.where(r0 == 0, 0, 1), h0 // 2]

    def finish(s, sb, kb, half, pr):
        _, keys = key_rows(sb)
        h0 = (2 * kb + half) * grp + 2 * pr
        vt1 = jnp.concatenate([vt_ref[0, kb * 2 * dh:(kb + 1) * 2 * dh, keys], ones], axis=0)
        sink = jnp.where(col_left, sinks_ref[h0], sinks_ref[h0 + 1]) * LOG2E
        m = jnp.maximum(jnp.max(s, axis=0, keepdims=True), sink)
        p = jnp.exp2(s - m).astype(BF16)
        acc = jnp.dot(vt1, p, preferred_element_type=F32)
        denom = acc[2 * dh:2 * dh + 1] + jnp.exp2(sink - m)
        o = acc[half * dh:(half + 1) * dh] / denom
        o = jnp.concatenate([o[:, :win], o[:, win:]], axis=0)
        o_ref[0, sb * win:(sb + 1) * win, h0 * dh:(h0 + 2) * dh] = o.T.astype(o_ref.dtype)

    tiles = [(sb, kb, half, pr) for sb in range(sub_blocks) for kb in range(SWA_KV_HEADS // 2)
             for half in range(2) for pr in range(grp // 2)]
    s_next = scores(*tiles[0])
    for n, t in enumerate(tiles):
        s_cur = s_next
        if n + 1 < len(tiles):
            s_next = scores(*tiles[n + 1])
        finish(s_cur, *t)


def _swa_attention(proj, vt, slopes, sinks, *, sub_blocks=2):
    b, s, _ = proj.shape
    win = SWA_WINDOW
    e = MIX_WIDTH
    kvw = SWA_KV_HEADS * SWA_HEAD_DIM
    tq = sub_blocks * win
    assert s % tq == 0 and s >= 2 * win
    smem = pl.BlockSpec(memory_space=pltpu.SMEM)
    return pl.pallas_call(
        functools.partial(_swa_attn_kernel, sub_blocks=sub_blocks),
        grid=(b, s // tq),
        in_specs=[
            smem, smem,
            pl.BlockSpec((1, tq, e), lambda bi, i: (bi, i, 0)),
            pl.BlockSpec((1, s, kvw), lambda bi, i: (bi, 0, 2 * e // kvw)),
            pl.BlockSpec((1, kvw, s), lambda bi, i: (bi, 0, 0)),
        ],
        out_specs=pl.BlockSpec((1, tq, e), lambda bi, i: (bi, i, 0)),
        out_shape=jax.ShapeDtypeStruct((b, s, e), BF16),
        scratch_shapes=[pltpu.VMEM((2, SWA_Q_HEADS // 2, 2 * win, 2 * win), F32)],
        compiler_params=_params("parallel", "arbitrary"),
        name="swa_attn",
    )(slopes, sinks, proj, proj, vt)


def _s5_ssm_kernel(u_ref, perm_ref, bb_ref, a_ref, cc_ref, d_ref, y_ref, st_ref, bu_ref, xb_ref,
                   *, bsz, tt, chunk_lanes):
    ns = S5_SLAB_STATE
    cw = chunk_lanes

    @pl.when(pl.program_id(1) == 0)
    def _():
        st_ref[...] = jnp.zeros_like(st_ref)

    u_bt = u_ref[...].reshape(bsz * tt, S5_SLAB)
    u32 = jnp.dot(perm_ref[0], u_bt, preferred_element_type=F32)
    u = u32.astype(BF16)
    y = d_ref[0] * u32
    for c in range(ns // cw):
        lo = c * cw
        bu_ref[2 * c] = jnp.dot(u, bb_ref[0, :, lo:lo + cw], preferred_element_type=F32)
        bu_ref[2 * c + 1] = jnp.dot(u, bb_ref[0, :, ns + lo:ns + lo + cw],
                                    preferred_element_type=F32)
        ar = jnp.broadcast_to(a_ref[0, 0:1, lo:lo + cw], (bsz, cw))
        ai = jnp.broadcast_to(a_ref[0, 1:2, lo:lo + cw], (bsz, cw))
        xr = st_ref[0, :, lo:lo + cw]
        xi = st_ref[1, :, lo:lo + cw]
        for t in range(tt):
            rows = slice(t * bsz, (t + 1) * bsz)
            xr, xi = (ar * xr - ai * xi + bu_ref[2 * c, rows, :],
                      ar * xi + ai * xr + bu_ref[2 * c + 1, rows, :])
            xb_ref[2 * c, rows, :] = xr.astype(BF16)
            xb_ref[2 * c + 1, rows, :] = xi.astype(BF16)
        st_ref[0, :, lo:lo + cw] = xr
        st_ref[1, :, lo:lo + cw] = xi
        y = (y + jnp.dot(xb_ref[2 * c], cc_ref[0, lo:lo + cw], preferred_element_type=F32)
             + jnp.dot(xb_ref[2 * c + 1], cc_ref[0, ns + lo:ns + lo + cw],
                       preferred_element_type=F32))
    y_bt = jnp.dot(perm_ref[1], y.astype(BF16), preferred_element_type=F32)
    y_ref[...] = y_bt.reshape(bsz, tt, S5_SLAB).astype(y_ref.dtype)


def _s5_ssm(proj, bd_b, a_bar, bd_c, d_skip, *, tt=32, chunk_lanes=512):
    b, s, _ = proj.shape
    tt = min(tt, s)
    assert s % tt == 0 and tt % 16 == 0
    n_slab = MIX_WIDTH // S5_SLAB
    ns = S5_SLAB_STATE
    n_buf = 2 * ns // chunk_lanes
    rows = b * tt
    src = (jnp.arange(rows) % b) * tt + jnp.arange(rows) // b
    fwd = jax.nn.one_hot(src, rows, dtype=BF16)
    perm = jnp.stack([fwd, fwd.T])
    return pl.pallas_call(
        functools.partial(_s5_ssm_kernel, bsz=b, tt=tt, chunk_lanes=chunk_lanes),
        grid=(n_slab, s // tt),
        in_specs=[
            pl.BlockSpec((b, tt, S5_SLAB), lambda sl, t: (0, t, sl)),
            pl.BlockSpec((2, rows, rows), lambda sl, t: (0, 0, 0)),
            pl.BlockSpec((1, S5_SLAB, 2 * ns), lambda sl, t: (sl, 0, 0)),
            pl.BlockSpec((1, 2, ns), lambda sl, t: (sl, 0, 0)),
            pl.BlockSpec((1, 2 * ns, S5_SLAB), lambda sl, t: (sl, 0, 0)),
            pl.BlockSpec((1, 1, S5_SLAB), lambda sl, t: (sl, 0, 0)),
        ],
        out_specs=pl.BlockSpec((b, tt, S5_SLAB), lambda sl, t: (0, t, sl)),
        out_shape=jax.ShapeDtypeStruct((b, s, MIX_WIDTH), BF16),
        scratch_shapes=[pltpu.VMEM((2, b, ns), F32),
                        pltpu.VMEM((n_buf, rows, chunk_lanes), F32),
                        pltpu.VMEM((n_buf, rows, chunk_lanes), BF16)],
        compiler_params=_params("parallel", "arbitrary"),
        name="s5_ssm",
    )(proj, perm, bd_b, a_bar, bd_c, d_skip)


def _s5_discretize(a_re, a_im, log_dt, b_re, b_im, c_re, c_im, d_skip):
    lr, li = a_re.astype(F32), a_im.astype(F32)
    dt = jnp.exp(log_dt.astype(F32))[:, None]
    mag = jnp.exp(lr * dt)
    ab_re, ab_im = mag * jnp.cos(li * dt), mag * jnp.sin(li * dt)
    den = lr * lr + li * li
    f_re = ((ab_re - 1.0) * lr + ab_im * li) / den
    f_im = (ab_im * lr - (ab_re - 1.0) * li) / den
    br, bi = b_re.astype(F32), b_im.astype(F32)
    bb_re = f_re[..., None] * br - f_im[..., None] * bi
    bb_im = f_re[..., None] * bi + f_im[..., None] * br
    n_slab = MIX_WIDTH // S5_SLAB
    gs = S5_SLAB // S5_GROUP
    eye = jnp.eye(gs, dtype=F32)

    def pack_b(m):
        m = m.reshape(n_slab, gs, S5_STATE, S5_GROUP)
        return jnp.einsum('sgpc,gh->sgchp', m, eye).reshape(n_slab, S5_SLAB, S5_SLAB_STATE)

    def pack_c(m):
        m = m.reshape(n_slab, gs, S5_GROUP, S5_STATE)
        return jnp.einsum('sgcp,gh->shpgc', m, eye).reshape(n_slab, S5_SLAB_STATE, S5_SLAB)

    bd_b = jnp.concatenate([pack_b(bb_re), pack_b(bb_im)], axis=2).astype(BF16)
    bd_c = jnp.concatenate([pack_c(c_re.astype(F32)), pack_c(-c_im.astype(F32))], axis=1).astype(BF16)
    a_bar = jnp.stack([ab_re.reshape(n_slab, S5_SLAB_STATE), ab_im.reshape(n_slab, S5_SLAB_STATE)],
                      axis=1)
    return bd_b, a_bar, bd_c, d_skip.astype(F32).reshape(n_slab, 1, S5_SLAB)


def _glu_kernel(y_ref, w_ref, b_ref, o_ref, *, chunks):
    cm = y_ref.shape[0] // chunks
    for c in range(chunks):
        rows = slice(c * cm, (c + 1) * cm)
        g = jax.nn.gelu(y_ref[rows, :].astype(F32))
        r = jnp.dot(g.astype(BF16), w_ref[...], preferred_element_type=F32) + b_ref[...]
        o_ref[rows, :] = (g * _sigmoid(r)).astype(o_ref.dtype)


def _glu(y2d, w, bias, *, tm=512, chunks=4):
    t, e = y2d.shape
    tm = min(tm, t)
    assert t % tm == 0 and tm % chunks == 0
    return pl.pallas_call(
        functools.partial(_glu_kernel, chunks=chunks),
        grid=(t // tm,),
        in_specs=[
            pl.BlockSpec((tm, e), lambda i: (i, 0)),
            pl.BlockSpec((e, e), lambda i: (0, 0), pipeline_mode=pl.Buffered(1)),
            pl.BlockSpec((1, e), lambda i: (0, 0)),
        ],
        out_specs=pl.BlockSpec((tm, e), lambda i: (i, 0)),
        out_shape=jax.ShapeDtypeStruct((t, e), BF16),
        compiler_params=_params("parallel"),
        name="s5_glu",
    )(y2d, w, bias.reshape(1, e).astype(F32))


def _s5_layer(x2d, h2d, bsz, seq, post_g, next_g, w_in, a_re, a_im, log_dt, b_re, b_im, c_re,
              c_im, d_skip, w_glu, b_glu, w_out):
    e = MIX_WIDTH
    proj = _in_proj(h2d, w_in.astype(BF16))
    bd_b, a_bar, bd_c, d3 = _s5_discretize(a_re, a_im, log_dt, b_re, b_im, c_re, c_im, d_skip)
    y = _s5_ssm(proj.reshape(bsz, seq, 2 * e), bd_b, a_bar, bd_c, d3)
    y = _glu(y.reshape(bsz * seq, e), w_glu.astype(BF16), b_glu)
    return _gate_out(y, proj, 1, x2d, w_out.astype(BF16), post_g, next_g)


def _attn_projections(h2d, seq, w_in):
    e = MIX_WIDTH
    proj = _in_proj(h2d, w_in.astype(BF16), skip_block=2)
    vt = _in_proj_t(h2d, w_in[:, 2 * e:3 * e].T.astype(BF16), seq)
    return proj, vt


def _diff_layer(x2d, h2d, bsz, seq, layer_idx, post_g, next_g, w_in, lq1, lk1, lq2, lk2, subln,
                w_out):
    e = MIX_WIDTH
    proj, vt = _attn_projections(h2d, seq, w_in)
    lam_init = 0.8 - 0.6 * math.exp(-0.3 * layer_idx)
    lam = (jnp.exp(jnp.sum(lq1.astype(F32) * lk1.astype(F32)))
           - jnp.exp(jnp.sum(lq2.astype(F32) * lk2.astype(F32))) + lam_init).reshape(1)
    o = _diff_attention(proj.reshape(bsz, seq, 3 * e), vt, _alibi_slopes(DIFF_HEADS), lam, subln,
                        1.0 - lam_init)
    return _gate_out(o.reshape(bsz * seq, e), proj, 2, x2d, w_out.astype(BF16), post_g, next_g)


def _moba_layer(x2d, h2d, bsz, seq, post_g, next_g, w_in, w_out):
    e = MIX_WIDTH
    proj, vt = _attn_projections(h2d, seq, w_in)
    o = _moba_attention(proj.reshape(bsz, seq, 3 * e), vt, _alibi_slopes(MOBA_HEADS))
    return _gate_out(o.reshape(bsz * seq, e), proj, 2, x2d, w_out.astype(BF16), post_g, next_g)


def _swa_layer(x2d, h2d, bsz, seq, post_g, next_g, w_in, sinks, w_out):
    e = MIX_WIDTH
    nqc, nkc = SWA_Q_HEADS * SWA_HEAD_DIM, SWA_KV_HEADS * SWA_HEAD_DIM
    wq, wk = w_in[:, :nqc], w_in[:, nqc:nqc + nkc]
    wv, wz = w_in[:, nqc + nkc:nqc + 2 * nkc], w_in[:, nqc + 2 * nkc:]
    w_cat = jnp.concatenate([wq, wz, wk, wv], axis=1).astype(BF16)
    proj = _in_proj(h2d, w_cat)
    vt = _in_proj_t(h2d, wv.T.astype(BF16), seq)
    o = _swa_attention(proj.reshape(bsz, seq, w_cat.shape[1]), vt, _alibi_slopes(SWA_Q_HEADS),
                       sinks.astype(F32))
    return _gate_out(o.reshape(bsz * seq, e), proj, 1, x2d, w_out.astype(BF16), post_g, next_g)


def kernel(x, pre_norm, post_norm, s5_w_in, s5_a_re, s5_a_im, s5_log_dt, s5_b_re, s5_b_im, s5_c_re, s5_c_im, s5_d, s5_w_glu, s5_b_glu, s5_w_out, diff_w_in, diff_lq1, diff_lk1, diff_lq2, diff_lk2, diff_subln, diff_w_out, moba_w_in, moba_w_out, swa_w_in, swa_sinks, swa_w_out):
    bsz, seq, d = x.shape
    depth = pre_norm.shape[0]
    x2d = x.reshape(bsz * seq, d)
    h2d = _prenorm(x2d, pre_norm[0])
    for i in range(depth):
        kind, j = i % 4, i // 4
        post_g = post_norm[i]
        next_g = pre_norm[i + 1] if i + 1 < depth else None
        if kind == 0:
            x2d, h2d = _s5_layer(x2d, h2d, bsz, seq, post_g, next_g, s5_w_in[j], s5_a_re[j],
                                 s5_a_im[j], s5_log_dt[j], s5_b_re[j], s5_b_im[j], s5_c_re[j],
                                 s5_c_im[j], s5_d[j], s5_w_glu[j], s5_b_glu[j], s5_w_out[j])
        elif kind == 1:
            x2d, h2d = _diff_layer(x2d, h2d, bsz, seq, i, post_g, next_g, diff_w_in[j],
                                   diff_lq1[j], diff_lk1[j], diff_lq2[j], diff_lk2[j],
                                   diff_subln[j], diff_w_out[j])
        elif kind == 2:
            x2d, h2d = _moba_layer(x2d, h2d, bsz, seq, post_g, next_g, moba_w_in[j],
                                   moba_w_out[j])
        else:
            x2d, h2d = _swa_layer(x2d, h2d, bsz, seq, post_g, next_g, swa_w_in[j], swa_sinks[j],
                                  swa_w_out[j])
    return x2d.reshape(bsz, seq, d)
```

```python
import functools
import math

import jax
import jax.numpy as jnp
from jax import lax
from jax.experimental import pallas as pl
from jax.experimental.pallas import tpu as pltpu

F32 = jnp.float32
BF16 = jnp.bfloat16

D_MODEL = 2048
MIX_WIDTH = D_MODEL
RMS_EPS = 1e-6
NEG_INF = -1e30
LOG2E = math.log2(math.e)
LANES = 128

S5_GROUP = 16
S5_GROUPS = MIX_WIDTH // S5_GROUP
S5_STATE = 64
S5_SLAB = 256
S5_SLAB_STATE = (S5_SLAB // S5_GROUP) * S5_STATE
PERM_STEPS = 16

DIFF_HEADS = 16
DIFF_HEAD_DIM = 64
MOBA_HEADS = 16
MOBA_HEAD_DIM = 128
MOBA_BLOCK = 256
MOBA_TOPK = 3
SWA_Q_HEADS = 32
SWA_KV_HEADS = 4
SWA_HEAD_DIM = 64
SWA_WINDOW = 128

VMEM_LIMIT = 56 * 1024 * 1024


def _params(*sem):
    return pltpu.CompilerParams(dimension_semantics=sem, vmem_limit_bytes=VMEM_LIMIT)


def _alibi_slopes(n_heads):
    return 2.0 ** (-8.0 * jnp.arange(1, n_heads + 1, dtype=F32) / n_heads)


def _dot_nt(a, b):
    return lax.dot_general(a, b, (((1,), (1,)), ((), ())), preferred_element_type=F32)


def _sigmoid(x):
    return 0.5 * jnp.tanh(0.5 * x) + 0.5


def _rms_normalize(x, gain):
    ms = jnp.mean(x * x, axis=-1, keepdims=True)
    return x * lax.rsqrt(ms + RMS_EPS) * gain


def _prenorm_kernel(x_ref, g_ref, h_ref):
    h_ref[...] = _rms_normalize(x_ref[...], g_ref[...]).astype(BF16)


def _prenorm(x2d, gain, *, tm=512):
    t, d = x2d.shape
    tm = min(tm, t)
    assert t % tm == 0
    return pl.pallas_call(
        _prenorm_kernel,
        grid=(t // tm,),
        in_specs=[pl.BlockSpec((tm, d), lambda i: (i, 0)), pl.BlockSpec((1, d), lambda i: (0, 0))],
        out_specs=pl.BlockSpec((tm, d), lambda i: (i, 0)),
        out_shape=jax.ShapeDtypeStruct((t, d), BF16),
        compiler_params=_params("parallel"),
        name="prenorm",
    )(x2d, gain.reshape(1, d).astype(F32))


def _col_tile(n, cap=2048):
    return max(t for t in range(256, cap + 1, 256) if n % t == 0)


def _matmul_kernel(h_ref, w_ref, o_ref):
    o_ref[...] = jnp.dot(h_ref[...], w_ref[...], preferred_element_type=F32).astype(o_ref.dtype)


def _in_proj(h2d, w, *, skip_block=None, tm=1024):
    t, d = h2d.shape
    tm = min(tm, t)
    tn = _col_tile(w.shape[1])
    assert t % tm == 0
    if skip_block is None:
        n = w.shape[1]
        w_map = lambda j, i: (0, j)
    else:
        n = w.shape[1] - tn
        w_map = lambda j, i: (0, j + (j >= skip_block).astype(jnp.int32))
    return pl.pallas_call(
        _matmul_kernel,
        grid=(n // tn, t // tm),
        in_specs=[
            pl.BlockSpec((tm, d), lambda j, i: (i, 0)),
            pl.BlockSpec((d, tn), w_map),
        ],
        out_specs=pl.BlockSpec((tm, tn), lambda j, i: (i, j)),
        out_shape=jax.ShapeDtypeStruct((t, n), BF16),
        compiler_params=_params("parallel", "parallel"),
        name="in_proj",
    )(h2d, w)


def _matmul_t_kernel(h_ref, wt_ref, o_ref):
    o_ref[0] = _dot_nt(wt_ref[...], h_ref[...]).astype(o_ref.dtype)


def _in_proj_t(h2d, wt, seq, *, tm=1024):
    t, d = h2d.shape
    n = wt.shape[0]
    tm = min(tm, seq)
    tn = _col_tile(n)
    assert seq % tm == 0
    tps = seq // tm
    return pl.pallas_call(
        _matmul_t_kernel,
        grid=(n // tn, t // tm),
        in_specs=[
            pl.BlockSpec((tm, d), lambda j, i: (i, 0)),
            pl.BlockSpec((tn, d), lambda j, i: (j, 0)),
        ],
        out_specs=pl.BlockSpec((1, tn, tm), lambda j, i: (i // tps, j, i % tps)),
        out_shape=jax.ShapeDtypeStruct((t // seq, n, seq), BF16),
        compiler_params=_params("parallel", "parallel"),
        name="in_proj_t",
    )(h2d, wt)


def _gate_out_kernel(o_ref, z_ref, x_ref, w_ref, g_ref, gn_ref, out_ref, *h_ref, chunks):
    cm = o_ref.shape[0] // chunks
    for c in range(chunks):
        rows = slice(c * cm, (c + 1) * cm)
        z = z_ref[rows, :].astype(F32)
        y = (o_ref[rows, :].astype(F32) * (z * _sigmoid(z))).astype(BF16)
        r = jnp.dot(y, w_ref[...], preferred_element_type=F32)
        xn = x_ref[rows, :] + _rms_normalize(r, g_ref[...])
        out_ref[rows, :] = xn
        if h_ref:
            h_ref[0][rows, :] = _rms_normalize(xn, gn_ref[...]).astype(BF16)


def _gate_out(o2d, zsrc, z_block, x2d, w, gain, next_gain, *, tm=512, chunks=4):
    t, e = o2d.shape
    d = w.shape[1]
    tm = min(tm, t)
    assert t % tm == 0 and tm % chunks == 0
    row_map = lambda i: (i, 0)
    const = lambda i: (0, 0)
    z_map = lambda i: (i, z_block)
    with_h = next_gain is not None
    gn = (next_gain if with_h else gain).reshape(1, d).astype(F32)
    out_specs = [pl.BlockSpec((tm, d), row_map)]
    out_shape = [jax.ShapeDtypeStruct((t, d), F32)]
    if with_h:
        out_specs.append(pl.BlockSpec((tm, d), row_map))
        out_shape.append(jax.ShapeDtypeStruct((t, d), BF16))
    res = pl.pallas_call(
        functools.partial(_gate_out_kernel, chunks=chunks),
        grid=(t // tm,),
        in_specs=[
            pl.BlockSpec((tm, e), row_map),
            pl.BlockSpec((tm, e), z_map),
            pl.BlockSpec((tm, d), row_map),
            pl.BlockSpec((e, d), const, pipeline_mode=pl.Buffered(1)),
            pl.BlockSpec((1, d), const),
            pl.BlockSpec((1, d), const),
        ],
        out_specs=out_specs,
        out_shape=out_shape,
        compiler_params=_params("parallel"),
        name="gate_out",
    )(o2d, zsrc, x2d, w, gain.reshape(1, d).astype(F32), gn)
    return (res[0], res[1]) if with_h else (res[0], None)


BIG = 1e30
SUM_ROWS = 16


def _softmax_step_t(s_ref, c, ok, vt, m_ref, acc_ref):
    m_old = m_ref[...]
    mt = jnp.max(s_ref[...], axis=0, keepdims=True) + c
    if ok is not None:
        mt = jnp.where(ok, mt, NEG_INF)
    m_new = jnp.maximum(m_old, mt)
    alpha = jnp.exp2(m_old - m_new)
    r = m_new - c
    if ok is not None:
        r = jnp.where(ok, r, BIG)
    p = jnp.exp2(s_ref[...] - r).astype(BF16)
    vt1 = jnp.concatenate([vt, jnp.ones((SUM_ROWS, vt.shape[1]), BF16)], axis=0)
    acc_ref[...] = alpha * acc_ref[...] + jnp.dot(vt1, p, preferred_element_type=F32)
    m_ref[...] = m_new


def _paired_causal_sweep(n_blocks, init, scores, finish, finalize):
    n_past = n_blocks - 1

    def pair(i, carry):
        blk_a, blk_b = i, n_blocks - 1 - i
        init()
        tiles = []
        for p in range(n_past):
            in_a = p < i
            tiles.append((jnp.where(in_a, blk_a, blk_b), jnp.where(in_a, p, p - i),
                          jnp.where(in_a, 0, 1), False))
        tiles += [(blk_a, blk_a, 0, True), (blk_b, blk_b, 1, True)]
        scores(tiles[0][0], tiles[0][1], 0)
        for p, (qb, j, slot, diagonal) in enumerate(tiles):
            if p + 1 < len(tiles):
                scores(tiles[p + 1][0], tiles[p + 1][1], (p + 1) % 2)
            finish(qb, j, slot, p % 2, diagonal)
        finalize(blk_a, 0)
        finalize(blk_b, 1)
        return carry

    lax.fori_loop(0, n_blocks // 2, pair, 0)


def _diff_attn_kernel(slopes_ref, lam_ref, q_ref, k_ref, vt_ref, g_ref, o_ref,
                      m_ref, acc_ref, bias_ref, qq_ref, s_ref, *, tq, hp, nqb, out_scale):
    hg = pl.program_id(1)
    lam = lam_ref[0]
    d = DIFF_HEAD_DIM
    w = 2 * d
    reps = 2 * tq // LANES

    lane = lax.broadcasted_iota(jnp.int32, (tq, w), 1)
    krow = lax.broadcasted_iota(jnp.int32, (tq, LANES), 0).astype(F32)
    slope2 = [slopes_ref[hg * hp + hh] * LOG2E for hh in range(hp)]
    for hh in range(hp):
        bias_ref[hh] = slope2[hh] * krow

    def prep(qb, carry):
        rows = pl.ds(pl.multiple_of(qb * tq, tq), tq)
        for hh in range(hp):
            q = q_ref[0, rows, hh * w:(hh + 1) * w].astype(F32) * (d ** -0.5 * LOG2E)
            qq_ref[qb, hh, :tq] = jnp.where(lane < d, q, 0.0).astype(BF16)
            qq_ref[qb, hh, tq:] = jnp.where(lane >= d, q, 0.0).astype(BF16)
        return carry

    lax.fori_loop(0, nqb, prep, 0)

    def init():
        m_ref[...] = jnp.full_like(m_ref, NEG_INF)
        acc_ref[...] = jnp.zeros_like(acc_ref)

    def scores(qb, j, buf):
        keys = pl.ds(pl.multiple_of(j * tq, tq), tq)
        for hh in range(hp):
            k = k_ref[0, keys, hh * w:(hh + 1) * w]
            s_ref[buf, hh] = (_dot_nt(k, qq_ref[qb, hh])
                              + jnp.concatenate([bias_ref[hh]] * reps, axis=1))

    def finish(qb, j, slot, buf, diagonal):
        keys = pl.ds(pl.multiple_of(j * tq, tq), tq)
        for hh in range(hp):
            vt = vt_ref[0, hh * w:(hh + 1) * w, keys]
            if diagonal:
                kr = lax.broadcasted_iota(jnp.int32, (tq, 2 * tq), 0)
                qc = lax.broadcasted_iota(jnp.int32, (tq, 2 * tq), 1)
                qc = jnp.where(qc >= tq, qc - tq, qc)
                s_ref[buf, hh] = jnp.where(qc >= kr, s_ref[buf, hh], NEG_INF)
            c = slope2[hh] * ((j - qb) * tq).astype(F32)
            _softmax_step_t(s_ref.at[buf, hh], c, None, vt, m_ref.at[slot, hh],
                            acc_ref.at[slot, hh])

    g = jnp.concatenate([g_ref[...]] * (tq // LANES), axis=1)

    def finalize(qb, slot):
        rows = pl.ds(pl.multiple_of(qb * tq, tq), tq)
        for hh in range(hp):
            a = acc_ref[slot, hh, :w] / acc_ref[slot, hh, w:w + 1]
            o = a[:, :tq] - lam * a[:, tq:]
            ms = jnp.mean(o * o, axis=0, keepdims=True)
            o = o * lax.rsqrt(ms + RMS_EPS) * g * out_scale
            o_ref[0, rows, hh * w:(hh + 1) * w] = o.T.astype(o_ref.dtype)

    _paired_causal_sweep(nqb, init, scores, finish, finalize)


def _diff_attention(proj, vt, slopes, lam, subln, out_scale, *, tq=256, hp=2):
    b, s, _ = proj.shape
    nh = DIFF_HEADS
    w = 2 * DIFF_HEAD_DIM
    tq = min(tq, s // 2)
    nqb = s // tq
    assert s % tq == 0 and tq % LANES == 0 and nh % hp == 0 and nqb % 2 == 0
    ng = nh // hp
    smem = pl.BlockSpec(memory_space=pltpu.SMEM)
    g = jnp.broadcast_to(subln.astype(F32)[:, None], (w, LANES))
    return pl.pallas_call(
        functools.partial(_diff_attn_kernel, tq=tq, hp=hp, nqb=nqb, out_scale=out_scale),
        grid=(b, ng),
        in_specs=[
            smem, smem,
            pl.BlockSpec((1, s, hp * w), lambda bi, h: (bi, 0, h)),
            pl.BlockSpec((1, s, hp * w), lambda bi, h: (bi, 0, ng + h)),
            pl.BlockSpec((1, hp * w, s), lambda bi, h: (bi, h, 0)),
            pl.BlockSpec((w, LANES), lambda bi, h: (0, 0)),
        ],
        out_specs=pl.BlockSpec((1, s, hp * w), lambda bi, h: (bi, 0, h)),
        out_shape=jax.ShapeDtypeStruct((b, s, MIX_WIDTH), BF16),
        scratch_shapes=[pltpu.VMEM((2, hp, 1, 2 * tq), F32),
                        pltpu.VMEM((2, hp, w + SUM_ROWS, 2 * tq), F32),
                        pltpu.VMEM((hp, tq, LANES), F32),
                        pltpu.VMEM((nqb, hp, 2 * tq, w), BF16),
                        pltpu.VMEM((2, hp, tq, 2 * tq), F32)],
        compiler_params=_params("parallel", "parallel"),
        name="diff_attn",
    )(slopes, lam, proj, proj, vt, g)


def _moba_attn_kernel(slopes_ref, q_ref, k_ref, vt_ref, o_ref, m_ref, acc_ref, bias_ref,
                      qs_ref, sel_ref, s_ref, *, nb, hp):
    hg = pl.program_id(1)
    blk = MOBA_BLOCK
    dh = MOBA_HEAD_DIM
    seq = nb * blk
    reps = blk // LANES

    krow = lax.broadcasted_iota(jnp.int32, (blk, LANES), 0).astype(F32)
    bidx = lax.broadcasted_iota(jnp.int32, (nb, seq), 0)
    qblk = lax.broadcasted_iota(jnp.int32, (nb, seq), 1) // blk
    slope2 = [slopes_ref[hg * hp + hh] * LOG2E for hh in range(hp)]
    for hh in range(hp):
        bias_ref[hh] = slope2[hh] * krow
        q = q_ref[0, :, hh * dh:(hh + 1) * dh]
        qs_ref[hh] = (q.astype(F32) * (dh ** -0.5 * LOG2E)).astype(BF16)

        kh = k_ref[0, :, hh * dh:(hh + 1) * dh].astype(F32)
        kmean = jnp.mean(kh.reshape(nb, blk, dh), axis=1)
        km_hi = kmean.astype(BF16)
        km_lo = (kmean - km_hi.astype(F32)).astype(BF16)
        gate = _dot_nt(km_hi, q) + _dot_nt(km_lo, q)

        rank = jnp.zeros((nb, seq), jnp.int32)
        for jp in range(nb):
            gj = gate[jp:jp + 1, :]
            before = (gj > gate) | ((gj == gate) & (jp < bidx))
            rank = rank + jnp.where(before, 1, 0) * jnp.where(jp < qblk, 1, 0)
        sel_ref[hh] = jnp.where((rank < MOBA_TOPK) & (bidx < qblk), 1.0, 0.0)

    def init():
        m_ref[...] = jnp.full_like(m_ref, NEG_INF)
        acc_ref[...] = jnp.zeros_like(acc_ref)

    def scores(qb, j, buf):
        keys = pl.ds(pl.multiple_of(j * blk, blk), blk)
        rows = pl.ds(pl.multiple_of(qb * blk, blk), blk)
        for hh in range(hp):
            k = k_ref[0, keys, hh * dh:(hh + 1) * dh]
            s_ref[buf, hh] = (_dot_nt(k, qs_ref[hh, rows, :])
                              + jnp.concatenate([bias_ref[hh]] * reps, axis=1))

    def finish(qb, j, slot, buf, diagonal):
        keys = pl.ds(pl.multiple_of(j * blk, blk), blk)
        cols = pl.ds(pl.multiple_of(qb * blk, blk), blk)
        for hh in range(hp):
            vt = vt_ref[0, hh * dh:(hh + 1) * dh, keys]
            c = slope2[hh] * ((j - qb) * blk).astype(F32)
            if diagonal:
                kr = lax.broadcasted_iota(jnp.int32, (blk, blk), 0)
                qc = lax.broadcasted_iota(jnp.int32, (blk, blk), 1)
                s_ref[buf, hh] = jnp.where(qc >= kr, s_ref[buf, hh], NEG_INF)
                ok = None
            else:
                tile_idx = lax.broadcasted_iota(jnp.int32, (nb, blk), 0)
                chosen = jnp.where(tile_idx == j, sel_ref[hh, :, cols], 0.0)
                ok = jnp.max(chosen, axis=0, keepdims=True) > 0.0
            _softmax_step_t(s_ref.at[buf, hh], c, ok, vt, m_ref.at[slot, hh],
                            acc_ref.at[slot, hh])

    def finalize(qb, slot):
        rows = pl.ds(pl.multiple_of(qb * blk, blk), blk)
        for hh in range(hp):
            o = acc_ref[slot, hh, :dh] / acc_ref[slot, hh, dh:dh + 1]
            o_ref[0, rows, hh * dh:(hh + 1) * dh] = o.T.astype(o_ref.dtype)

    _paired_causal_sweep(nb, init, scores, finish, finalize)


def _moba_attention(proj, vt, slopes, *, hp=4):
    b, s, _ = proj.shape
    nh = MOBA_HEADS
    dh = MOBA_HEAD_DIM
    blk = MOBA_BLOCK
    nb = s // blk
    assert s % blk == 0 and nh % hp == 0 and nb % 2 == 0
    ng = nh // hp
    smem = pl.BlockSpec(memory_space=pltpu.SMEM)
    return pl.pallas_call(
        functools.partial(_moba_attn_kernel, nb=nb, hp=hp),
        grid=(b, ng),
        in_specs=[
            smem,
            pl.BlockSpec((1, s, hp * dh), lambda bi, h: (bi, 0, h)),
            pl.BlockSpec((1, s, hp * dh), lambda bi, h: (bi, 0, ng + h)),
            pl.BlockSpec((1, hp * dh, s), lambda bi, h: (bi, h, 0)),
        ],
        out_specs=pl.BlockSpec((1, s, hp * dh), lambda bi, h: (bi, 0, h)),
        out_shape=jax.ShapeDtypeStruct((b, s, MIX_WIDTH), BF16),
        scratch_shapes=[pltpu.VMEM((2, hp, 1, blk), F32),
                        pltpu.VMEM((2, hp, dh + SUM_ROWS, blk), F32),
                        pltpu.VMEM((hp, blk, LANES), F32),
                        pltpu.VMEM((hp, s, dh), BF16),
                        pltpu.VMEM((hp, nb, s), F32),
                        pltpu.VMEM((2, hp, blk, blk), F32)],
        compiler_params=_params("parallel", "parallel"),
        name="moba_attn",
    )(slopes, proj, proj, vt)


def _swa_attn_kernel(slopes_ref, sinks_ref, q_ref, k_ref, v_ref, o_ref):
    i = pl.program_id(1)
    win = SWA_WINDOW
    dh = SWA_HEAD_DIM
    grp = SWA_Q_HEADS // SWA_KV_HEADS
    tq = win

    kstart = pl.multiple_of(jnp.maximum(i - 1, 0) * win, win)
    qpos = i * win + lax.broadcasted_iota(jnp.int32, (2 * tq, 1), 0) % tq
    kpos = kstart + lax.broadcasted_iota(jnp.int32, (1, 2 * win), 1)
    dist = qpos - kpos
    valid = (dist >= 0) & (dist < win)
    distf = dist.astype(F32)
    top = lax.broadcasted_iota(jnp.int32, (2 * tq, 1), 0) < tq
    lane = lax.broadcasted_iota(jnp.int32, (tq, 2 * dh), 1)

    for kvh in range(SWA_KV_HEADS):
        k = k_ref[0, pl.ds(kstart, 2 * win), kvh * 2 * dh:(kvh + 1) * 2 * dh]
        v = v_ref[0, pl.ds(kstart, 2 * win), kvh * 2 * dh:(kvh + 1) * 2 * dh]
        for pair in range(grp // 2):
            h0 = kvh * grp + 2 * pair
            q = q_ref[0, :, h0 * dh:(h0 + 2) * dh].astype(F32) * (dh ** -0.5)
            qq = jnp.concatenate([jnp.where(lane < dh, q, 0.0), jnp.where(lane >= dh, q, 0.0)],
                                 axis=0).astype(BF16)
            slope = jnp.where(top, slopes_ref[h0], slopes_ref[h0 + 1])
            sink = jnp.where(top, sinks_ref[h0], sinks_ref[h0 + 1])
            s = _dot_nt(qq, k) - slope * distf
            s = jnp.where(valid, s, NEG_INF)
            m = jnp.maximum(jnp.max(s, axis=-1, keepdims=True), sink)
            p = jnp.exp(s - m)
            denom = jnp.sum(p, axis=-1, keepdims=True) + jnp.exp(sink - m)
            o2 = jnp.dot(p.astype(BF16), v, preferred_element_type=F32) / denom
            o = jnp.where(lane < dh, o2[:tq], o2[tq:])
            o_ref[0, :, h0 * dh:(h0 + 2) * dh] = o.astype(o_ref.dtype)


def _swa_attention(proj, slopes, sinks):
    b, s, _ = proj.shape
    win = SWA_WINDOW
    e = MIX_WIDTH
    kvw = SWA_KV_HEADS * 2 * SWA_HEAD_DIM
    assert s % win == 0 and s >= 2 * win
    smem = pl.BlockSpec(memory_space=pltpu.SMEM)
    return pl.pallas_call(
        _swa_attn_kernel,
        grid=(b, s // win),
        in_specs=[
            smem, smem,
            pl.BlockSpec((1, win, e), lambda bi, i: (bi, i, 0)),
            pl.BlockSpec((1, s, kvw), lambda bi, i: (bi, 0, 2 * e // kvw)),
            pl.BlockSpec((1, s, kvw), lambda bi, i: (bi, 0, 2 * e // kvw + 1)),
        ],
        out_specs=pl.BlockSpec((1, win, e), lambda bi, i: (bi, i, 0)),
        out_shape=jax.ShapeDtypeStruct((b, s, e), BF16),
        compiler_params=_params("parallel", "arbitrary"),
        name="swa_attn",
    )(slopes, sinks, proj, proj, proj)


def _s5_ssm_kernel(u_ref, perm_ref, bb_ref, a_ref, cc_ref, d_ref, y_ref, st_ref, bu_ref, xb_ref,
                   *, bsz, tt, chunk_lanes):
    ns = S5_SLAB_STATE
    cw = chunk_lanes

    @pl.when(pl.program_id(1) == 0)
    def _():
        st_ref[...] = jnp.zeros_like(st_ref)

    tp = PERM_STEPS
    u32 = jnp.concatenate(
        [jnp.dot(perm_ref[0], u_ref[:, g * tp:(g + 1) * tp, :].reshape(bsz * tp, S5_SLAB),
                 preferred_element_type=F32) for g in range(tt // tp)], axis=0)
    u = u32.astype(BF16)
    y = d_ref[0] * u32
    for c in range(ns // cw):
        lo = c * cw
        bu_ref[2 * c] = jnp.dot(u, bb_ref[0, :, lo:lo + cw], preferred_element_type=F32)
        bu_ref[2 * c + 1] = jnp.dot(u, bb_ref[0, :, ns + lo:ns + lo + cw],
                                    preferred_element_type=F32)
        ar = jnp.broadcast_to(a_ref[0, 0:1, lo:lo + cw], (bsz, cw))
        ai = jnp.broadcast_to(a_ref[0, 1:2, lo:lo + cw], (bsz, cw))
        xr = st_ref[0, :, lo:lo + cw]
        xi = st_ref[1, :, lo:lo + cw]
        for t in range(tt):
            rows = slice(t * bsz, (t + 1) * bsz)
            xr, xi = (ar * xr - ai * xi + bu_ref[2 * c, rows, :],
                      ar * xi + ai * xr + bu_ref[2 * c + 1, rows, :])
            xb_ref[2 * c, rows, :] = xr.astype(BF16)
            xb_ref[2 * c + 1, rows, :] = xi.astype(BF16)
        st_ref[0, :, lo:lo + cw] = xr
        st_ref[1, :, lo:lo + cw] = xi
        y = (y + jnp.dot(xb_ref[2 * c], cc_ref[0, lo:lo + cw], preferred_element_type=F32)
             + jnp.dot(xb_ref[2 * c + 1], cc_ref[0, ns + lo:ns + lo + cw],
                       preferred_element_type=F32))
    y = y.astype(BF16)
    for g in range(tt // tp):
        y_bt = jnp.dot(perm_ref[1], y[g * bsz * tp:(g + 1) * bsz * tp],
                       preferred_element_type=F32)
        y_ref[:, g * tp:(g + 1) * tp, :] = y_bt.reshape(bsz, tp, S5_SLAB).astype(y_ref.dtype)


def _s5_ssm(proj, bd_b, a_bar, bd_c, d_skip, *, tt=32, chunk_lanes=512):
    b, s, _ = proj.shape
    tt = min(tt, s)
    assert s % tt == 0 and tt % PERM_STEPS == 0 and b % 16 == 0
    n_slab = MIX_WIDTH // S5_SLAB
    ns = S5_SLAB_STATE
    n_buf = 2 * ns // chunk_lanes
    rows = b * tt
    prow = jnp.arange(b * PERM_STEPS)
    fwd = jax.nn.one_hot((prow % b) * PERM_STEPS + prow // b, b * PERM_STEPS, dtype=BF16)
    perm = jnp.stack([fwd, fwd.T])
    return pl.pallas_call(
        functools.partial(_s5_ssm_kernel, bsz=b, tt=tt, chunk_lanes=chunk_lanes),
        grid=(n_slab, s // tt),
        in_specs=[
            pl.BlockSpec((b, tt, S5_SLAB), lambda sl, t: (0, t, sl)),
            pl.BlockSpec((2, b * PERM_STEPS, b * PERM_STEPS), lambda sl, t: (0, 0, 0)),
            pl.BlockSpec((1, S5_SLAB, 2 * ns), lambda sl, t: (sl, 0, 0)),
            pl.BlockSpec((1, 2, ns), lambda sl, t: (sl, 0, 0)),
            pl.BlockSpec((1, 2 * ns, S5_SLAB), lambda sl, t: (sl, 0, 0)),
            pl.BlockSpec((1, 1, S5_SLAB), lambda sl, t: (sl, 0, 0)),
        ],
        out_specs=pl.BlockSpec((b, tt, S5_SLAB), lambda sl, t: (0, t, sl)),
        out_shape=jax.ShapeDtypeStruct((b, s, MIX_WIDTH), BF16),
        scratch_shapes=[pltpu.VMEM((2, b, ns), F32),
                        pltpu.VMEM((n_buf, rows, chunk_lanes), F32),
                        pltpu.VMEM((n_buf, rows, chunk_lanes), BF16)],
        compiler_params=_params("parallel", "arbitrary"),
        name="s5_ssm",
    )(proj, perm, bd_b, a_bar, bd_c, d_skip)


def _s5_discretize(a_re, a_im, log_dt, b_re, b_im, c_re, c_im, d_skip):
    lr, li = a_re.astype(F32), a_im.astype(F32)
    dt = jnp.exp(log_dt.astype(F32))[:, None]
    mag = jnp.exp(lr * dt)
    ab_re, ab_im = mag * jnp.cos(li * dt), mag * jnp.sin(li * dt)
    den = lr * lr + li * li
    f_re = ((ab_re - 1.0) * lr + ab_im * li) / den
    f_im = (ab_im * lr - (ab_re - 1.0) * li) / den
    br, bi = b_re.astype(F32), b_im.astype(F32)
    bb_re = f_re[..., None] * br - f_im[..., None] * bi
    bb_im = f_re[..., None] * bi + f_im[..., None] * br
    n_slab = MIX_WIDTH // S5_SLAB
    gs = S5_SLAB // S5_GROUP
    eye = jnp.eye(gs, dtype=F32)

    def pack_b(m):
        m = m.reshape(n_slab, gs, S5_STATE, S5_GROUP)
        return jnp.einsum('sgpc,gh->sgchp', m, eye).reshape(n_slab, S5_SLAB, S5_SLAB_STATE)

    def pack_c(m):
        m = m.reshape(n_slab, gs, S5_GROUP, S5_STATE)
        return jnp.einsum('sgcp,gh->shpgc', m, eye).reshape(n_slab, S5_SLAB_STATE, S5_SLAB)

    bd_b = jnp.concatenate([pack_b(bb_re), pack_b(bb_im)], axis=2).astype(BF16)
    bd_c = jnp.concatenate([pack_c(c_re.astype(F32)), pack_c(-c_im.astype(F32))], axis=1).astype(BF16)
    a_bar = jnp.stack([ab_re.reshape(n_slab, S5_SLAB_STATE), ab_im.reshape(n_slab, S5_SLAB_STATE)],
                      axis=1)
    return bd_b, a_bar, bd_c, d_skip.astype(F32).reshape(n_slab, 1, S5_SLAB)


def _glu_kernel(y_ref, w_ref, b_ref, o_ref, *, chunks):
    cm = y_ref.shape[0] // chunks
    for c in range(chunks):
        rows = slice(c * cm, (c + 1) * cm)
        g = jax.nn.gelu(y_ref[rows, :].astype(F32))
        r = jnp.dot(g.astype(BF16), w_ref[...], preferred_element_type=F32) + b_ref[...]
        o_ref[rows, :] = (g * _sigmoid(r)).astype(o_ref.dtype)


def _glu(y2d, w, bias, *, tm=512, chunks=4):
    t, e = y2d.shape
    tm = min(tm, t)
    assert t % tm == 0 and tm % chunks == 0
    return pl.pallas_call(
        functools.partial(_glu_kernel, chunks=chunks),
        grid=(t // tm,),
        in_specs=[
            pl.BlockSpec((tm, e), lambda i: (i, 0)),
            pl.BlockSpec((e, e), lambda i: (0, 0), pipeline_mode=pl.Buffered(1)),
            pl.BlockSpec((1, e), lambda i: (0, 0)),
        ],
        out_specs=pl.BlockSpec((tm, e), lambda i: (i, 0)),
        out_shape=jax.ShapeDtypeStruct((t, e), BF16),
        compiler_params=_params("parallel"),
        name="s5_glu",
    )(y2d, w, bias.reshape(1, e).astype(F32))


def _s5_layer(x2d, h2d, bsz, seq, post_g, next_g, w_in, a_re, a_im, log_dt, b_re, b_im, c_re,
              c_im, d_skip, w_glu, b_glu, w_out):
    e = MIX_WIDTH
    proj = _in_proj(h2d, w_in.astype(BF16))
    bd_b, a_bar, bd_c, d3 = _s5_discretize(a_re, a_im, log_dt, b_re, b_im, c_re, c_im, d_skip)
    y = _s5_ssm(proj.reshape(bsz, seq, 2 * e), bd_b, a_bar, bd_c, d3)
    y = _glu(y.reshape(bsz * seq, e), w_glu.astype(BF16), b_glu)
    return _gate_out(y, proj, 1, x2d, w_out.astype(BF16), post_g, next_g)


def _attn_projections(h2d, seq, w_in):
    e = MIX_WIDTH
    proj = _in_proj(h2d, w_in.astype(BF16), skip_block=2)
    vt = _in_proj_t(h2d, w_in[:, 2 * e:3 * e].T.astype(BF16), seq)
    return proj, vt


def _diff_layer(x2d, h2d, bsz, seq, layer_idx, post_g, next_g, w_in, lq1, lk1, lq2, lk2, subln,
                w_out):
    e = MIX_WIDTH
    proj, vt = _attn_projections(h2d, seq, w_in)
    lam_init = 0.8 - 0.6 * math.exp(-0.3 * layer_idx)
    lam = (jnp.exp(jnp.sum(lq1.astype(F32) * lk1.astype(F32)))
           - jnp.exp(jnp.sum(lq2.astype(F32) * lk2.astype(F32))) + lam_init).reshape(1)
    o = _diff_attention(proj.reshape(bsz, seq, 3 * e), vt, _alibi_slopes(DIFF_HEADS), lam, subln,
                        1.0 - lam_init)
    return _gate_out(o.reshape(bsz * seq, e), proj, 2, x2d, w_out.astype(BF16), post_g, next_g)


def _moba_layer(x2d, h2d, bsz, seq, post_g, next_g, w_in, w_out):
    e = MIX_WIDTH
    proj, vt = _attn_projections(h2d, seq, w_in)
    o = _moba_attention(proj.reshape(bsz, seq, 3 * e), vt, _alibi_slopes(MOBA_HEADS))
    return _gate_out(o.reshape(bsz * seq, e), proj, 2, x2d, w_out.astype(BF16), post_g, next_g)


def _swa_layer(x2d, h2d, bsz, seq, post_g, next_g, w_in, sinks, w_out):
    e = MIX_WIDTH
    dh = SWA_HEAD_DIM
    nqc, nkc = SWA_Q_HEADS * dh, SWA_KV_HEADS * dh
    wq, wk = w_in[:, :nqc], w_in[:, nqc:nqc + nkc]
    wv, wz = w_in[:, nqc + nkc:nqc + 2 * nkc], w_in[:, nqc + 2 * nkc:]

    def dup(w):
        w = w.reshape(-1, SWA_KV_HEADS, 1, dh)
        return jnp.broadcast_to(w, (w.shape[0], SWA_KV_HEADS, 2, dh)).reshape(-1, 2 * nkc)

    w_cat = jnp.concatenate([wq, wz, dup(wk), dup(wv)], axis=1).astype(BF16)
    proj = _in_proj(h2d, w_cat)
    o = _swa_attention(proj.reshape(bsz, seq, w_cat.shape[1]), _alibi_slopes(SWA_Q_HEADS),
                       sinks.astype(F32))
    return _gate_out(o.reshape(bsz * seq, e), proj, 1, x2d, w_out.astype(BF16), post_g, next_g)


def kernel(x, pre_norm, post_norm, s5_w_in, s5_a_re, s5_a_im, s5_log_dt, s5_b_re, s5_b_im, s5_c_re, s5_c_im, s5_d, s5_w_glu, s5_b_glu, s5_w_out, diff_w_in, diff_lq1, diff_lk1, diff_lq2, diff_lk2, diff_subln, diff_w_out, moba_w_in, moba_w_out, swa_w_in, swa_sinks, swa_w_out):
    bsz, seq, d = x.shape
    depth = pre_norm.shape[0]
    x2d = x.reshape(bsz * seq, d)
    h2d = _prenorm(x2d, pre_norm[0])
    for i in range(depth):
        kind, j = i % 4, i // 4
        post_g = post_norm[i]
        next_g = pre_norm[i + 1] if i + 1 < depth else None
        if kind == 0:
            x2d, h2d = _s5_layer(x2d, h2d, bsz, seq, post_g, next_g, s5_w_in[j], s5_a_re[j],
                                 s5_a_im[j], s5_log_dt[j], s5_b_re[j], s5_b_im[j], s5_c_re[j],
                                 s5_c_im[j], s5_d[j], s5_w_glu[j], s5_b_glu[j], s5_w_out[j])
        elif kind == 1:
            x2d, h2d = _diff_layer(x2d, h2d, bsz, seq, i, post_g, next_g, diff_w_in[j],
                                   diff_lq1[j], diff_lk1[j], diff_lq2[j], diff_lk2[j],
                                   diff_subln[j], diff_w_out[j])
        elif kind == 2:
            x2d, h2d = _moba_layer(x2d, h2d, bsz, seq, post_g, next_g, moba_w_in[j],
                                   moba_w_out[j])
        else:
            x2d, h2d = _swa_layer(x2d, h2d, bsz, seq, post_g, next_g, swa_w_in[j], swa_sinks[j],
                                  swa_w_out[j])
    return x2d.reshape(bsz, seq, d)
```

```python
import functools
import math

import jax
import jax.numpy as jnp
from jax import lax
from jax.experimental import pallas as pl
from jax.experimental.pallas import tpu as pltpu

F32 = jnp.float32
BF16 = jnp.bfloat16

D_MODEL = 2048
MIX_WIDTH = D_MODEL
RMS_EPS = 1e-6
NEG_INF = -1e30
LOG2E = math.log2(math.e)
LANES = 128

S5_GROUP = 16
S5_GROUPS = MIX_WIDTH // S5_GROUP
S5_STATE = 64
S5_SLAB = 256
S5_SLAB_STATE = (S5_SLAB // S5_GROUP) * S5_STATE
PERM_STEPS = 16

DIFF_HEADS = 16
DIFF_HEAD_DIM = 64
MOBA_HEADS = 16
MOBA_HEAD_DIM = 128
MOBA_BLOCK = 256
MOBA_TOPK = 3
SWA_Q_HEADS = 32
SWA_KV_HEADS = 4
SWA_HEAD_DIM = 64
SWA_WINDOW = 128

VMEM_LIMIT = 56 * 1024 * 1024


def _params(*sem):
    return pltpu.CompilerParams(dimension_semantics=sem, vmem_limit_bytes=VMEM_LIMIT)


def _alibi_slopes(n_heads):
    return 2.0 ** (-8.0 * jnp.arange(1, n_heads + 1, dtype=F32) / n_heads)


def _dot_nt(a, b):
    return lax.dot_general(a, b, (((1,), (1,)), ((), ())), preferred_element_type=F32)


def _sigmoid(x):
    return 0.5 * jnp.tanh(0.5 * x) + 0.5


def _rms_normalize(x, gain):
    ms = jnp.mean(x * x, axis=-1, keepdims=True)
    return x * lax.rsqrt(ms + RMS_EPS) * gain


def _prenorm_kernel(x_ref, g_ref, h_ref):
    h_ref[...] = _rms_normalize(x_ref[...], g_ref[...]).astype(BF16)


def _prenorm(x2d, gain, *, tm=512):
    t, d = x2d.shape
    tm = min(tm, t)
    assert t % tm == 0
    return pl.pallas_call(
        _prenorm_kernel,
        grid=(t // tm,),
        in_specs=[pl.BlockSpec((tm, d), lambda i: (i, 0)), pl.BlockSpec((1, d), lambda i: (0, 0))],
        out_specs=pl.BlockSpec((tm, d), lambda i: (i, 0)),
        out_shape=jax.ShapeDtypeStruct((t, d), BF16),
        compiler_params=_params("parallel"),
        name="prenorm",
    )(x2d, gain.reshape(1, d).astype(F32))


def _col_tile(n, cap=2048):
    return max(t for t in range(256, cap + 1, 256) if n % t == 0)


def _matmul_kernel(h_ref, w_ref, o_ref):
    o_ref[...] = jnp.dot(h_ref[...], w_ref[...], preferred_element_type=F32).astype(o_ref.dtype)


def _in_proj(h2d, w, *, skip_block=None, tm=1024):
    t, d = h2d.shape
    tm = min(tm, t)
    tn = _col_tile(w.shape[1])
    assert t % tm == 0
    if skip_block is None:
        n = w.shape[1]
        w_map = lambda j, i: (0, j)
    else:
        n = w.shape[1] - tn
        w_map = lambda j, i: (0, j + (j >= skip_block).astype(jnp.int32))
    return pl.pallas_call(
        _matmul_kernel,
        grid=(n // tn, t // tm),
        in_specs=[
            pl.BlockSpec((tm, d), lambda j, i: (i, 0)),
            pl.BlockSpec((d, tn), w_map),
        ],
        out_specs=pl.BlockSpec((tm, tn), lambda j, i: (i, j)),
        out_shape=jax.ShapeDtypeStruct((t, n), BF16),
        compiler_params=_params("parallel", "parallel"),
        name="in_proj",
    )(h2d, w)


def _matmul_t_kernel(h_ref, wt_ref, o_ref):
    o_ref[0] = _dot_nt(wt_ref[...], h_ref[...]).astype(o_ref.dtype)


def _in_proj_t(h2d, wt, seq, *, tm=1024):
    t, d = h2d.shape
    n = wt.shape[0]
    tm = min(tm, seq)
    tn = _col_tile(n)
    assert seq % tm == 0
    tps = seq // tm
    return pl.pallas_call(
        _matmul_t_kernel,
        grid=(n // tn, t // tm),
        in_specs=[
            pl.BlockSpec((tm, d), lambda j, i: (i, 0)),
            pl.BlockSpec((tn, d), lambda j, i: (j, 0)),
        ],
        out_specs=pl.BlockSpec((1, tn, tm), lambda j, i: (i // tps, j, i % tps)),
        out_shape=jax.ShapeDtypeStruct((t // seq, n, seq), BF16),
        compiler_params=_params("parallel", "parallel"),
        name="in_proj_t",
    )(h2d, wt)


def _gate_out_kernel(o_ref, z_ref, x_ref, w_ref, g_ref, gn_ref, out_ref, *h_ref, chunks):
    cm = o_ref.shape[0] // chunks
    for c in range(chunks):
        rows = slice(c * cm, (c + 1) * cm)
        z = z_ref[rows, :].astype(F32)
        y = (o_ref[rows, :].astype(F32) * (z * _sigmoid(z))).astype(BF16)
        r = jnp.dot(y, w_ref[...], preferred_element_type=F32)
        xn = x_ref[rows, :] + _rms_normalize(r, g_ref[...])
        out_ref[rows, :] = xn
        if h_ref:
            h_ref[0][rows, :] = _rms_normalize(xn, gn_ref[...]).astype(BF16)


def _gate_out(o2d, zsrc, z_block, x2d, w, gain, next_gain, *, tm=512, chunks=4):
    t, e = o2d.shape
    d = w.shape[1]
    tm = min(tm, t)
    assert t % tm == 0 and tm % chunks == 0
    row_map = lambda i: (i, 0)
    const = lambda i: (0, 0)
    z_map = lambda i: (i, z_block)
    with_h = next_gain is not None
    gn = (next_gain if with_h else gain).reshape(1, d).astype(F32)
    out_specs = [pl.BlockSpec((tm, d), row_map)]
    out_shape = [jax.ShapeDtypeStruct((t, d), F32)]
    if with_h:
        out_specs.append(pl.BlockSpec((tm, d), row_map))
        out_shape.append(jax.ShapeDtypeStruct((t, d), BF16))
    res = pl.pallas_call(
        functools.partial(_gate_out_kernel, chunks=chunks),
        grid=(t // tm,),
        in_specs=[
            pl.BlockSpec((tm, e), row_map),
            pl.BlockSpec((tm, e), z_map),
            pl.BlockSpec((tm, d), row_map),
            pl.BlockSpec((e, d), const, pipeline_mode=pl.Buffered(1)),
            pl.BlockSpec((1, d), const),
            pl.BlockSpec((1, d), const),
        ],
        out_specs=out_specs,
        out_shape=out_shape,
        compiler_params=_params("parallel"),
        name="gate_out",
    )(o2d, zsrc, x2d, w, gain.reshape(1, d).astype(F32), gn)
    return (res[0], res[1]) if with_h else (res[0], None)


BIG = 1e30
SUM_ROWS = 16


def _softmax_step_t(s_ref, c, ok, vt, m_ref, acc_ref):
    m_old = m_ref[...]
    mt = jnp.max(s_ref[...], axis=0, keepdims=True) + c
    if ok is not None:
        mt = jnp.where(ok, mt, NEG_INF)
    m_new = jnp.maximum(m_old, mt)
    alpha = jnp.exp2(m_old - m_new)
    r = m_new - c
    if ok is not None:
        r = jnp.where(ok, r, BIG)
    p = jnp.exp2(s_ref[...] - r).astype(BF16)
    vt1 = jnp.concatenate([vt, jnp.ones((SUM_ROWS, vt.shape[1]), BF16)], axis=0)
    acc_ref[...] = alpha * acc_ref[...] + jnp.dot(vt1, p, preferred_element_type=F32)
    m_ref[...] = m_new


def _paired_causal_sweep(n_blocks, init, scores, finish, finalize):
    n_past = n_blocks - 1

    def pair(i, carry):
        blk_a, blk_b = i, n_blocks - 1 - i
        init()

        def past(p):
            in_a = p < i
            return jnp.where(in_a, blk_a, blk_b), jnp.where(in_a, p, p - i), jnp.where(in_a, 0, 1)

        scores(*past(0)[:2], 0)

        def two_tiles(k, carry2):
            qb0, j0, slot0 = past(2 * k)
            qb1, j1, slot1 = past(2 * k + 1)
            scores(qb1, j1, 1)
            finish(qb0, j0, slot0, 0, False)
            scores(*past(2 * k + 2)[:2], 0)
            finish(qb1, j1, slot1, 1, False)
            return carry2

        lax.fori_loop(0, n_past // 2, two_tiles, 0)
        scores(blk_a, blk_a, 1)
        finish(*past(n_past - 1), 0, False)
        scores(blk_b, blk_b, 0)
        finish(blk_a, blk_a, 0, 1, True)
        finish(blk_b, blk_b, 1, 0, True)
        finalize(blk_a, 0)
        finalize(blk_b, 1)
        return carry

    lax.fori_loop(0, n_blocks // 2, pair, 0)


def _diff_attn_kernel(slopes_ref, lam_ref, q_ref, k_ref, vt_ref, g_ref, o_ref,
                      m_ref, acc_ref, bias_ref, qq_ref, s_ref, *, tq, hp, nqb, out_scale):
    hg = pl.program_id(1)
    lam = lam_ref[0]
    d = DIFF_HEAD_DIM
    w = 2 * d
    reps = 2 * tq // LANES

    lane = lax.broadcasted_iota(jnp.int32, (tq, w), 1)
    krow = lax.broadcasted_iota(jnp.int32, (tq, LANES), 0).astype(F32)
    slope2 = [slopes_ref[hg * hp + hh] * LOG2E for hh in range(hp)]
    for hh in range(hp):
        bias_ref[hh] = slope2[hh] * krow

    def prep(qb, carry):
        rows = pl.ds(pl.multiple_of(qb * tq, tq), tq)
        for hh in range(hp):
            q = q_ref[0, rows, hh * w:(hh + 1) * w].astype(F32) * (d ** -0.5 * LOG2E)
            qq_ref[qb, hh, :tq] = jnp.where(lane < d, q, 0.0).astype(BF16)
            qq_ref[qb, hh, tq:] = jnp.where(lane >= d, q, 0.0).astype(BF16)
        return carry

    lax.fori_loop(0, nqb, prep, 0)

    def init():
        m_ref[...] = jnp.full_like(m_ref, NEG_INF)
        acc_ref[...] = jnp.zeros_like(acc_ref)

    def scores(qb, j, buf):
        keys = pl.ds(pl.multiple_of(j * tq, tq), tq)
        for hh in range(hp):
            k = k_ref[0, keys, hh * w:(hh + 1) * w]
            s_ref[buf, hh] = (_dot_nt(k, qq_ref[qb, hh])
                              + jnp.concatenate([bias_ref[hh]] * reps, axis=1))

    def finish(qb, j, slot, buf, diagonal):
        keys = pl.ds(pl.multiple_of(j * tq, tq), tq)
        for hh in range(hp):
            vt = vt_ref[0, hh * w:(hh + 1) * w, keys]
            if diagonal:
                kr = lax.broadcasted_iota(jnp.int32, (tq, 2 * tq), 0)
                qc = lax.broadcasted_iota(jnp.int32, (tq, 2 * tq), 1)
                qc = jnp.where(qc >= tq, qc - tq, qc)
                s_ref[buf, hh] = jnp.where(qc >= kr, s_ref[buf, hh], NEG_INF)
            c = slope2[hh] * ((j - qb) * tq).astype(F32)
            _softmax_step_t(s_ref.at[buf, hh], c, None, vt, m_ref.at[slot, hh],
                            acc_ref.at[slot, hh])

    g = jnp.concatenate([g_ref[...]] * (tq // LANES), axis=1)

    def finalize(qb, slot):
        rows = pl.ds(pl.multiple_of(qb * tq, tq), tq)
        for hh in range(hp):
            a = acc_ref[slot, hh, :w] / acc_ref[slot, hh, w:w + 1]
            o = a[:, :tq] - lam * a[:, tq:]
            ms = jnp.mean(o * o, axis=0, keepdims=True)
            o = o * lax.rsqrt(ms + RMS_EPS) * g * out_scale
            o_ref[0, rows, hh * w:(hh + 1) * w] = o.T.astype(o_ref.dtype)

    _paired_causal_sweep(nqb, init, scores, finish, finalize)


def _diff_attention(proj, vt, slopes, lam, subln, out_scale, *, tq=256, hp=2):
    b, s, _ = proj.shape
    nh = DIFF_HEADS
    w = 2 * DIFF_HEAD_DIM
    tq = min(tq, s // 2)
    nqb = s // tq
    assert s % tq == 0 and tq % LANES == 0 and nh % hp == 0 and nqb % 2 == 0
    ng = nh // hp
    smem = pl.BlockSpec(memory_space=pltpu.SMEM)
    g = jnp.broadcast_to(subln.astype(F32)[:, None], (w, LANES))
    return pl.pallas_call(
        functools.partial(_diff_attn_kernel, tq=tq, hp=hp, nqb=nqb, out_scale=out_scale),
        grid=(b, ng),
        in_specs=[
            smem, smem,
            pl.BlockSpec((1, s, hp * w), lambda bi, h: (bi, 0, h)),
            pl.BlockSpec((1, s, hp * w), lambda bi, h: (bi, 0, ng + h)),
            pl.BlockSpec((1, hp * w, s), lambda bi, h: (bi, h, 0)),
            pl.BlockSpec((w, LANES), lambda bi, h: (0, 0)),
        ],
        out_specs=pl.BlockSpec((1, s, hp * w), lambda bi, h: (bi, 0, h)),
        out_shape=jax.ShapeDtypeStruct((b, s, MIX_WIDTH), BF16),
        scratch_shapes=[pltpu.VMEM((2, hp, 1, 2 * tq), F32),
                        pltpu.VMEM((2, hp, w + SUM_ROWS, 2 * tq), F32),
                        pltpu.VMEM((hp, tq, LANES), F32),
                        pltpu.VMEM((nqb, hp, 2 * tq, w), BF16),
                        pltpu.VMEM((2, hp, tq, 2 * tq), F32)],
        compiler_params=_params("parallel", "parallel"),
        name="diff_attn",
    )(slopes, lam, proj, proj, vt, g)


def _moba_attn_kernel(slopes_ref, q_ref, k_ref, vt_ref, o_ref, m_ref, acc_ref, bias_ref,
                      qs_ref, sel_ref, s_ref, *, nb, hp):
    hg = pl.program_id(1)
    blk = MOBA_BLOCK
    dh = MOBA_HEAD_DIM
    seq = nb * blk
    reps = blk // LANES

    krow = lax.broadcasted_iota(jnp.int32, (blk, LANES), 0).astype(F32)
    bidx = lax.broadcasted_iota(jnp.int32, (nb, seq), 0)
    qblk = lax.broadcasted_iota(jnp.int32, (nb, seq), 1) // blk
    slope2 = [slopes_ref[hg * hp + hh] * LOG2E for hh in range(hp)]
    for hh in range(hp):
        bias_ref[hh] = slope2[hh] * krow
        q = q_ref[0, :, hh * dh:(hh + 1) * dh]
        qs_ref[hh] = (q.astype(F32) * (dh ** -0.5 * LOG2E)).astype(BF16)

        kh = k_ref[0, :, hh * dh:(hh + 1) * dh].astype(F32)
        kmean = jnp.mean(kh.reshape(nb, blk, dh), axis=1)
        km_hi = kmean.astype(BF16)
        km_lo = (kmean - km_hi.astype(F32)).astype(BF16)
        gate = _dot_nt(km_hi, q) + _dot_nt(km_lo, q)

        rank = jnp.zeros((nb, seq), jnp.int32)
        for jp in range(nb):
            gj = gate[jp:jp + 1, :]
            before = (gj > gate) | ((gj == gate) & (jp < bidx))
            rank = rank + jnp.where(before, 1, 0) * jnp.where(jp < qblk, 1, 0)
        sel_ref[hh] = jnp.where((rank < MOBA_TOPK) & (bidx < qblk), 1.0, 0.0)

    def init():
        m_ref[...] = jnp.full_like(m_ref, NEG_INF)
        acc_ref[...] = jnp.zeros_like(acc_ref)

    def scores(qb, j, buf):
        keys = pl.ds(pl.multiple_of(j * blk, blk), blk)
        rows = pl.ds(pl.multiple_of(qb * blk, blk), blk)
        for hh in range(hp):
            k = k_ref[0, keys, hh * dh:(hh + 1) * dh]
            s_ref[buf, hh] = (_dot_nt(k, qs_ref[hh, rows, :])
                              + jnp.concatenate([bias_ref[hh]] * reps, axis=1))

    def finish(qb, j, slot, buf, diagonal):
        keys = pl.ds(pl.multiple_of(j * blk, blk), blk)
        cols = pl.ds(pl.multiple_of(qb * blk, blk), blk)
        for hh in range(hp):
            vt = vt_ref[0, hh * dh:(hh + 1) * dh, keys]
            c = slope2[hh] * ((j - qb) * blk).astype(F32)
            if diagonal:
                kr = lax.broadcasted_iota(jnp.int32, (blk, blk), 0)
                qc = lax.broadcasted_iota(jnp.int32, (blk, blk), 1)
                s_ref[buf, hh] = jnp.where(qc >= kr, s_ref[buf, hh], NEG_INF)
                ok = None
            else:
                tile_idx = lax.broadcasted_iota(jnp.int32, (nb, blk), 0)
                chosen = jnp.where(tile_idx == j, sel_ref[hh, :, cols], 0.0)
                ok = jnp.max(chosen, axis=0, keepdims=True) > 0.0
            _softmax_step_t(s_ref.at[buf, hh], c, ok, vt, m_ref.at[slot, hh],
                            acc_ref.at[slot, hh])

    def finalize(qb, slot):
        rows = pl.ds(pl.multiple_of(qb * blk, blk), blk)
        for hh in range(hp):
            o = acc_ref[slot, hh, :dh] / acc_ref[slot, hh, dh:dh + 1]
            o_ref[0, rows, hh * dh:(hh + 1) * dh] = o.T.astype(o_ref.dtype)

    _paired_causal_sweep(nb, init, scores, finish, finalize)


def _moba_attention(proj, vt, slopes, *, hp=4):
    b, s, _ = proj.shape
    nh = MOBA_HEADS
    dh = MOBA_HEAD_DIM
    blk = MOBA_BLOCK
    nb = s // blk
    assert s % blk == 0 and nh % hp == 0 and nb % 2 == 0
    ng = nh // hp
    smem = pl.BlockSpec(memory_space=pltpu.SMEM)
    return pl.pallas_call(
        functools.partial(_moba_attn_kernel, nb=nb, hp=hp),
        grid=(b, ng),
        in_specs=[
            smem,
            pl.BlockSpec((1, s, hp * dh), lambda bi, h: (bi, 0, h)),
            pl.BlockSpec((1, s, hp * dh), lambda bi, h: (bi, 0, ng + h)),
            pl.BlockSpec((1, hp * dh, s), lambda bi, h: (bi, h, 0)),
        ],
        out_specs=pl.BlockSpec((1, s, hp * dh), lambda bi, h: (bi, 0, h)),
        out_shape=jax.ShapeDtypeStruct((b, s, MIX_WIDTH), BF16),
        scratch_shapes=[pltpu.VMEM((2, hp, 1, blk), F32),
                        pltpu.VMEM((2, hp, dh + SUM_ROWS, blk), F32),
                        pltpu.VMEM((hp, blk, LANES), F32),
                        pltpu.VMEM((hp, s, dh), BF16),
                        pltpu.VMEM((hp, nb, s), F32),
                        pltpu.VMEM((2, hp, blk, blk), F32)],
        compiler_params=_params("parallel", "parallel"),
        name="moba_attn",
    )(slopes, proj, proj, vt)


def _swa_attn_kernel(slopes_ref, sinks_ref, q_ref, k_ref, v_ref, o_ref):
    i = pl.program_id(1)
    win = SWA_WINDOW
    dh = SWA_HEAD_DIM
    grp = SWA_Q_HEADS // SWA_KV_HEADS
    tq = win

    kstart = pl.multiple_of(jnp.maximum(i - 1, 0) * win, win)
    qpos = i * win + lax.broadcasted_iota(jnp.int32, (2 * tq, 1), 0) % tq
    kpos = kstart + lax.broadcasted_iota(jnp.int32, (1, 2 * win), 1)
    dist = qpos - kpos
    valid = (dist >= 0) & (dist < win)
    distf = dist.astype(F32)
    top = lax.broadcasted_iota(jnp.int32, (2 * tq, 1), 0) < tq
    lane = lax.broadcasted_iota(jnp.int32, (tq, 2 * dh), 1)

    for kvh in range(SWA_KV_HEADS):
        k = k_ref[0, pl.ds(kstart, 2 * win), kvh * 2 * dh:(kvh + 1) * 2 * dh]
        v = v_ref[0, pl.ds(kstart, 2 * win), kvh * 2 * dh:(kvh + 1) * 2 * dh]
        for pair in range(grp // 2):
            h0 = kvh * grp + 2 * pair
            q = q_ref[0, :, h0 * dh:(h0 + 2) * dh].astype(F32) * (dh ** -0.5)
            qq = jnp.concatenate([jnp.where(lane < dh, q, 0.0), jnp.where(lane >= dh, q, 0.0)],
                                 axis=0).astype(BF16)
            slope = jnp.where(top, slopes_ref[h0], slopes_ref[h0 + 1])
            sink = jnp.where(top, sinks_ref[h0], sinks_ref[h0 + 1])
            s = _dot_nt(qq, k) - slope * distf
            s = jnp.where(valid, s, NEG_INF)
            m = jnp.maximum(jnp.max(s, axis=-1, keepdims=True), sink)
            p = jnp.exp(s - m)
            denom = jnp.sum(p, axis=-1, keepdims=True) + jnp.exp(sink - m)
            o2 = jnp.dot(p.astype(BF16), v, preferred_element_type=F32) / denom
            o = jnp.where(lane < dh, o2[:tq], o2[tq:])
            o_ref[0, :, h0 * dh:(h0 + 2) * dh] = o.astype(o_ref.dtype)


def _swa_attention(proj, slopes, sinks):
    b, s, _ = proj.shape
    win = SWA_WINDOW
    e = MIX_WIDTH
    kvw = SWA_KV_HEADS * 2 * SWA_HEAD_DIM
    assert s % win == 0 and s >= 2 * win
    smem = pl.BlockSpec(memory_space=pltpu.SMEM)
    return pl.pallas_call(
        _swa_attn_kernel,
        grid=(b, s // win),
        in_specs=[
            smem, smem,
            pl.BlockSpec((1, win, e), lambda bi, i: (bi, i, 0)),
            pl.BlockSpec((1, s, kvw), lambda bi, i: (bi, 0, 2 * e // kvw)),
            pl.BlockSpec((1, s, kvw), lambda bi, i: (bi, 0, 2 * e // kvw + 1)),
        ],
        out_specs=pl.BlockSpec((1, win, e), lambda bi, i: (bi, i, 0)),
        out_shape=jax.ShapeDtypeStruct((b, s, e), BF16),
        compiler_params=_params("parallel", "arbitrary"),
        name="swa_attn",
    )(slopes, sinks, proj, proj, proj)


def _s5_ssm_kernel(u_ref, perm_ref, bb_ref, a_ref, cc_ref, d_ref, y_ref, st_ref, bu_ref, xb_ref,
                   *, bsz, tt, chunk_lanes):
    ns = S5_SLAB_STATE
    cw = chunk_lanes

    @pl.when(pl.program_id(1) == 0)
    def _():
        st_ref[...] = jnp.zeros_like(st_ref)

    tp = PERM_STEPS
    u32 = jnp.concatenate(
        [jnp.dot(perm_ref[0], u_ref[:, g * tp:(g + 1) * tp, :].reshape(bsz * tp, S5_SLAB),
                 preferred_element_type=F32) for g in range(tt // tp)], axis=0)
    u = u32.astype(BF16)
    y = d_ref[0] * u32
    for c in range(ns // cw):
        lo = c * cw
        bu_ref[2 * c] = jnp.dot(u, bb_ref[0, :, lo:lo + cw], preferred_element_type=F32)
        bu_ref[2 * c + 1] = jnp.dot(u, bb_ref[0, :, ns + lo:ns + lo + cw],
                                    preferred_element_type=F32)
        ar = jnp.broadcast_to(a_ref[0, 0:1, lo:lo + cw], (bsz, cw))
        ai = jnp.broadcast_to(a_ref[0, 1:2, lo:lo + cw], (bsz, cw))
        xr = st_ref[0, :, lo:lo + cw]
        xi = st_ref[1, :, lo:lo + cw]
        for t in range(tt):
            rows = slice(t * bsz, (t + 1) * bsz)
            xr, xi = (ar * xr - ai * xi + bu_ref[2 * c, rows, :],
                      ar * xi + ai * xr + bu_ref[2 * c + 1, rows, :])
            xb_ref[2 * c, rows, :] = xr.astype(BF16)
            xb_ref[2 * c + 1, rows, :] = xi.astype(BF16)
        st_ref[0, :, lo:lo + cw] = xr
        st_ref[1, :, lo:lo + cw] = xi
        y = (y + jnp.dot(xb_ref[2 * c], cc_ref[0, lo:lo + cw], preferred_element_type=F32)
             + jnp.dot(xb_ref[2 * c + 1], cc_ref[0, ns + lo:ns + lo + cw],
                       preferred_element_type=F32))
    y = y.astype(BF16)
    for g in range(tt // tp):
        y_bt = jnp.dot(perm_ref[1], y[g * bsz * tp:(g + 1) * bsz * tp],
                       preferred_element_type=F32)
        y_ref[:, g * tp:(g + 1) * tp, :] = y_bt.reshape(bsz, tp, S5_SLAB).astype(y_ref.dtype)


def _s5_ssm(proj, bd_b, a_bar, bd_c, d_skip, *, tt=32, chunk_lanes=512):
    b, s, _ = proj.shape
    tt = min(tt, s)
    assert s % tt == 0 and tt % PERM_STEPS == 0 and b % 16 == 0
    n_slab = MIX_WIDTH // S5_SLAB
    ns = S5_SLAB_STATE
    n_buf = 2 * ns // chunk_lanes
    rows = b * tt
    prow = jnp.arange(b * PERM_STEPS)
    fwd = jax.nn.one_hot((prow % b) * PERM_STEPS + prow // b, b * PERM_STEPS, dtype=BF16)
    perm = jnp.stack([fwd, fwd.T])
    return pl.pallas_call(
        functools.partial(_s5_ssm_kernel, bsz=b, tt=tt, chunk_lanes=chunk_lanes),
        grid=(n_slab, s // tt),
        in_specs=[
            pl.BlockSpec((b, tt, S5_SLAB), lambda sl, t: (0, t, sl)),
            pl.BlockSpec((2, b * PERM_STEPS, b * PERM_STEPS), lambda sl, t: (0, 0, 0)),
            pl.BlockSpec((1, S5_SLAB, 2 * ns), lambda sl, t: (sl, 0, 0)),
            pl.BlockSpec((1, 2, ns), lambda sl, t: (sl, 0, 0)),
            pl.BlockSpec((1, 2 * ns, S5_SLAB), lambda sl, t: (sl, 0, 0)),
            pl.BlockSpec((1, 1, S5_SLAB), lambda sl, t: (sl, 0, 0)),
        ],
        out_specs=pl.BlockSpec((b, tt, S5_SLAB), lambda sl, t: (0, t, sl)),
        out_shape=jax.ShapeDtypeStruct((b, s, MIX_WIDTH), BF16),
        scratch_shapes=[pltpu.VMEM((2, b, ns), F32),
                        pltpu.VMEM((n_buf, rows, chunk_lanes), F32),
                        pltpu.VMEM((n_buf, rows, chunk_lanes), BF16)],
        compiler_params=_params("parallel", "arbitrary"),
        name="s5_ssm",
    )(proj, perm, bd_b, a_bar, bd_c, d_skip)


def _s5_discretize(a_re, a_im, log_dt, b_re, b_im, c_re, c_im, d_skip):
    lr, li = a_re.astype(F32), a_im.astype(F32)
    dt = jnp.exp(log_dt.astype(F32))[:, None]
    mag = jnp.exp(lr * dt)
    ab_re, ab_im = mag * jnp.cos(li * dt), mag * jnp.sin(li * dt)
    den = lr * lr + li * li
    f_re = ((ab_re - 1.0) * lr + ab_im * li) / den
    f_im = (ab_im * lr - (ab_re - 1.0) * li) / den
    br, bi = b_re.astype(F32), b_im.astype(F32)
    bb_re = f_re[..., None] * br - f_im[..., None] * bi
    bb_im = f_re[..., None] * bi + f_im[..., None] * br
    n_slab = MIX_WIDTH // S5_SLAB
    gs = S5_SLAB // S5_GROUP
    eye = jnp.eye(gs, dtype=F32)

    def pack_b(m):
        m = m.reshape(n_slab, gs, S5_STATE, S5_GROUP)
        return jnp.einsum('sgpc,gh->sgchp', m, eye).reshape(n_slab, S5_SLAB, S5_SLAB_STATE)

    def pack_c(m):
        m = m.reshape(n_slab, gs, S5_GROUP, S5_STATE)
        return jnp.einsum('sgcp,gh->shpgc', m, eye).reshape(n_slab, S5_SLAB_STATE, S5_SLAB)

    bd_b = jnp.concatenate([pack_b(bb_re), pack_b(bb_im)], axis=2).astype(BF16)
    bd_c = jnp.concatenate([pack_c(c_re.astype(F32)), pack_c(-c_im.astype(F32))], axis=1).astype(BF16)
    a_bar = jnp.stack([ab_re.reshape(n_slab, S5_SLAB_STATE), ab_im.reshape(n_slab, S5_SLAB_STATE)],
                      axis=1)
    return bd_b, a_bar, bd_c, d_skip.astype(F32).reshape(n_slab, 1, S5_SLAB)


def _glu_kernel(y_ref, w_ref, b_ref, o_ref, *, chunks):
    cm = y_ref.shape[0] // chunks
    for c in range(chunks):
        rows = slice(c * cm, (c + 1) * cm)
        g = jax.nn.gelu(y_ref[rows, :].astype(F32))
        r = jnp.dot(g.astype(BF16), w_ref[...], preferred_element_type=F32) + b_ref[...]
        o_ref[rows, :] = (g * _sigmoid(r)).astype(o_ref.dtype)


def _glu(y2d, w, bias, *, tm=512, chunks=4):
    t, e = y2d.shape
    tm = min(tm, t)
    assert t % tm == 0 and tm % chunks == 0
    return pl.pallas_call(
        functools.partial(_glu_kernel, chunks=chunks),
        grid=(t // tm,),
        in_specs=[
            pl.BlockSpec((tm, e), lambda i: (i, 0)),
            pl.BlockSpec((e, e), lambda i: (0, 0), pipeline_mode=pl.Buffered(1)),
            pl.BlockSpec((1, e), lambda i: (0, 0)),
        ],
        out_specs=pl.BlockSpec((tm, e), lambda i: (i, 0)),
        out_shape=jax.ShapeDtypeStruct((t, e), BF16),
        compiler_params=_params("parallel"),
        name="s5_glu",
    )(y2d, w, bias.reshape(1, e).astype(F32))


def _s5_layer(x2d, h2d, bsz, seq, post_g, next_g, w_in, a_re, a_im, log_dt, b_re, b_im, c_re,
              c_im, d_skip, w_glu, b_glu, w_out):
    e = MIX_WIDTH
    proj = _in_proj(h2d, w_in.astype(BF16))
    bd_b, a_bar, bd_c, d3 = _s5_discretize(a_re, a_im, log_dt, b_re, b_im, c_re, c_im, d_skip)
    y = _s5_ssm(proj.reshape(bsz, seq, 2 * e), bd_b, a_bar, bd_c, d3)
    y = _glu(y.reshape(bsz * seq, e), w_glu.astype(BF16), b_glu)
    return _gate_out(y, proj, 1, x2d, w_out.astype(BF16), post_g, next_g)


def _attn_projections(h2d, seq, w_in):
    e = MIX_WIDTH
    proj = _in_proj(h2d, w_in.astype(BF16), skip_block=2)
    vt = _in_proj_t(h2d, w_in[:, 2 * e:3 * e].T.astype(BF16), seq)
    return proj, vt


def _diff_layer(x2d, h2d, bsz, seq, layer_idx, post_g, next_g, w_in, lq1, lk1, lq2, lk2, subln,
                w_out):
    e = MIX_WIDTH
    proj, vt = _attn_projections(h2d, seq, w_in)
    lam_init = 0.8 - 0.6 * math.exp(-0.3 * layer_idx)
    lam = (jnp.exp(jnp.sum(lq1.astype(F32) * lk1.astype(F32)))
           - jnp.exp(jnp.sum(lq2.astype(F32) * lk2.astype(F32))) + lam_init).reshape(1)
    o = _diff_attention(proj.reshape(bsz, seq, 3 * e), vt, _alibi_slopes(DIFF_HEADS), lam, subln,
                        1.0 - lam_init)
    return _gate_out(o.reshape(bsz * seq, e), proj, 2, x2d, w_out.astype(BF16), post_g, next_g)


def _moba_layer(x2d, h2d, bsz, seq, post_g, next_g, w_in, w_out):
    e = MIX_WIDTH
    proj, vt = _attn_projections(h2d, seq, w_in)
    o = _moba_attention(proj.reshape(bsz, seq, 3 * e), vt, _alibi_slopes(MOBA_HEADS))
    return _gate_out(o.reshape(bsz * seq, e), proj, 2, x2d, w_out.astype(BF16), post_g, next_g)


def _swa_layer(x2d, h2d, bsz, seq, post_g, next_g, w_in, sinks, w_out):
    e = MIX_WIDTH
    dh = SWA_HEAD_DIM
    nqc, nkc = SWA_Q_HEADS * dh, SWA_KV_HEADS * dh
    wq, wk = w_in[:, :nqc], w_in[:, nqc:nqc + nkc]
    wv, wz = w_in[:, nqc + nkc:nqc + 2 * nkc], w_in[:, nqc + 2 * nkc:]

    def dup(w):
        w = w.reshape(-1, SWA_KV_HEADS, 1, dh)
        return jnp.broadcast_to(w, (w.shape[0], SWA_KV_HEADS, 2, dh)).reshape(-1, 2 * nkc)

    w_cat = jnp.concatenate([wq, wz, dup(wk), dup(wv)], axis=1).astype(BF16)
    proj = _in_proj(h2d, w_cat)
    o = _swa_attention(proj.reshape(bsz, seq, w_cat.shape[1]), _alibi_slopes(SWA_Q_HEADS),
                       sinks.astype(F32))
    return _gate_out(o.reshape(bsz * seq, e), proj, 1, x2d, w_out.astype(BF16), post_g, next_g)


def kernel(x, pre_norm, post_norm, s5_w_in, s5_a_re, s5_a_im, s5_log_dt, s5_b_re, s5_b_im, s5_c_re, s5_c_im, s5_d, s5_w_glu, s5_b_glu, s5_w_out, diff_w_in, diff_lq1, diff_lk1, diff_lq2, diff_lk2, diff_subln, diff_w_out, moba_w_in, moba_w_out, swa_w_in, swa_sinks, swa_w_out):
    bsz, seq, d = x.shape
    depth = pre_norm.shape[0]
    x2d = x.reshape(bsz * seq, d)
    h2d = _prenorm(x2d, pre_norm[0])
    for i in range(depth):
        kind, j = i % 4, i // 4
        post_g = post_norm[i]
        next_g = pre_norm[i + 1] if i + 1 < depth else None
        if kind == 0:
            x2d, h2d = _s5_layer(x2d, h2d, bsz, seq, post_g, next_g, s5_w_in[j], s5_a_re[j],
                                 s5_a_im[j], s5_log_dt[j], s5_b_re[j], s5_b_im[j], s5_c_re[j],
                                 s5_c_im[j], s5_d[j], s5_w_glu[j], s5_b_glu[j], s5_w_out[j])
        elif kind == 1:
            x2d, h2d = _diff_layer(x2d, h2d, bsz, seq, i, post_g, next_g, diff_w_in[j],
                                   diff_lq1[j], diff_lk1[j], diff_lq2[j], diff_lk2[j],
                                   diff_subln[j], diff_w_out[j])
        elif kind == 2:
            x2d, h2d = _moba_layer(x2d, h2d, bsz, seq, post_g, next_g, moba_w_in[j],
                                   moba_w_out[j])
        else:
            x2d, h2d = _swa_layer(x2d, h2d, bsz, seq, post_g, next_g, swa_w_in[j], swa_sinks[j],
                                  swa_w_out[j])
    return x2d.reshape(bsz, seq, d)
```

```python
import functools
import math

import jax
import jax.numpy as jnp
from jax import lax
from jax.experimental import pallas as pl
from jax.experimental.pallas import tpu as pltpu

F32 = jnp.float32
BF16 = jnp.bfloat16

D_MODEL = 2048
MIX_WIDTH = D_MODEL
RMS_EPS = 1e-6
NEG_INF = -1e30
LOG2E = math.log2(math.e)
LANES = 128

S5_GROUP = 16
S5_GROUPS = MIX_WIDTH // S5_GROUP
S5_STATE = 64
S5_SLAB = 256
S5_SLAB_STATE = (S5_SLAB // S5_GROUP) * S5_STATE
PERM_STEPS = 16

DIFF_HEADS = 16
DIFF_HEAD_DIM = 64
MOBA_HEADS = 16
MOBA_HEAD_DIM = 128
MOBA_BLOCK = 256
MOBA_TOPK = 3
SWA_Q_HEADS = 32
SWA_KV_HEADS = 4
SWA_HEAD_DIM = 64
SWA_WINDOW = 128

VMEM_LIMIT = 56 * 1024 * 1024


def _params(*sem):
    return pltpu.CompilerParams(dimension_semantics=sem, vmem_limit_bytes=VMEM_LIMIT)


def _alibi_slopes(n_heads):
    return 2.0 ** (-8.0 * jnp.arange(1, n_heads + 1, dtype=F32) / n_heads)


def _dot_nt(a, b):
    return lax.dot_general(a, b, (((1,), (1,)), ((), ())), preferred_element_type=F32)


def _sigmoid(x):
    return 0.5 * jnp.tanh(0.5 * x) + 0.5


def _rms_normalize(x, gain):
    ms = jnp.mean(x * x, axis=-1, keepdims=True)
    return x * lax.rsqrt(ms + RMS_EPS) * gain


def _prenorm_kernel(x_ref, g_ref, h_ref):
    h_ref[...] = _rms_normalize(x_ref[...], g_ref[...]).astype(BF16)


def _prenorm(x2d, gain, *, tm=512):
    t, d = x2d.shape
    tm = min(tm, t)
    assert t % tm == 0
    return pl.pallas_call(
        _prenorm_kernel,
        grid=(t // tm,),
        in_specs=[pl.BlockSpec((tm, d), lambda i: (i, 0)), pl.BlockSpec((1, d), lambda i: (0, 0))],
        out_specs=pl.BlockSpec((tm, d), lambda i: (i, 0)),
        out_shape=jax.ShapeDtypeStruct((t, d), BF16),
        compiler_params=_params("parallel"),
        name="prenorm",
    )(x2d, gain.reshape(1, d).astype(F32))


def _col_tile(n, cap=2048):
    return max(t for t in range(256, cap + 1, 256) if n % t == 0)


def _matmul_kernel(h_ref, w_ref, o_ref):
    o_ref[...] = jnp.dot(h_ref[...], w_ref[...], preferred_element_type=F32).astype(o_ref.dtype)


def _in_proj(h2d, w, *, skip_block=None, tm=1024):
    t, d = h2d.shape
    tm = min(tm, t)
    tn = _col_tile(w.shape[1])
    assert t % tm == 0
    if skip_block is None:
        n = w.shape[1]
        w_map = lambda j, i: (0, j)
    else:
        n = w.shape[1] - tn
        w_map = lambda j, i: (0, j + (j >= skip_block).astype(jnp.int32))
    return pl.pallas_call(
        _matmul_kernel,
        grid=(n // tn, t // tm),
        in_specs=[
            pl.BlockSpec((tm, d), lambda j, i: (i, 0)),
            pl.BlockSpec((d, tn), w_map),
        ],
        out_specs=pl.BlockSpec((tm, tn), lambda j, i: (i, j)),
        out_shape=jax.ShapeDtypeStruct((t, n), BF16),
        compiler_params=_params("parallel", "parallel"),
        name="in_proj",
    )(h2d, w)


def _matmul_t_kernel(h_ref, w_ref, o_ref, wt_ref):
    @pl.when(pl.program_id(1) == 0)
    def _():
        wt_ref[...] = w_ref[...].T

    o_ref[0] = _dot_nt(wt_ref[...], h_ref[...]).astype(o_ref.dtype)


def _in_proj_t(h2d, w, col_block, seq, *, tm=1024):
    t, d = h2d.shape
    tm = min(tm, seq)
    tn = _col_tile(w.shape[1])
    assert seq % tm == 0
    tps = seq // tm
    return pl.pallas_call(
        _matmul_t_kernel,
        grid=(1, t // tm),
        in_specs=[
            pl.BlockSpec((tm, d), lambda j, i: (i, 0)),
            pl.BlockSpec((d, tn), lambda j, i: (0, col_block)),
        ],
        out_specs=pl.BlockSpec((1, tn, tm), lambda j, i: (i // tps, 0, i % tps)),
        out_shape=jax.ShapeDtypeStruct((t // seq, tn, seq), BF16),
        scratch_shapes=[pltpu.VMEM((tn, d), BF16)],
        compiler_params=_params("parallel", "arbitrary"),
        name="in_proj_t",
    )(h2d, w)


def _gate_out_kernel(o_ref, z_ref, x_ref, w_ref, g_ref, gn_ref, out_ref, *h_ref, chunks):
    cm = o_ref.shape[0] // chunks
    for c in range(chunks):
        rows = slice(c * cm, (c + 1) * cm)
        hz = z_ref[rows, :] * 0.5
        y = o_ref[rows, :] * (hz * jnp.tanh(hz) + hz)
        r = jnp.dot(y, w_ref[...], preferred_element_type=F32)
        xn = x_ref[rows, :] + _rms_normalize(r, g_ref[...])
        out_ref[rows, :] = xn
        if h_ref:
            h_ref[0][rows, :] = _rms_normalize(xn, gn_ref[...]).astype(BF16)


def _gate_out(o2d, zsrc, z_block, x2d, w, gain, next_gain, *, tm=512, chunks=4):
    t, e = o2d.shape
    d = w.shape[1]
    tm = min(tm, t)
    assert t % tm == 0 and tm % chunks == 0
    row_map = lambda i: (i, 0)
    const = lambda i: (0, 0)
    z_map = lambda i: (i, z_block)
    with_h = next_gain is not None
    gn = (next_gain if with_h else gain).reshape(1, d).astype(F32)
    out_specs = [pl.BlockSpec((tm, d), row_map)]
    out_shape = [jax.ShapeDtypeStruct((t, d), F32)]
    if with_h:
        out_specs.append(pl.BlockSpec((tm, d), row_map))
        out_shape.append(jax.ShapeDtypeStruct((t, d), BF16))
    res = pl.pallas_call(
        functools.partial(_gate_out_kernel, chunks=chunks),
        grid=(t // tm,),
        in_specs=[
            pl.BlockSpec((tm, e), row_map),
            pl.BlockSpec((tm, e), z_map),
            pl.BlockSpec((tm, d), row_map),
            pl.BlockSpec((e, d), const, pipeline_mode=pl.Buffered(1)),
            pl.BlockSpec((1, d), const),
            pl.BlockSpec((1, d), const),
        ],
        out_specs=out_specs,
        out_shape=out_shape,
        compiler_params=_params("parallel"),
        name="gate_out",
    )(o2d, zsrc, x2d, w, gain.reshape(1, d).astype(F32), gn)
    return (res[0], res[1]) if with_h else (res[0], None)


BIG = 1e30
SUM_ROWS = 16


def _softmax_step_t(s_ref, c, ok, vt, m_ref, acc_ref):
    m_old = m_ref[...]
    mt = jnp.max(s_ref[...], axis=0, keepdims=True) + c
    if ok is not None:
        mt = jnp.where(ok, mt, NEG_INF)
    m_new = jnp.maximum(m_old, mt)
    alpha = jnp.exp2(m_old - m_new)
    r = m_new - c
    if ok is not None:
        r = jnp.where(ok, r, BIG)
    p = jnp.exp2(s_ref[...] - r).astype(BF16)
    vt1 = jnp.concatenate([vt, jnp.ones((SUM_ROWS, vt.shape[1]), BF16)], axis=0)
    acc_ref[...] = alpha * acc_ref[...] + jnp.dot(vt1, p, preferred_element_type=F32)
    m_ref[...] = m_new


def _paired_causal_sweep(n_blocks, init, scores, finish, finalize):
    n_past = n_blocks - 1

    def pair(i, carry):
        blk_a, blk_b = i, n_blocks - 1 - i
        init()
        tiles = []
        for p in range(n_past):
            in_a = p < i
            tiles.append((jnp.where(in_a, blk_a, blk_b), jnp.where(in_a, p, p - i),
                          jnp.where(in_a, 0, 1), False))
        tiles += [(blk_a, blk_a, 0, True), (blk_b, blk_b, 1, True)]
        scores(tiles[0][0], tiles[0][1], 0)
        for p, (qb, j, slot, diagonal) in enumerate(tiles):
            if p + 1 < len(tiles):
                scores(tiles[p + 1][0], tiles[p + 1][1], (p + 1) % 2)
            finish(qb, j, slot, p % 2, diagonal)
        finalize(blk_a, 0)
        finalize(blk_b, 1)
        return carry

    lax.fori_loop(0, n_blocks // 2, pair, 0)


def _diff_attn_kernel(slopes_ref, lam_ref, q_ref, k_ref, vt_ref, g_ref, o_ref,
                      m_ref, acc_ref, bias_ref, qq_ref, s_ref, *, tq, hp, nqb, out_scale):
    hg = pl.program_id(1)
    lam = lam_ref[0]
    d = DIFF_HEAD_DIM
    w = 2 * d
    reps = 2 * tq // LANES

    lane = lax.broadcasted_iota(jnp.int32, (tq, w), 1)
    krow = lax.broadcasted_iota(jnp.int32, (tq, LANES), 0).astype(F32)
    slope2 = [slopes_ref[hg * hp + hh] * LOG2E for hh in range(hp)]
    for hh in range(hp):
        bias_ref[hh] = slope2[hh] * krow

    def prep(qb, carry):
        rows = pl.ds(pl.multiple_of(qb * tq, tq), tq)
        for hh in range(hp):
            q = q_ref[0, rows, hh * w:(hh + 1) * w].astype(F32) * (d ** -0.5 * LOG2E)
            qq_ref[qb, hh, :tq] = jnp.where(lane < d, q, 0.0).astype(BF16)
            qq_ref[qb, hh, tq:] = jnp.where(lane >= d, q, 0.0).astype(BF16)
        return carry

    lax.fori_loop(0, nqb, prep, 0)

    def init():
        m_ref[...] = jnp.full_like(m_ref, NEG_INF)
        acc_ref[...] = jnp.zeros_like(acc_ref)

    def scores(qb, j, buf):
        keys = pl.ds(pl.multiple_of(j * tq, tq), tq)
        for hh in range(hp):
            k = k_ref[0, keys, hh * w:(hh + 1) * w]
            s_ref[buf, hh] = (_dot_nt(k, qq_ref[qb, hh])
                              + jnp.concatenate([bias_ref[hh]] * reps, axis=1))

    def finish(qb, j, slot, buf, diagonal):
        keys = pl.ds(pl.multiple_of(j * tq, tq), tq)
        for hh in range(hp):
            vt = vt_ref[0, hh * w:(hh + 1) * w, keys]
            if diagonal:
                kr = lax.broadcasted_iota(jnp.int32, (tq, 2 * tq), 0)
                qc = lax.broadcasted_iota(jnp.int32, (tq, 2 * tq), 1)
                qc = jnp.where(qc >= tq, qc - tq, qc)
                s_ref[buf, hh] = jnp.where(qc >= kr, s_ref[buf, hh], NEG_INF)
            c = slope2[hh] * ((j - qb) * tq).astype(F32)
            _softmax_step_t(s_ref.at[buf, hh], c, None, vt, m_ref.at[slot, hh],
                            acc_ref.at[slot, hh])

    g = jnp.concatenate([g_ref[...]] * (tq // LANES), axis=1)

    def finalize(qb, slot):
        rows = pl.ds(pl.multiple_of(qb * tq, tq), tq)
        for hh in range(hp):
            a = acc_ref[slot, hh, :w] / acc_ref[slot, hh, w:w + 1]
            o = a[:, :tq] - lam * a[:, tq:]
            ms = jnp.mean(o * o, axis=0, keepdims=True)
            o = o * lax.rsqrt(ms + RMS_EPS) * g * out_scale
            o_ref[0, rows, hh * w:(hh + 1) * w] = o.T.astype(o_ref.dtype)

    _paired_causal_sweep(nqb, init, scores, finish, finalize)


def _diff_attention(proj, vt, slopes, lam, subln, out_scale, *, tq=256, hp=4):
    b, s, _ = proj.shape
    nh = DIFF_HEADS
    w = 2 * DIFF_HEAD_DIM
    tq = min(tq, s // 2)
    nqb = s // tq
    assert s % tq == 0 and tq % LANES == 0 and nh % hp == 0 and nqb % 2 == 0
    ng = nh // hp
    smem = pl.BlockSpec(memory_space=pltpu.SMEM)
    g = jnp.broadcast_to(subln.astype(F32)[:, None], (w, LANES))
    return pl.pallas_call(
        functools.partial(_diff_attn_kernel, tq=tq, hp=hp, nqb=nqb, out_scale=out_scale),
        grid=(b, ng),
        in_specs=[
            smem, smem,
            pl.BlockSpec((1, s, hp * w), lambda bi, h: (bi, 0, h)),
            pl.BlockSpec((1, s, hp * w), lambda bi, h: (bi, 0, ng + h)),
            pl.BlockSpec((1, hp * w, s), lambda bi, h: (bi, h, 0)),
            pl.BlockSpec((w, LANES), lambda bi, h: (0, 0)),
        ],
        out_specs=pl.BlockSpec((1, s, hp * w), lambda bi, h: (bi, 0, h)),
        out_shape=jax.ShapeDtypeStruct((b, s, MIX_WIDTH), BF16),
        scratch_shapes=[pltpu.VMEM((2, hp, 1, 2 * tq), F32),
                        pltpu.VMEM((2, hp, w + SUM_ROWS, 2 * tq), F32),
                        pltpu.VMEM((hp, tq, LANES), F32),
                        pltpu.VMEM((nqb, hp, 2 * tq, w), BF16),
                        pltpu.VMEM((2, hp, tq, 2 * tq), F32)],
        compiler_params=_params("parallel", "parallel"),
        name="diff_attn",
    )(slopes, lam, proj, proj, vt, g)


def _moba_attn_kernel(slopes_ref, q_ref, k_ref, vt_ref, o_ref, m_ref, acc_ref, bias_ref,
                      qs_ref, sel_ref, s_ref, *, nb, hp):
    hg = pl.program_id(1)
    blk = MOBA_BLOCK
    dh = MOBA_HEAD_DIM
    seq = nb * blk
    reps = blk // LANES

    krow = lax.broadcasted_iota(jnp.int32, (blk, LANES), 0).astype(F32)
    bidx = lax.broadcasted_iota(jnp.int32, (nb, seq), 0)
    qblk = lax.broadcasted_iota(jnp.int32, (nb, seq), 1) // blk
    slope2 = [slopes_ref[hg * hp + hh] * LOG2E for hh in range(hp)]
    for hh in range(hp):
        bias_ref[hh] = slope2[hh] * krow
        q = q_ref[0, :, hh * dh:(hh + 1) * dh]
        qs_ref[hh] = (q.astype(F32) * (dh ** -0.5 * LOG2E)).astype(BF16)

        kh = k_ref[0, :, hh * dh:(hh + 1) * dh].astype(F32)
        kmean = jnp.mean(kh.reshape(nb, blk, dh), axis=1)
        km_hi = kmean.astype(BF16)
        km_lo = (kmean - km_hi.astype(F32)).astype(BF16)
        gate = _dot_nt(km_hi, q) + _dot_nt(km_lo, q)

        rank = jnp.zeros((nb, seq), jnp.int32)
        for jp in range(nb):
            gj = gate[jp:jp + 1, :]
            before = (gj > gate) | ((gj == gate) & (jp < bidx))
            rank = rank + jnp.where(before, 1, 0) * jnp.where(jp < qblk, 1, 0)
        sel_ref[hh] = jnp.where((rank < MOBA_TOPK) & (bidx < qblk), 1.0, 0.0)

    def init():
        m_ref[...] = jnp.full_like(m_ref, NEG_INF)
        acc_ref[...] = jnp.zeros_like(acc_ref)

    def scores(qb, j, buf):
        keys = pl.ds(pl.multiple_of(j * blk, blk), blk)
        rows = pl.ds(pl.multiple_of(qb * blk, blk), blk)
        for hh in range(hp):
            k = k_ref[0, keys, hh * dh:(hh + 1) * dh]
            s_ref[buf, hh] = (_dot_nt(k, qs_ref[hh, rows, :])
                              + jnp.concatenate([bias_ref[hh]] * reps, axis=1))

    def finish(qb, j, slot, buf, diagonal):
        keys = pl.ds(pl.multiple_of(j * blk, blk), blk)
        cols = pl.ds(pl.multiple_of(qb * blk, blk), blk)
        for hh in range(hp):
            vt = vt_ref[0, hh * dh:(hh + 1) * dh, keys]
            c = slope2[hh] * ((j - qb) * blk).astype(F32)
            if diagonal:
                kr = lax.broadcasted_iota(jnp.int32, (blk, blk), 0)
                qc = lax.broadcasted_iota(jnp.int32, (blk, blk), 1)
                s_ref[buf, hh] = jnp.where(qc >= kr, s_ref[buf, hh], NEG_INF)
                ok = None
            else:
                tile_idx = lax.broadcasted_iota(jnp.int32, (nb, blk), 0)
                chosen = jnp.where(tile_idx == j, sel_ref[hh, :, cols], 0.0)
                ok = jnp.max(chosen, axis=0, keepdims=True) > 0.0
            _softmax_step_t(s_ref.at[buf, hh], c, ok, vt, m_ref.at[slot, hh],
                            acc_ref.at[slot, hh])

    def finalize(qb, slot):
        rows = pl.ds(pl.multiple_of(qb * blk, blk), blk)
        for hh in range(hp):
            o = acc_ref[slot, hh, :dh] / acc_ref[slot, hh, dh:dh + 1]
            o_ref[0, rows, hh * dh:(hh + 1) * dh] = o.T.astype(o_ref.dtype)

    _paired_causal_sweep(nb, init, scores, finish, finalize)


def _moba_attention(proj, vt, slopes, *, hp=4):
    b, s, _ = proj.shape
    nh = MOBA_HEADS
    dh = MOBA_HEAD_DIM
    blk = MOBA_BLOCK
    nb = s // blk
    assert s % blk == 0 and nh % hp == 0 and nb % 2 == 0
    ng = nh // hp
    smem = pl.BlockSpec(memory_space=pltpu.SMEM)
    return pl.pallas_call(
        functools.partial(_moba_attn_kernel, nb=nb, hp=hp),
        grid=(b, ng),
        in_specs=[
            smem,
            pl.BlockSpec((1, s, hp * dh), lambda bi, h: (bi, 0, h)),
            pl.BlockSpec((1, s, hp * dh), lambda bi, h: (bi, 0, ng + h)),
            pl.BlockSpec((1, hp * dh, s), lambda bi, h: (bi, h, 0)),
        ],
        out_specs=pl.BlockSpec((1, s, hp * dh), lambda bi, h: (bi, 0, h)),
        out_shape=jax.ShapeDtypeStruct((b, s, MIX_WIDTH), BF16),
        scratch_shapes=[pltpu.VMEM((2, hp, 1, blk), F32),
                        pltpu.VMEM((2, hp, dh + SUM_ROWS, blk), F32),
                        pltpu.VMEM((hp, blk, LANES), F32),
                        pltpu.VMEM((hp, s, dh), BF16),
                        pltpu.VMEM((hp, nb, s), F32),
                        pltpu.VMEM((2, hp, blk, blk), F32)],
        compiler_params=_params("parallel", "parallel"),
        name="moba_attn",
    )(slopes, proj, proj, vt)


def _swa_attn_kernel(slopes_ref, sinks_ref, q_ref, k_ref, v_ref, o_ref):
    i = pl.program_id(1)
    win = SWA_WINDOW
    dh = SWA_HEAD_DIM
    grp = SWA_Q_HEADS // SWA_KV_HEADS
    tq = win

    kstart = pl.multiple_of(jnp.maximum(i - 1, 0) * win, win)
    qpos = i * win + lax.broadcasted_iota(jnp.int32, (2 * tq, 1), 0) % tq
    kpos = kstart + lax.broadcasted_iota(jnp.int32, (1, 2 * win), 1)
    dist = qpos - kpos
    valid = (dist >= 0) & (dist < win)
    distf = dist.astype(F32)
    top = lax.broadcasted_iota(jnp.int32, (2 * tq, 1), 0) < tq
    lane = lax.broadcasted_iota(jnp.int32, (tq, 2 * dh), 1)

    for kvh in range(SWA_KV_HEADS):
        k = k_ref[0, pl.ds(kstart, 2 * win), kvh * 2 * dh:(kvh + 1) * 2 * dh]
        v = v_ref[0, pl.ds(kstart, 2 * win), kvh * 2 * dh:(kvh + 1) * 2 * dh]
        for pair in range(grp // 2):
            h0 = kvh * grp + 2 * pair
            q = q_ref[0, :, h0 * dh:(h0 + 2) * dh].astype(F32) * (dh ** -0.5)
            qq = jnp.concatenate([jnp.where(lane < dh, q, 0.0), jnp.where(lane >= dh, q, 0.0)],
                                 axis=0).astype(BF16)
            slope = jnp.where(top, slopes_ref[h0], slopes_ref[h0 + 1])
            sink = jnp.where(top, sinks_ref[h0], sinks_ref[h0 + 1])
            s = _dot_nt(qq, k) - slope * distf
            s = jnp.where(valid, s, NEG_INF)
            m = jnp.maximum(jnp.max(s, axis=-1, keepdims=True), sink)
            p = jnp.exp(s - m)
            denom = jnp.sum(p, axis=-1, keepdims=True) + jnp.exp(sink - m)
            o2 = jnp.dot(p.astype(BF16), v, preferred_element_type=F32) / denom
            o = jnp.where(lane < dh, o2[:tq], o2[tq:])
            o_ref[0, :, h0 * dh:(h0 + 2) * dh] = o.astype(o_ref.dtype)


def _swa_attention(proj, slopes, sinks):
    b, s, _ = proj.shape
    win = SWA_WINDOW
    e = MIX_WIDTH
    kvw = SWA_KV_HEADS * 2 * SWA_HEAD_DIM
    assert s % win == 0 and s >= 2 * win
    smem = pl.BlockSpec(memory_space=pltpu.SMEM)
    return pl.pallas_call(
        _swa_attn_kernel,
        grid=(b, s // win),
        in_specs=[
            smem, smem,
            pl.BlockSpec((1, win, e), lambda bi, i: (bi, i, 0)),
            pl.BlockSpec((1, s, kvw), lambda bi, i: (bi, 0, 2 * e // kvw)),
            pl.BlockSpec((1, s, kvw), lambda bi, i: (bi, 0, 2 * e // kvw + 1)),
        ],
        out_specs=pl.BlockSpec((1, win, e), lambda bi, i: (bi, i, 0)),
        out_shape=jax.ShapeDtypeStruct((b, s, e), BF16),
        compiler_params=_params("parallel", "arbitrary"),
        name="swa_attn",
    )(slopes, sinks, proj, proj, proj)


def _s5_ssm_kernel(u_ref, perm_ref, bb_ref, a_ref, cc_ref, d_ref, y_ref, st_ref, bu_ref, xb_ref,
                   *, bsz, tt, chunk_lanes):
    ns = S5_SLAB_STATE
    cw = chunk_lanes

    @pl.when(pl.program_id(1) == 0)
    def _():
        st_ref[...] = jnp.zeros_like(st_ref)

    tp = PERM_STEPS
    u32 = jnp.concatenate(
        [jnp.dot(perm_ref[0], u_ref[:, g * tp:(g + 1) * tp, :].reshape(bsz * tp, S5_SLAB),
                 preferred_element_type=F32) for g in range(tt // tp)], axis=0)
    u = u32.astype(BF16)
    y = d_ref[0] * u32
    for c in range(ns // cw):
        lo = c * cw
        bu_ref[2 * c] = jnp.dot(u, bb_ref[0, :, lo:lo + cw], preferred_element_type=F32)
        bu_ref[2 * c + 1] = jnp.dot(u, bb_ref[0, :, ns + lo:ns + lo + cw],
                                    preferred_element_type=F32)
        ar = jnp.broadcast_to(a_ref[0, 0:1, lo:lo + cw], (bsz, cw))
        ai = jnp.broadcast_to(a_ref[0, 1:2, lo:lo + cw], (bsz, cw))
        xr = st_ref[0, :, lo:lo + cw]
        xi = st_ref[1, :, lo:lo + cw]
        for t in range(tt):
            rows = slice(t * bsz, (t + 1) * bsz)
            xr, xi = (ar * xr - ai * xi + bu_ref[2 * c, rows, :],
                      ar * xi + ai * xr + bu_ref[2 * c + 1, rows, :])
            xb_ref[2 * c, rows, :] = xr.astype(BF16)
            xb_ref[2 * c + 1, rows, :] = xi.astype(BF16)
        st_ref[0, :, lo:lo + cw] = xr
        st_ref[1, :, lo:lo + cw] = xi
        y = (y + jnp.dot(xb_ref[2 * c], cc_ref[0, lo:lo + cw], preferred_element_type=F32)
             + jnp.dot(xb_ref[2 * c + 1], cc_ref[0, ns + lo:ns + lo + cw],
                       preferred_element_type=F32))
    y = y.astype(BF16)
    for g in range(tt // tp):
        y_bt = jnp.dot(perm_ref[1], y[g * bsz * tp:(g + 1) * bsz * tp],
                       preferred_element_type=F32)
        y_ref[:, g * tp:(g + 1) * tp, :] = y_bt.reshape(bsz, tp, S5_SLAB).astype(y_ref.dtype)


def _s5_ssm(proj, bd_b, a_bar, bd_c, d_skip, *, tt=32, chunk_lanes=512):
    b, s, _ = proj.shape
    tt = min(tt, s)
    assert s % tt == 0 and tt % PERM_STEPS == 0 and b % 16 == 0
    n_slab = MIX_WIDTH // S5_SLAB
    ns = S5_SLAB_STATE
    n_buf = 2 * ns // chunk_lanes
    rows = b * tt
    prow = jnp.arange(b * PERM_STEPS)
    fwd = jax.nn.one_hot((prow % b) * PERM_STEPS + prow // b, b * PERM_STEPS, dtype=BF16)
    perm = jnp.stack([fwd, fwd.T])
    return pl.pallas_call(
        functools.partial(_s5_ssm_kernel, bsz=b, tt=tt, chunk_lanes=chunk_lanes),
        grid=(n_slab, s // tt),
        in_specs=[
            pl.BlockSpec((b, tt, S5_SLAB), lambda sl, t: (0, t, sl)),
            pl.BlockSpec((2, b * PERM_STEPS, b * PERM_STEPS), lambda sl, t: (0, 0, 0)),
            pl.BlockSpec((1, S5_SLAB, 2 * ns), lambda sl, t: (sl, 0, 0)),
            pl.BlockSpec((1, 2, ns), lambda sl, t: (sl, 0, 0)),
            pl.BlockSpec((1, 2 * ns, S5_SLAB), lambda sl, t: (sl, 0, 0)),
            pl.BlockSpec((1, 1, S5_SLAB), lambda sl, t: (sl, 0, 0)),
        ],
        out_specs=pl.BlockSpec((b, tt, S5_SLAB), lambda sl, t: (0, t, sl)),
        out_shape=jax.ShapeDtypeStruct((b, s, MIX_WIDTH), BF16),
        scratch_shapes=[pltpu.VMEM((2, b, ns), F32),
                        pltpu.VMEM((n_buf, rows, chunk_lanes), F32),
                        pltpu.VMEM((n_buf, rows, chunk_lanes), BF16)],
        compiler_params=_params("parallel", "arbitrary"),
        name="s5_ssm",
    )(proj, perm, bd_b, a_bar, bd_c, d_skip)


def _s5_discretize(a_re, a_im, log_dt, b_re, b_im, c_re, c_im, d_skip):
    lr, li = a_re.astype(F32), a_im.astype(F32)
    dt = jnp.exp(log_dt.astype(F32))[:, None]
    mag = jnp.exp(lr * dt)
    ab_re, ab_im = mag * jnp.cos(li * dt), mag * jnp.sin(li * dt)
    den = lr * lr + li * li
    f_re = ((ab_re - 1.0) * lr + ab_im * li) / den
    f_im = (ab_im * lr - (ab_re - 1.0) * li) / den
    br, bi = b_re.astype(F32), b_im.astype(F32)
    bb_re = f_re[..., None] * br - f_im[..., None] * bi
    bb_im = f_re[..., None] * bi + f_im[..., None] * br
    n_slab = MIX_WIDTH // S5_SLAB
    gs = S5_SLAB // S5_GROUP
    eye = jnp.eye(gs, dtype=F32)

    def pack_b(m):
        m = m.reshape(n_slab, gs, S5_STATE, S5_GROUP)
        return jnp.einsum('sgpc,gh->sgchp', m, eye).reshape(n_slab, S5_SLAB, S5_SLAB_STATE)

    def pack_c(m):
        m = m.reshape(n_slab, gs, S5_GROUP, S5_STATE)
        return jnp.einsum('sgcp,gh->shpgc', m, eye).reshape(n_slab, S5_SLAB_STATE, S5_SLAB)

    bd_b = jnp.concatenate([pack_b(bb_re), pack_b(bb_im)], axis=2).astype(BF16)
    bd_c = jnp.concatenate([pack_c(c_re.astype(F32)), pack_c(-c_im.astype(F32))], axis=1).astype(BF16)
    a_bar = jnp.stack([ab_re.reshape(n_slab, S5_SLAB_STATE), ab_im.reshape(n_slab, S5_SLAB_STATE)],
                      axis=1)
    return bd_b, a_bar, bd_c, d_skip.astype(F32).reshape(n_slab, 1, S5_SLAB)


def _glu_kernel(y_ref, w_ref, b_ref, o_ref, *, chunks):
    cm = y_ref.shape[0] // chunks
    for c in range(chunks):
        rows = slice(c * cm, (c + 1) * cm)
        g = jax.nn.gelu(y_ref[rows, :].astype(F32))
        r = jnp.dot(g.astype(BF16), w_ref[...], preferred_element_type=F32) + b_ref[...]
        o_ref[rows, :] = (g * _sigmoid(r)).astype(o_ref.dtype)


def _glu(y2d, w, bias, *, tm=512, chunks=4):
    t, e = y2d.shape
    tm = min(tm, t)
    assert t % tm == 0 and tm % chunks == 0
    return pl.pallas_call(
        functools.partial(_glu_kernel, chunks=chunks),
        grid=(t // tm,),
        in_specs=[
            pl.BlockSpec((tm, e), lambda i: (i, 0)),
            pl.BlockSpec((e, e), lambda i: (0, 0), pipeline_mode=pl.Buffered(1)),
            pl.BlockSpec((1, e), lambda i: (0, 0)),
        ],
        out_specs=pl.BlockSpec((tm, e), lambda i: (i, 0)),
        out_shape=jax.ShapeDtypeStruct((t, e), BF16),
        compiler_params=_params("parallel"),
        name="s5_glu",
    )(y2d, w, bias.reshape(1, e).astype(F32))


def _s5_layer(x2d, h2d, bsz, seq, post_g, next_g, w_in, a_re, a_im, log_dt, b_re, b_im, c_re,
              c_im, d_skip, w_glu, b_glu, w_out):
    e = MIX_WIDTH
    proj = _in_proj(h2d, w_in.astype(BF16))
    bd_b, a_bar, bd_c, d3 = _s5_discretize(a_re, a_im, log_dt, b_re, b_im, c_re, c_im, d_skip)
    y = _s5_ssm(proj.reshape(bsz, seq, 2 * e), bd_b, a_bar, bd_c, d3)
    y = _glu(y.reshape(bsz * seq, e), w_glu.astype(BF16), b_glu)
    return _gate_out(y, proj, 1, x2d, w_out.astype(BF16), post_g, next_g)


def _attn_projections(h2d, seq, w_in):
    w = w_in.astype(BF16)
    proj = _in_proj(h2d, w, skip_block=2)
    vt = _in_proj_t(h2d, w, 2, seq)
    return proj, vt


def _diff_layer(x2d, h2d, bsz, seq, layer_idx, post_g, next_g, w_in, lq1, lk1, lq2, lk2, subln,
                w_out):
    e = MIX_WIDTH
    proj, vt = _attn_projections(h2d, seq, w_in)
    lam_init = 0.8 - 0.6 * math.exp(-0.3 * layer_idx)
    lam = (jnp.exp(jnp.sum(lq1.astype(F32) * lk1.astype(F32)))
           - jnp.exp(jnp.sum(lq2.astype(F32) * lk2.astype(F32))) + lam_init).reshape(1)
    o = _diff_attention(proj.reshape(bsz, seq, 3 * e), vt, _alibi_slopes(DIFF_HEADS), lam, subln,
                        1.0 - lam_init)
    return _gate_out(o.reshape(bsz * seq, e), proj, 2, x2d, w_out.astype(BF16), post_g, next_g)


def _moba_layer(x2d, h2d, bsz, seq, post_g, next_g, w_in, w_out):
    e = MIX_WIDTH
    proj, vt = _attn_projections(h2d, seq, w_in)
    o = _moba_attention(proj.reshape(bsz, seq, 3 * e), vt, _alibi_slopes(MOBA_HEADS))
    return _gate_out(o.reshape(bsz * seq, e), proj, 2, x2d, w_out.astype(BF16), post_g, next_g)


def _swa_layer(x2d, h2d, bsz, seq, post_g, next_g, w_in, sinks, w_out):
    e = MIX_WIDTH
    dh = SWA_HEAD_DIM
    nqc, nkc = SWA_Q_HEADS * dh, SWA_KV_HEADS * dh
    wq, wk = w_in[:, :nqc], w_in[:, nqc:nqc + nkc]
    wv, wz = w_in[:, nqc + nkc:nqc + 2 * nkc], w_in[:, nqc + 2 * nkc:]

    def dup(w):
        w = w.reshape(-1, SWA_KV_HEADS, 1, dh)
        return jnp.broadcast_to(w, (w.shape[0], SWA_KV_HEADS, 2, dh)).reshape(-1, 2 * nkc)

    w_cat = jnp.concatenate([wq, wz, dup(wk), dup(wv)], axis=1).astype(BF16)
    proj = _in_proj(h2d, w_cat)
    o = _swa_attention(proj.reshape(bsz, seq, w_cat.shape[1]), _alibi_slopes(SWA_Q_HEADS),
                       sinks.astype(F32))
    return _gate_out(o.reshape(bsz * seq, e), proj, 1, x2d, w_out.astype(BF16), post_g, next_g)


def kernel(x, pre_norm, post_norm, s5_w_in, s5_a_re, s5_a_im, s5_log_dt, s5_b_re, s5_b_im, s5_c_re, s5_c_im, s5_d, s5_w_glu, s5_b_glu, s5_w_out, diff_w_in, diff_lq1, diff_lk1, diff_lq2, diff_lk2, diff_subln, diff_w_out, moba_w_in, moba_w_out, swa_w_in, swa_sinks, swa_w_out):
    bsz, seq, d = x.shape
    depth = pre_norm.shape[0]
    x2d = x.reshape(bsz * seq, d)
    h2d = _prenorm(x2d, pre_norm[0])
    for i in range(depth):
        kind, j = i % 4, i // 4
        post_g = post_norm[i]
        next_g = pre_norm[i + 1] if i + 1 < depth else None
        if kind == 0:
            x2d, h2d = _s5_layer(x2d, h2d, bsz, seq, post_g, next_g, s5_w_in[j], s5_a_re[j],
                                 s5_a_im[j], s5_log_dt[j], s5_b_re[j], s5_b_im[j], s5_c_re[j],
                                 s5_c_im[j], s5_d[j], s5_w_glu[j], s5_b_glu[j], s5_w_out[j])
        elif kind == 1:
            x2d, h2d = _diff_layer(x2d, h2d, bsz, seq, i, post_g, next_g, diff_w_in[j],
                                   diff_lq1[j], diff_lk1[j], diff_lq2[j], diff_lk2[j],
                                   diff_subln[j], diff_w_out[j])
        elif kind == 2:
            x2d, h2d = _moba_layer(x2d, h2d, bsz, seq, post_g, next_g, moba_w_in[j],
                                   moba_w_out[j])
        else:
            x2d, h2d = _swa_layer(x2d, h2d, bsz, seq, post_g, next_g, swa_w_in[j], swa_sinks[j],
                                  swa_w_out[j])
    return x2d.reshape(bsz, seq, d)
```

```python
import functools
import math

import jax
import jax.numpy as jnp
from jax import lax
from jax.experimental import pallas as pl
from jax.experimental.pallas import tpu as pltpu

F32 = jnp.float32
BF16 = jnp.bfloat16

D_MODEL = 2048
MIX_WIDTH = D_MODEL
RMS_EPS = 1e-6
NEG_INF = -1e30
LOG2E = math.log2(math.e)
LANES = 128

S5_GROUP = 16
S5_GROUPS = MIX_WIDTH // S5_GROUP
S5_STATE = 64
S5_SLAB = 256
S5_SLAB_STATE = (S5_SLAB // S5_GROUP) * S5_STATE
PERM_STEPS = 16

DIFF_HEADS = 16
DIFF_HEAD_DIM = 64
MOBA_HEADS = 16
MOBA_HEAD_DIM = 128
MOBA_BLOCK = 256
MOBA_TOPK = 3
SWA_Q_HEADS = 32
SWA_KV_HEADS = 4
SWA_HEAD_DIM = 64
SWA_WINDOW = 128

VMEM_LIMIT = 56 * 1024 * 1024


def _params(*sem):
    return pltpu.CompilerParams(dimension_semantics=sem, vmem_limit_bytes=VMEM_LIMIT)


def _alibi_slopes(n_heads):
    return 2.0 ** (-8.0 * jnp.arange(1, n_heads + 1, dtype=F32) / n_heads)


def _dot_nt(a, b):
    return lax.dot_general(a, b, (((1,), (1,)), ((), ())), preferred_element_type=F32)


def _sigmoid(x):
    return 0.5 * jnp.tanh(0.5 * x) + 0.5


def _rms_normalize(x, gain):
    ms = jnp.mean(x * x, axis=-1, keepdims=True)
    return x * lax.rsqrt(ms + RMS_EPS) * gain


def _prenorm_kernel(x_ref, g_ref, h_ref):
    h_ref[...] = _rms_normalize(x_ref[...], g_ref[...]).astype(BF16)


def _prenorm(x2d, gain, *, tm=512):
    t, d = x2d.shape
    tm = min(tm, t)
    assert t % tm == 0
    return pl.pallas_call(
        _prenorm_kernel,
        grid=(t // tm,),
        in_specs=[pl.BlockSpec((tm, d), lambda i: (i, 0)), pl.BlockSpec((1, d), lambda i: (0, 0))],
        out_specs=pl.BlockSpec((tm, d), lambda i: (i, 0)),
        out_shape=jax.ShapeDtypeStruct((t, d), BF16),
        compiler_params=_params("parallel"),
        name="prenorm",
    )(x2d, gain.reshape(1, d).astype(F32))


def _col_tile(n, cap=2048):
    return max(t for t in range(256, cap + 1, 256) if n % t == 0)


def _matmul_kernel(h_ref, w_ref, o_ref):
    o_ref[...] = jnp.dot(h_ref[...], w_ref[...], preferred_element_type=F32).astype(o_ref.dtype)


def _in_proj(h2d, w, *, skip_block=None, tm=1024):
    t, d = h2d.shape
    tm = min(tm, t)
    tn = _col_tile(w.shape[1])
    assert t % tm == 0
    if skip_block is None:
        n = w.shape[1]
        w_map = lambda j, i: (0, j)
    else:
        n = w.shape[1] - tn
        w_map = lambda j, i: (0, j + (j >= skip_block).astype(jnp.int32))
    return pl.pallas_call(
        _matmul_kernel,
        grid=(n // tn, t // tm),
        in_specs=[
            pl.BlockSpec((tm, d), lambda j, i: (i, 0)),
            pl.BlockSpec((d, tn), w_map),
        ],
        out_specs=pl.BlockSpec((tm, tn), lambda j, i: (i, j)),
        out_shape=jax.ShapeDtypeStruct((t, n), BF16),
        compiler_params=_params("parallel", "parallel"),
        name="in_proj",
    )(h2d, w)


def _matmul_t_kernel(h_ref, w_ref, o_ref, wt_ref):
    @pl.when(pl.program_id(1) == 0)
    def _():
        wt_ref[...] = w_ref[...].T

    o_ref[0] = _dot_nt(wt_ref[...], h_ref[...]).astype(o_ref.dtype)


def _in_proj_t(h2d, w, col_block, seq, *, tm=1024):
    t, d = h2d.shape
    tm = min(tm, seq)
    tn = _col_tile(w.shape[1])
    assert seq % tm == 0
    tps = seq // tm
    return pl.pallas_call(
        _matmul_t_kernel,
        grid=(1, t // tm),
        in_specs=[
            pl.BlockSpec((tm, d), lambda j, i: (i, 0)),
            pl.BlockSpec((d, tn), lambda j, i: (0, col_block)),
        ],
        out_specs=pl.BlockSpec((1, tn, tm), lambda j, i: (i // tps, 0, i % tps)),
        out_shape=jax.ShapeDtypeStruct((t // seq, tn, seq), BF16),
        scratch_shapes=[pltpu.VMEM((tn, d), BF16)],
        compiler_params=_params("parallel", "arbitrary"),
        name="in_proj_t",
    )(h2d, w)


def _gate_out_kernel(o_ref, z_ref, x_ref, w_ref, g_ref, gn_ref, out_ref, *h_ref, chunks):
    cm = o_ref.shape[0] // chunks
    for c in range(chunks):
        rows = slice(c * cm, (c + 1) * cm)
        hz = z_ref[rows, :] * 0.5
        y = o_ref[rows, :] * (hz * jnp.tanh(hz) + hz)
        r = jnp.dot(y, w_ref[...], preferred_element_type=F32)
        xn = x_ref[rows, :] + _rms_normalize(r, g_ref[...])
        out_ref[rows, :] = xn
        if h_ref:
            h_ref[0][rows, :] = _rms_normalize(xn, gn_ref[...]).astype(BF16)


def _gate_out(o2d, zsrc, z_block, x2d, w, gain, next_gain, *, tm=512, chunks=4):
    t, e = o2d.shape
    d = w.shape[1]
    tm = min(tm, t)
    assert t % tm == 0 and tm % chunks == 0
    row_map = lambda i: (i, 0)
    const = lambda i: (0, 0)
    z_map = lambda i: (i, z_block)
    with_h = next_gain is not None
    gn = (next_gain if with_h else gain).reshape(1, d).astype(F32)
    out_specs = [pl.BlockSpec((tm, d), row_map)]
    out_shape = [jax.ShapeDtypeStruct((t, d), F32)]
    if with_h:
        out_specs.append(pl.BlockSpec((tm, d), row_map))
        out_shape.append(jax.ShapeDtypeStruct((t, d), BF16))
    res = pl.pallas_call(
        functools.partial(_gate_out_kernel, chunks=chunks),
        grid=(t // tm,),
        in_specs=[
            pl.BlockSpec((tm, e), row_map),
            pl.BlockSpec((tm, e), z_map),
            pl.BlockSpec((tm, d), row_map),
            pl.BlockSpec((e, d), const, pipeline_mode=pl.Buffered(1)),
            pl.BlockSpec((1, d), const),
            pl.BlockSpec((1, d), const),
        ],
        out_specs=out_specs,
        out_shape=out_shape,
        compiler_params=_params("parallel"),
        name="gate_out",
    )(o2d, zsrc, x2d, w, gain.reshape(1, d).astype(F32), gn)
    return (res[0], res[1]) if with_h else (res[0], None)


BIG = 1e30
SUM_ROWS = 16


def _softmax_step_t(s_ref, c, ok, vt, m_ref, acc_ref):
    m_old = m_ref[...]
    mt = jnp.max(s_ref[...], axis=0, keepdims=True) + c
    if ok is not None:
        mt = jnp.where(ok, mt, NEG_INF)
    m_new = jnp.maximum(m_old, mt)
    alpha = jnp.exp2(m_old - m_new)
    r = m_new - c
    if ok is not None:
        r = jnp.where(ok, r, BIG)
    p = jnp.exp2(s_ref[...] - r).astype(BF16)
    vt1 = jnp.concatenate([vt, jnp.ones((SUM_ROWS, vt.shape[1]), BF16)], axis=0)
    acc_ref[...] = alpha * acc_ref[...] + jnp.dot(vt1, p, preferred_element_type=F32)
    m_ref[...] = m_new


def _paired_causal_sweep(n_blocks, init, scores, finish, finalize):
    n_past = n_blocks - 1

    for i in range(n_blocks // 2):
        blk_a, blk_b = i, n_blocks - 1 - i
        init()
        tiles = []
        for p in range(n_past):
            tiles.append((blk_a, p, 0, False) if p < i else (blk_b, p - i, 1, False))
        tiles += [(blk_a, blk_a, 0, True), (blk_b, blk_b, 1, True)]
        scores(tiles[0][0], tiles[0][1], 0)
        for p, (qb, j, slot, diagonal) in enumerate(tiles):
            if p + 1 < len(tiles):
                scores(tiles[p + 1][0], tiles[p + 1][1], (p + 1) % 2)
            finish(qb, j, slot, p % 2, diagonal)
        finalize(blk_a, 0)
        finalize(blk_b, 1)


def _diff_attn_kernel(slopes_ref, lam_ref, q_ref, k_ref, vt_ref, g_ref, o_ref,
                      m_ref, acc_ref, bias_ref, qq_ref, s_ref, *, tq, hp, nqb, out_scale):
    hg = pl.program_id(1)
    lam = lam_ref[0]
    d = DIFF_HEAD_DIM
    w = 2 * d
    reps = 2 * tq // LANES

    lane = lax.broadcasted_iota(jnp.int32, (tq, w), 1)
    krow = lax.broadcasted_iota(jnp.int32, (tq, LANES), 0).astype(F32)
    slope2 = [slopes_ref[hg * hp + hh] * LOG2E for hh in range(hp)]
    for hh in range(hp):
        bias_ref[hh] = slope2[hh] * krow

    def prep(qb, carry):
        rows = pl.ds(pl.multiple_of(qb * tq, tq), tq)
        for hh in range(hp):
            q = q_ref[0, rows, hh * w:(hh + 1) * w].astype(F32) * (d ** -0.5 * LOG2E)
            qq_ref[qb, hh, :tq] = jnp.where(lane < d, q, 0.0).astype(BF16)
            qq_ref[qb, hh, tq:] = jnp.where(lane >= d, q, 0.0).astype(BF16)
        return carry

    lax.fori_loop(0, nqb, prep, 0)

    def init():
        m_ref[...] = jnp.full_like(m_ref, NEG_INF)
        acc_ref[...] = jnp.zeros_like(acc_ref)

    def scores(qb, j, buf):
        keys = pl.ds(j * tq, tq)
        for hh in range(hp):
            k = k_ref[0, keys, hh * w:(hh + 1) * w]
            s_ref[buf, hh] = (_dot_nt(k, qq_ref[qb, hh])
                              + jnp.concatenate([bias_ref[hh]] * reps, axis=1))

    def finish(qb, j, slot, buf, diagonal):
        keys = pl.ds(j * tq, tq)
        for hh in range(hp):
            vt = vt_ref[0, hh * w:(hh + 1) * w, keys]
            if diagonal:
                kr = lax.broadcasted_iota(jnp.int32, (tq, 2 * tq), 0)
                qc = lax.broadcasted_iota(jnp.int32, (tq, 2 * tq), 1)
                qc = jnp.where(qc >= tq, qc - tq, qc)
                s_ref[buf, hh] = jnp.where(qc >= kr, s_ref[buf, hh], NEG_INF)
            c = slope2[hh] * float((j - qb) * tq)
            _softmax_step_t(s_ref.at[buf, hh], c, None, vt, m_ref.at[slot, hh],
                            acc_ref.at[slot, hh])

    g = jnp.concatenate([g_ref[...]] * (tq // LANES), axis=1)

    def finalize(qb, slot):
        rows = pl.ds(qb * tq, tq)
        for hh in range(hp):
            a = acc_ref[slot, hh, :w] / acc_ref[slot, hh, w:w + 1]
            o = a[:, :tq] - lam * a[:, tq:]
            ms = jnp.mean(o * o, axis=0, keepdims=True)
            o = o * lax.rsqrt(ms + RMS_EPS) * g * out_scale
            o_ref[0, rows, hh * w:(hh + 1) * w] = o.T.astype(o_ref.dtype)

    _paired_causal_sweep(nqb, init, scores, finish, finalize)


def _diff_attention(proj, vt, slopes, lam, subln, out_scale, *, tq=256, hp=2):
    b, s, _ = proj.shape
    nh = DIFF_HEADS
    w = 2 * DIFF_HEAD_DIM
    tq = min(tq, s // 2)
    nqb = s // tq
    assert s % tq == 0 and tq % LANES == 0 and nh % hp == 0 and nqb % 2 == 0
    ng = nh // hp
    smem = pl.BlockSpec(memory_space=pltpu.SMEM)
    g = jnp.broadcast_to(subln.astype(F32)[:, None], (w, LANES))
    return pl.pallas_call(
        functools.partial(_diff_attn_kernel, tq=tq, hp=hp, nqb=nqb, out_scale=out_scale),
        grid=(b, ng),
        in_specs=[
            smem, smem,
            pl.BlockSpec((1, s, hp * w), lambda bi, h: (bi, 0, h)),
            pl.BlockSpec((1, s, hp * w), lambda bi, h: (bi, 0, ng + h)),
            pl.BlockSpec((1, hp * w, s), lambda bi, h: (bi, h, 0)),
            pl.BlockSpec((w, LANES), lambda bi, h: (0, 0)),
        ],
        out_specs=pl.BlockSpec((1, s, hp * w), lambda bi, h: (bi, 0, h)),
        out_shape=jax.ShapeDtypeStruct((b, s, MIX_WIDTH), BF16),
        scratch_shapes=[pltpu.VMEM((2, hp, 1, 2 * tq), F32),
                        pltpu.VMEM((2, hp, w + SUM_ROWS, 2 * tq), F32),
                        pltpu.VMEM((hp, tq, LANES), F32),
                        pltpu.VMEM((nqb, hp, 2 * tq, w), BF16),
                        pltpu.VMEM((2, hp, tq, 2 * tq), F32)],
        compiler_params=_params("parallel", "parallel"),
        name="diff_attn",
    )(slopes, lam, proj, proj, vt, g)


def _moba_attn_kernel(slopes_ref, q_ref, k_ref, vt_ref, o_ref, m_ref, acc_ref, bias_ref,
                      qs_ref, sel_ref, s_ref, *, nb, hp):
    hg = pl.program_id(1)
    blk = MOBA_BLOCK
    dh = MOBA_HEAD_DIM
    seq = nb * blk
    reps = blk // LANES

    krow = lax.broadcasted_iota(jnp.int32, (blk, LANES), 0).astype(F32)
    bidx = lax.broadcasted_iota(jnp.int32, (nb, seq), 0)
    qblk = lax.broadcasted_iota(jnp.int32, (nb, seq), 1) // blk
    slope2 = [slopes_ref[hg * hp + hh] * LOG2E for hh in range(hp)]
    for hh in range(hp):
        bias_ref[hh] = slope2[hh] * krow
        q = q_ref[0, :, hh * dh:(hh + 1) * dh]
        qs_ref[hh] = (q.astype(F32) * (dh ** -0.5 * LOG2E)).astype(BF16)

        kh = k_ref[0, :, hh * dh:(hh + 1) * dh].astype(F32)
        kmean = jnp.mean(kh.reshape(nb, blk, dh), axis=1)
        km_hi = kmean.astype(BF16)
        km_lo = (kmean - km_hi.astype(F32)).astype(BF16)
        gate = _dot_nt(km_hi, q) + _dot_nt(km_lo, q)

        rank = jnp.zeros((nb, seq), jnp.int32)
        for jp in range(nb):
            gj = gate[jp:jp + 1, :]
            before = (gj > gate) | ((gj == gate) & (jp < bidx))
            rank = rank + jnp.where(before, 1, 0) * jnp.where(jp < qblk, 1, 0)
        sel_ref[hh] = jnp.where((rank < MOBA_TOPK) & (bidx < qblk), 1.0, 0.0)

    def init():
        m_ref[...] = jnp.full_like(m_ref, NEG_INF)
        acc_ref[...] = jnp.zeros_like(acc_ref)

    def scores(qb, j, buf):
        keys = pl.ds(j * blk, blk)
        rows = pl.ds(qb * blk, blk)
        for hh in range(hp):
            k = k_ref[0, keys, hh * dh:(hh + 1) * dh]
            s_ref[buf, hh] = (_dot_nt(k, qs_ref[hh, rows, :])
                              + jnp.concatenate([bias_ref[hh]] * reps, axis=1))

    def finish(qb, j, slot, buf, diagonal):
        keys = pl.ds(j * blk, blk)
        cols = pl.ds(qb * blk, blk)
        for hh in range(hp):
            vt = vt_ref[0, hh * dh:(hh + 1) * dh, keys]
            c = slope2[hh] * float((j - qb) * blk)
            if diagonal:
                kr = lax.broadcasted_iota(jnp.int32, (blk, blk), 0)
                qc = lax.broadcasted_iota(jnp.int32, (blk, blk), 1)
                s_ref[buf, hh] = jnp.where(qc >= kr, s_ref[buf, hh], NEG_INF)
                ok = None
            else:
                tile_idx = lax.broadcasted_iota(jnp.int32, (nb, blk), 0)
                chosen = jnp.where(tile_idx == j, sel_ref[hh, :, cols], 0.0)
                ok = jnp.max(chosen, axis=0, keepdims=True) > 0.0
            _softmax_step_t(s_ref.at[buf, hh], c, ok, vt, m_ref.at[slot, hh],
                            acc_ref.at[slot, hh])

    def finalize(qb, slot):
        rows = pl.ds(qb * blk, blk)
        for hh in range(hp):
            o = acc_ref[slot, hh, :dh] / acc_ref[slot, hh, dh:dh + 1]
            o_ref[0, rows, hh * dh:(hh + 1) * dh] = o.T.astype(o_ref.dtype)

    _paired_causal_sweep(nb, init, scores, finish, finalize)


def _moba_attention(proj, vt, slopes, *, hp=4):
    b, s, _ = proj.shape
    nh = MOBA_HEADS
    dh = MOBA_HEAD_DIM
    blk = MOBA_BLOCK
    nb = s // blk
    assert s % blk == 0 and nh % hp == 0 and nb % 2 == 0
    ng = nh // hp
    smem = pl.BlockSpec(memory_space=pltpu.SMEM)
    return pl.pallas_call(
        functools.partial(_moba_attn_kernel, nb=nb, hp=hp),
        grid=(b, ng),
        in_specs=[
            smem,
            pl.BlockSpec((1, s, hp * dh), lambda bi, h: (bi, 0, h)),
            pl.BlockSpec((1, s, hp * dh), lambda bi, h: (bi, 0, ng + h)),
            pl.BlockSpec((1, hp * dh, s), lambda bi, h: (bi, h, 0)),
        ],
        out_specs=pl.BlockSpec((1, s, hp * dh), lambda bi, h: (bi, 0, h)),
        out_shape=jax.ShapeDtypeStruct((b, s, MIX_WIDTH), BF16),
        scratch_shapes=[pltpu.VMEM((2, hp, 1, blk), F32),
                        pltpu.VMEM((2, hp, dh + SUM_ROWS, blk), F32),
                        pltpu.VMEM((hp, blk, LANES), F32),
                        pltpu.VMEM((hp, s, dh), BF16),
                        pltpu.VMEM((hp, nb, s), F32),
                        pltpu.VMEM((2, hp, blk, blk), F32)],
        compiler_params=_params("parallel", "parallel"),
        name="moba_attn",
    )(slopes, proj, proj, vt)


def _swa_attn_kernel(slopes_ref, sinks_ref, q_ref, k_ref, v_ref, o_ref):
    i = pl.program_id(1)
    win = SWA_WINDOW
    dh = SWA_HEAD_DIM
    grp = SWA_Q_HEADS // SWA_KV_HEADS
    tq = win

    kstart = pl.multiple_of(jnp.maximum(i - 1, 0) * win, win)
    qpos = i * win + lax.broadcasted_iota(jnp.int32, (2 * tq, 1), 0) % tq
    kpos = kstart + lax.broadcasted_iota(jnp.int32, (1, 2 * win), 1)
    dist = qpos - kpos
    valid = (dist >= 0) & (dist < win)
    distf = dist.astype(F32)
    top = lax.broadcasted_iota(jnp.int32, (2 * tq, 1), 0) < tq
    lane = lax.broadcasted_iota(jnp.int32, (tq, 2 * dh), 1)

    for kvh in range(SWA_KV_HEADS):
        k = k_ref[0, pl.ds(kstart, 2 * win), kvh * 2 * dh:(kvh + 1) * 2 * dh]
        v = v_ref[0, pl.ds(kstart, 2 * win), kvh * 2 * dh:(kvh + 1) * 2 * dh]
        for pair in range(grp // 2):
            h0 = kvh * grp + 2 * pair
            q = q_ref[0, :, h0 * dh:(h0 + 2) * dh].astype(F32) * (dh ** -0.5)
            qq = jnp.concatenate([jnp.where(lane < dh, q, 0.0), jnp.where(lane >= dh, q, 0.0)],
                                 axis=0).astype(BF16)
            slope = jnp.where(top, slopes_ref[h0], slopes_ref[h0 + 1])
            sink = jnp.where(top, sinks_ref[h0], sinks_ref[h0 + 1])
            s = _dot_nt(qq, k) - slope * distf
            s = jnp.where(valid, s, NEG_INF)
            m = jnp.maximum(jnp.max(s, axis=-1, keepdims=True), sink)
            p = jnp.exp(s - m)
            denom = jnp.sum(p, axis=-1, keepdims=True) + jnp.exp(sink - m)
            o2 = jnp.dot(p.astype(BF16), v, preferred_element_type=F32) / denom
            o = jnp.where(lane < dh, o2[:tq], o2[tq:])
            o_ref[0, :, h0 * dh:(h0 + 2) * dh] = o.astype(o_ref.dtype)


def _swa_attention(proj, slopes, sinks):
    b, s, _ = proj.shape
    win = SWA_WINDOW
    e = MIX_WIDTH
    kvw = SWA_KV_HEADS * 2 * SWA_HEAD_DIM
    assert s % win == 0 and s >= 2 * win
    smem = pl.BlockSpec(memory_space=pltpu.SMEM)
    return pl.pallas_call(
        _swa_attn_kernel,
        grid=(b, s // win),
        in_specs=[
            smem, smem,
            pl.BlockSpec((1, win, e), lambda bi, i: (bi, i, 0)),
            pl.BlockSpec((1, s, kvw), lambda bi, i: (bi, 0, 2 * e // kvw)),
            pl.BlockSpec((1, s, kvw), lambda bi, i: (bi, 0, 2 * e // kvw + 1)),
        ],
        out_specs=pl.BlockSpec((1, win, e), lambda bi, i: (bi, i, 0)),
        out_shape=jax.ShapeDtypeStruct((b, s, e), BF16),
        compiler_params=_params("parallel", "arbitrary"),
        name="swa_attn",
    )(slopes, sinks, proj, proj, proj)


def _s5_ssm_kernel(u_ref, perm_ref, bb_ref, a_ref, cc_ref, d_ref, y_ref, st_ref, bu_ref, xb_ref,
                   *, bsz, tt, chunk_lanes):
    ns = S5_SLAB_STATE
    cw = chunk_lanes

    @pl.when(pl.program_id(1) == 0)
    def _():
        st_ref[...] = jnp.zeros_like(st_ref)

    tp = PERM_STEPS
    u32 = jnp.concatenate(
        [jnp.dot(perm_ref[0], u_ref[:, g * tp:(g + 1) * tp, :].reshape(bsz * tp, S5_SLAB),
                 preferred_element_type=F32) for g in range(tt // tp)], axis=0)
    u = u32.astype(BF16)
    y = d_ref[0] * u32
    for c in range(ns // cw):
        lo = c * cw
        bu_ref[2 * c] = jnp.dot(u, bb_ref[0, :, lo:lo + cw], preferred_element_type=F32)
        bu_ref[2 * c + 1] = jnp.dot(u, bb_ref[0, :, ns + lo:ns + lo + cw],
                                    preferred_element_type=F32)
        ar = jnp.broadcast_to(a_ref[0, 0:1, lo:lo + cw], (bsz, cw))
        ai = jnp.broadcast_to(a_ref[0, 1:2, lo:lo + cw], (bsz, cw))
        xr = st_ref[0, :, lo:lo + cw]
        xi = st_ref[1, :, lo:lo + cw]
        for t in range(tt):
            rows = slice(t * bsz, (t + 1) * bsz)
            xr, xi = (ar * xr - ai * xi + bu_ref[2 * c, rows, :],
                      ar * xi + ai * xr + bu_ref[2 * c + 1, rows, :])
            xb_ref[2 * c, rows, :] = xr.astype(BF16)
            xb_ref[2 * c + 1, rows, :] = xi.astype(BF16)
        st_ref[0, :, lo:lo + cw] = xr
        st_ref[1, :, lo:lo + cw] = xi
        y = (y + jnp.dot(xb_ref[2 * c], cc_ref[0, lo:lo + cw], preferred_element_type=F32)
             + jnp.dot(xb_ref[2 * c + 1], cc_ref[0, ns + lo:ns + lo + cw],
                       preferred_element_type=F32))
    y = y.astype(BF16)
    for g in range(tt // tp):
        y_bt = jnp.dot(perm_ref[1], y[g * bsz * tp:(g + 1) * bsz * tp],
                       preferred_element_type=F32)
        y_ref[:, g * tp:(g + 1) * tp, :] = y_bt.reshape(bsz, tp, S5_SLAB).astype(y_ref.dtype)


def _s5_ssm(proj, bd_b, a_bar, bd_c, d_skip, *, tt=32, chunk_lanes=512):
    b, s, _ = proj.shape
    tt = min(tt, s)
    assert s % tt == 0 and tt % PERM_STEPS == 0 and b % 16 == 0
    n_slab = MIX_WIDTH // S5_SLAB
    ns = S5_SLAB_STATE
    n_buf = 2 * ns // chunk_lanes
    rows = b * tt
    prow = jnp.arange(b * PERM_STEPS)
    fwd = jax.nn.one_hot((prow % b) * PERM_STEPS + prow // b, b * PERM_STEPS, dtype=BF16)
    perm = jnp.stack([fwd, fwd.T])
    return pl.pallas_call(
        functools.partial(_s5_ssm_kernel, bsz=b, tt=tt, chunk_lanes=chunk_lanes),
        grid=(n_slab, s // tt),
        in_specs=[
            pl.BlockSpec((b, tt, S5_SLAB), lambda sl, t: (0, t, sl)),
            pl.BlockSpec((2, b * PERM_STEPS, b * PERM_STEPS), lambda sl, t: (0, 0, 0)),
            pl.BlockSpec((1, S5_SLAB, 2 * ns), lambda sl, t: (sl, 0, 0)),
            pl.BlockSpec((1, 2, ns), lambda sl, t: (sl, 0, 0)),
            pl.BlockSpec((1, 2 * ns, S5_SLAB), lambda sl, t: (sl, 0, 0)),
            pl.BlockSpec((1, 1, S5_SLAB), lambda sl, t: (sl, 0, 0)),
        ],
        out_specs=pl.BlockSpec((b, tt, S5_SLAB), lambda sl, t: (0, t, sl)),
        out_shape=jax.ShapeDtypeStruct((b, s, MIX_WIDTH), BF16),
        scratch_shapes=[pltpu.VMEM((2, b, ns), F32),
                        pltpu.VMEM((n_buf, rows, chunk_lanes), F32),
                        pltpu.VMEM((n_buf, rows, chunk_lanes), BF16)],
        compiler_params=_params("parallel", "arbitrary"),
        name="s5_ssm",
    )(proj, perm, bd_b, a_bar, bd_c, d_skip)


def _s5_discretize(a_re, a_im, log_dt, b_re, b_im, c_re, c_im, d_skip):
    lr, li = a_re.astype(F32), a_im.astype(F32)
    dt = jnp.exp(log_dt.astype(F32))[:, None]
    mag = jnp.exp(lr * dt)
    ab_re, ab_im = mag * jnp.cos(li * dt), mag * jnp.sin(li * dt)
    den = lr * lr + li * li
    f_re = ((ab_re - 1.0) * lr + ab_im * li) / den
    f_im = (ab_im * lr - (ab_re - 1.0) * li) / den
    br, bi = b_re.astype(F32), b_im.astype(F32)
    bb_re = f_re[..., None] * br - f_im[..., None] * bi
    bb_im = f_re[..., None] * bi + f_im[..., None] * br
    n_slab = MIX_WIDTH // S5_SLAB
    gs = S5_SLAB // S5_GROUP
    eye = jnp.eye(gs, dtype=F32)

    def pack_b(m):
        m = m.reshape(n_slab, gs, S5_STATE, S5_GROUP)
        return jnp.einsum('sgpc,gh->sgchp', m, eye).reshape(n_slab, S5_SLAB, S5_SLAB_STATE)

    def pack_c(m):
        m = m.reshape(n_slab, gs, S5_GROUP, S5_STATE)
        return jnp.einsum('sgcp,gh->shpgc', m, eye).reshape(n_slab, S5_SLAB_STATE, S5_SLAB)

    bd_b = jnp.concatenate([pack_b(bb_re), pack_b(bb_im)], axis=2).astype(BF16)
    bd_c = jnp.concatenate([pack_c(c_re.astype(F32)), pack_c(-c_im.astype(F32))], axis=1).astype(BF16)
    a_bar = jnp.stack([ab_re.reshape(n_slab, S5_SLAB_STATE), ab_im.reshape(n_slab, S5_SLAB_STATE)],
                      axis=1)
    return bd_b, a_bar, bd_c, d_skip.astype(F32).reshape(n_slab, 1, S5_SLAB)


def _glu_kernel(y_ref, w_ref, b_ref, o_ref, *, chunks):
    cm = y_ref.shape[0] // chunks
    for c in range(chunks):
        rows = slice(c * cm, (c + 1) * cm)
        g = jax.nn.gelu(y_ref[rows, :].astype(F32))
        r = jnp.dot(g.astype(BF16), w_ref[...], preferred_element_type=F32) + b_ref[...]
        o_ref[rows, :] = (g * _sigmoid(r)).astype(o_ref.dtype)


def _glu(y2d, w, bias, *, tm=512, chunks=4):
    t, e = y2d.shape
    tm = min(tm, t)
    assert t % tm == 0 and tm % chunks == 0
    return pl.pallas_call(
        functools.partial(_glu_kernel, chunks=chunks),
        grid=(t // tm,),
        in_specs=[
            pl.BlockSpec((tm, e), lambda i: (i, 0)),
            pl.BlockSpec((e, e), lambda i: (0, 0), pipeline_mode=pl.Buffered(1)),
            pl.BlockSpec((1, e), lambda i: (0, 0)),
        ],
        out_specs=pl.BlockSpec((tm, e), lambda i: (i, 0)),
        out_shape=jax.ShapeDtypeStruct((t, e), BF16),
        compiler_params=_params("parallel"),
        name="s5_glu",
    )(y2d, w, bias.reshape(1, e).astype(F32))


def _s5_layer(x2d, h2d, bsz, seq, post_g, next_g, w_in, a_re, a_im, log_dt, b_re, b_im, c_re,
              c_im, d_skip, w_glu, b_glu, w_out):
    e = MIX_WIDTH
    proj = _in_proj(h2d, w_in.astype(BF16))
    bd_b, a_bar, bd_c, d3 = _s5_discretize(a_re, a_im, log_dt, b_re, b_im, c_re, c_im, d_skip)
    y = _s5_ssm(proj.reshape(bsz, seq, 2 * e), bd_b, a_bar, bd_c, d3)
    y = _glu(y.reshape(bsz * seq, e), w_glu.astype(BF16), b_glu)
    return _gate_out(y, proj, 1, x2d, w_out.astype(BF16), post_g, next_g)


def _attn_projections(h2d, seq, w_in):
    w = w_in.astype(BF16)
    proj = _in_proj(h2d, w, skip_block=2)
    vt = _in_proj_t(h2d, w, 2, seq)
    return proj, vt


def _diff_layer(x2d, h2d, bsz, seq, layer_idx, post_g, next_g, w_in, lq1, lk1, lq2, lk2, subln,
                w_out):
    e = MIX_WIDTH
    proj, vt = _attn_projections(h2d, seq, w_in)
    lam_init = 0.8 - 0.6 * math.exp(-0.3 * layer_idx)
    lam = (jnp.exp(jnp.sum(lq1.astype(F32) * lk1.astype(F32)))
           - jnp.exp(jnp.sum(lq2.astype(F32) * lk2.astype(F32))) + lam_init).reshape(1)
    o = _diff_attention(proj.reshape(bsz, seq, 3 * e), vt, _alibi_slopes(DIFF_HEADS), lam, subln,
                        1.0 - lam_init)
    return _gate_out(o.reshape(bsz * seq, e), proj, 2, x2d, w_out.astype(BF16), post_g, next_g)


def _moba_layer(x2d, h2d, bsz, seq, post_g, next_g, w_in, w_out):
    e = MIX_WIDTH
    proj, vt = _attn_projections(h2d, seq, w_in)
    o = _moba_attention(proj.reshape(bsz, seq, 3 * e), vt, _alibi_slopes(MOBA_HEADS))
    return _gate_out(o.reshape(bsz * seq, e), proj, 2, x2d, w_out.astype(BF16), post_g, next_g)


def _swa_layer(x2d, h2d, bsz, seq, post_g, next_g, w_in, sinks, w_out):
    e = MIX_WIDTH
    dh = SWA_HEAD_DIM
    nqc, nkc = SWA_Q_HEADS * dh, SWA_KV_HEADS * dh
    wq, wk = w_in[:, :nqc], w_in[:, nqc:nqc + nkc]
    wv, wz = w_in[:, nqc + nkc:nqc + 2 * nkc], w_in[:, nqc + 2 * nkc:]

    def dup(w):
        w = w.reshape(-1, SWA_KV_HEADS, 1, dh)
        return jnp.broadcast_to(w, (w.shape[0], SWA_KV_HEADS, 2, dh)).reshape(-1, 2 * nkc)

    w_cat = jnp.concatenate([wq, wz, dup(wk), dup(wv)], axis=1).astype(BF16)
    proj = _in_proj(h2d, w_cat)
    o = _swa_attention(proj.reshape(bsz, seq, w_cat.shape[1]), _alibi_slopes(SWA_Q_HEADS),
                       sinks.astype(F32))
    return _gate_out(o.reshape(bsz * seq, e), proj, 1, x2d, w_out.astype(BF16), post_g, next_g)


def kernel(x, pre_norm, post_norm, s5_w_in, s5_a_re, s5_a_im, s5_log_dt, s5_b_re, s5_b_im, s5_c_re, s5_c_im, s5_d, s5_w_glu, s5_b_glu, s5_w_out, diff_w_in, diff_lq1, diff_lk1, diff_lq2, diff_lk2, diff_subln, diff_w_out, moba_w_in, moba_w_out, swa_w_in, swa_sinks, swa_w_out):
    bsz, seq, d = x.shape
    depth = pre_norm.shape[0]
    x2d = x.reshape(bsz * seq, d)
    h2d = _prenorm(x2d, pre_norm[0])
    for i in range(depth):
        kind, j = i % 4, i // 4
        post_g = post_norm[i]
        next_g = pre_norm[i + 1] if i + 1 < depth else None
        if kind == 0:
            x2d, h2d = _s5_layer(x2d, h2d, bsz, seq, post_g, next_g, s5_w_in[j], s5_a_re[j],
                                 s5_a_im[j], s5_log_dt[j], s5_b_re[j], s5_b_im[j], s5_c_re[j],
                                 s5_c_im[j], s5_d[j], s5_w_glu[j], s5_b_glu[j], s5_w_out[j])
        elif kind == 1:
            x2d, h2d = _diff_layer(x2d, h2d, bsz, seq, i, post_g, next_g, diff_w_in[j],
                                   diff_lq1[j], diff_lk1[j], diff_lq2[j], diff_lk2[j],
                                   diff_subln[j], diff_w_out[j])
        elif kind == 2:
            x2d, h2d = _moba_layer(x2d, h2d, bsz, seq, post_g, next_g, moba_w_in[j],
                                   moba_w_out[j])
        else:
            x2d, h2d = _swa_layer(x2d, h2d, bsz, seq, post_g, next_g, swa_w_in[j], swa_sinks[j],
                                  swa_w_out[j])
    return x2d.reshape(bsz, seq, d)
```

```python
import functools
import math

import jax
import jax.numpy as jnp
from jax import lax
from jax.experimental import pallas as pl
from jax.experimental.pallas import tpu as pltpu

F32 = jnp.float32
BF16 = jnp.bfloat16

D_MODEL = 2048
MIX_WIDTH = D_MODEL
RMS_EPS = 1e-6
NEG_INF = -1e30
LOG2E = math.log2(math.e)
LANES = 128

S5_GROUP = 16
S5_GROUPS = MIX_WIDTH // S5_GROUP
S5_STATE = 64
S5_SLAB = 256
S5_SLAB_STATE = (S5_SLAB // S5_GROUP) * S5_STATE
PERM_STEPS = 16

DIFF_HEADS = 16
DIFF_HEAD_DIM = 64
MOBA_HEADS = 16
MOBA_HEAD_DIM = 128
MOBA_BLOCK = 256
MOBA_TOPK = 3
SWA_Q_HEADS = 32
SWA_KV_HEADS = 4
SWA_HEAD_DIM = 64
SWA_WINDOW = 128

VMEM_LIMIT = 56 * 1024 * 1024


def _params(*sem):
    return pltpu.CompilerParams(dimension_semantics=sem, vmem_limit_bytes=VMEM_LIMIT)


def _alibi_slopes(n_heads):
    return 2.0 ** (-8.0 * jnp.arange(1, n_heads + 1, dtype=F32) / n_heads)


def _dot_nt(a, b):
    return lax.dot_general(a, b, (((1,), (1,)), ((), ())), preferred_element_type=F32)


def _sigmoid(x):
    return 0.5 * jnp.tanh(0.5 * x) + 0.5


def _rms_normalize(x, gain):
    ms = jnp.mean(x * x, axis=-1, keepdims=True)
    return x * lax.rsqrt(ms + RMS_EPS) * gain


def _prenorm_kernel(x_ref, g_ref, h_ref):
    h_ref[...] = _rms_normalize(x_ref[...], g_ref[...]).astype(BF16)


def _prenorm(x2d, gain, *, tm=512):
    t, d = x2d.shape
    tm = min(tm, t)
    assert t % tm == 0
    return pl.pallas_call(
        _prenorm_kernel,
        grid=(t // tm,),
        in_specs=[pl.BlockSpec((tm, d), lambda i: (i, 0)), pl.BlockSpec((1, d), lambda i: (0, 0))],
        out_specs=pl.BlockSpec((tm, d), lambda i: (i, 0)),
        out_shape=jax.ShapeDtypeStruct((t, d), BF16),
        compiler_params=_params("parallel"),
        name="prenorm",
    )(x2d, gain.reshape(1, d).astype(F32))


def _col_tile(n, cap=2048):
    return max(t for t in range(256, cap + 1, 256) if n % t == 0)


def _matmul_kernel(h_ref, w_ref, o_ref):
    o_ref[...] = jnp.dot(h_ref[...], w_ref[...], preferred_element_type=F32).astype(o_ref.dtype)


def _in_proj(h2d, w, *, skip_block=None, tm=1024):
    t, d = h2d.shape
    tm = min(tm, t)
    tn = _col_tile(w.shape[1])
    assert t % tm == 0
    if skip_block is None:
        n = w.shape[1]
        w_map = lambda j, i: (0, j)
    else:
        n = w.shape[1] - tn
        w_map = lambda j, i: (0, j + (j >= skip_block).astype(jnp.int32))
    return pl.pallas_call(
        _matmul_kernel,
        grid=(n // tn, t // tm),
        in_specs=[
            pl.BlockSpec((tm, d), lambda j, i: (i, 0)),
            pl.BlockSpec((d, tn), w_map),
        ],
        out_specs=pl.BlockSpec((tm, tn), lambda j, i: (i, j)),
        out_shape=jax.ShapeDtypeStruct((t, n), BF16),
        compiler_params=_params("parallel", "parallel"),
        name="in_proj",
    )(h2d, w)


def _matmul_t_kernel(h_ref, w_ref, o_ref, wt_ref):
    @pl.when(pl.program_id(1) == 0)
    def _():
        wt_ref[...] = w_ref[...].T

    o_ref[0] = _dot_nt(wt_ref[...], h_ref[...]).astype(o_ref.dtype)


def _in_proj_t(h2d, w, col_block, seq, *, tm=1024):
    t, d = h2d.shape
    tm = min(tm, seq)
    tn = _col_tile(w.shape[1])
    assert seq % tm == 0
    tps = seq // tm
    return pl.pallas_call(
        _matmul_t_kernel,
        grid=(1, t // tm),
        in_specs=[
            pl.BlockSpec((tm, d), lambda j, i: (i, 0)),
            pl.BlockSpec((d, tn), lambda j, i: (0, col_block)),
        ],
        out_specs=pl.BlockSpec((1, tn, tm), lambda j, i: (i // tps, 0, i % tps)),
        out_shape=jax.ShapeDtypeStruct((t // seq, tn, seq), BF16),
        scratch_shapes=[pltpu.VMEM((tn, d), BF16)],
        compiler_params=_params("parallel", "arbitrary"),
        name="in_proj_t",
    )(h2d, w)


def _gate_out_kernel(o_ref, z_ref, x_ref, w_ref, g_ref, gn_ref, out_ref, *h_ref, chunks):
    cm = o_ref.shape[0] // chunks
    for c in range(chunks):
        rows = slice(c * cm, (c + 1) * cm)
        hz = z_ref[rows, :] * 0.5
        y = o_ref[rows, :] * (hz * jnp.tanh(hz) + hz)
        r = jnp.dot(y, w_ref[...], preferred_element_type=F32)
        xn = x_ref[rows, :] + _rms_normalize(r, g_ref[...])
        out_ref[rows, :] = xn
        if h_ref:
            h_ref[0][rows, :] = _rms_normalize(xn, gn_ref[...]).astype(BF16)


def _gate_out(o2d, zsrc, z_block, x2d, w, gain, next_gain, *, tm=512, chunks=4):
    t, e = o2d.shape
    d = w.shape[1]
    tm = min(tm, t)
    assert t % tm == 0 and tm % chunks == 0
    row_map = lambda i: (i, 0)
    const = lambda i: (0, 0)
    z_map = lambda i: (i, z_block)
    with_h = next_gain is not None
    gn = (next_gain if with_h else gain).reshape(1, d).astype(F32)
    out_specs = [pl.BlockSpec((tm, d), row_map)]
    out_shape = [jax.ShapeDtypeStruct((t, d), F32)]
    if with_h:
        out_specs.append(pl.BlockSpec((tm, d), row_map))
        out_shape.append(jax.ShapeDtypeStruct((t, d), BF16))
    res = pl.pallas_call(
        functools.partial(_gate_out_kernel, chunks=chunks),
        grid=(t // tm,),
        in_specs=[
            pl.BlockSpec((tm, e), row_map),
            pl.BlockSpec((tm, e), z_map),
            pl.BlockSpec((tm, d), row_map),
            pl.BlockSpec((e, d), const, pipeline_mode=pl.Buffered(1)),
            pl.BlockSpec((1, d), const),
            pl.BlockSpec((1, d), const),
        ],
        out_specs=out_specs,
        out_shape=out_shape,
        compiler_params=_params("parallel"),
        name="gate_out",
    )(o2d, zsrc, x2d, w, gain.reshape(1, d).astype(F32), gn)
    return (res[0], res[1]) if with_h else (res[0], None)


BIG = 1e30
SUM_ROWS = 16


def _softmax_step_t(s_ref, c, ok, vt, m_ref, acc_ref):
    m_old = m_ref[...]
    mt = jnp.max(s_ref[...], axis=0, keepdims=True) + c
    if ok is not None:
        mt = jnp.where(ok, mt, NEG_INF)
    m_new = jnp.maximum(m_old, mt)
    alpha = jnp.exp2(m_old - m_new)
    r = m_new - c
    if ok is not None:
        r = jnp.where(ok, r, BIG)
    p = jnp.exp2(s_ref[...] - r).astype(BF16)
    vt1 = jnp.concatenate([vt, jnp.ones((SUM_ROWS, vt.shape[1]), BF16)], axis=0)
    acc_ref[...] = alpha * acc_ref[...] + jnp.dot(vt1, p, preferred_element_type=F32)
    m_ref[...] = m_new


def _paired_causal_sweep(n_blocks, init, scores, finish, finalize):
    n_past = n_blocks - 1

    for i in range(n_blocks // 2):
        blk_a, blk_b = i, n_blocks - 1 - i
        init()
        tiles = []
        for p in range(n_past):
            tiles.append((blk_a, p, 0, False) if p < i else (blk_b, p - i, 1, False))
        tiles += [(blk_a, blk_a, 0, True), (blk_b, blk_b, 1, True)]
        scores(tiles[0][0], tiles[0][1], 0)
        for p, (qb, j, slot, diagonal) in enumerate(tiles):
            if p + 1 < len(tiles):
                scores(tiles[p + 1][0], tiles[p + 1][1], (p + 1) % 2)
            finish(qb, j, slot, p % 2, diagonal)
        finalize(blk_a, 0)
        finalize(blk_b, 1)


def _diff_attn_kernel(slopes_ref, lam_ref, q_ref, k_ref, vt_ref, g_ref, o_ref,
                      m_ref, acc_ref, bias_ref, qq_ref, s_ref, *, tq, hp, nqb, out_scale):
    hg = pl.program_id(1)
    lam = lam_ref[0]
    d = DIFF_HEAD_DIM
    w = 2 * d
    reps = 2 * tq // LANES

    lane = lax.broadcasted_iota(jnp.int32, (tq, w), 1)
    krow = lax.broadcasted_iota(jnp.int32, (tq, LANES), 0).astype(F32)
    slope2 = [slopes_ref[hg * hp + hh] * LOG2E for hh in range(hp)]
    for hh in range(hp):
        bias_ref[hh] = slope2[hh] * krow

    def prep(qb, carry):
        rows = pl.ds(pl.multiple_of(qb * tq, tq), tq)
        for hh in range(hp):
            q = q_ref[0, rows, hh * w:(hh + 1) * w].astype(F32) * (d ** -0.5 * LOG2E)
            qq_ref[qb, hh, :tq] = jnp.where(lane < d, q, 0.0).astype(BF16)
            qq_ref[qb, hh, tq:] = jnp.where(lane >= d, q, 0.0).astype(BF16)
        return carry

    lax.fori_loop(0, nqb, prep, 0)

    def init():
        m_ref[...] = jnp.full_like(m_ref, NEG_INF)
        acc_ref[...] = jnp.zeros_like(acc_ref)

    def scores(qb, j, buf):
        keys = pl.ds(j * tq, tq)
        for hh in range(hp):
            k = k_ref[0, keys, hh * w:(hh + 1) * w]
            s_ref[buf, hh] = (_dot_nt(k, qq_ref[qb, hh])
                              + jnp.concatenate([bias_ref[hh]] * reps, axis=1))

    def finish(qb, j, slot, buf, diagonal):
        keys = pl.ds(j * tq, tq)
        for hh in range(hp):
            vt = vt_ref[0, hh * w:(hh + 1) * w, keys]
            if diagonal:
                kr = lax.broadcasted_iota(jnp.int32, (tq, 2 * tq), 0)
                qc = lax.broadcasted_iota(jnp.int32, (tq, 2 * tq), 1)
                qc = jnp.where(qc >= tq, qc - tq, qc)
                s_ref[buf, hh] = jnp.where(qc >= kr, s_ref[buf, hh], NEG_INF)
            c = slope2[hh] * float((j - qb) * tq)
            _softmax_step_t(s_ref.at[buf, hh], c, None, vt, m_ref.at[slot, hh],
                            acc_ref.at[slot, hh])

    g = jnp.concatenate([g_ref[...]] * (tq // LANES), axis=1)

    def finalize(qb, slot):
        rows = pl.ds(qb * tq, tq)
        for hh in range(hp):
            a = acc_ref[slot, hh, :w] / acc_ref[slot, hh, w:w + 1]
            o = a[:, :tq] - lam * a[:, tq:]
            ms = jnp.mean(o * o, axis=0, keepdims=True)
            o = o * lax.rsqrt(ms + RMS_EPS) * g * out_scale
            o_ref[0, rows, hh * w:(hh + 1) * w] = o.T.astype(o_ref.dtype)

    _paired_causal_sweep(nqb, init, scores, finish, finalize)


def _diff_attention(proj, vt, slopes, lam, subln, out_scale, *, tq=256, hp=2):
    b, s, _ = proj.shape
    nh = DIFF_HEADS
    w = 2 * DIFF_HEAD_DIM
    tq = min(tq, s // 2)
    nqb = s // tq
    assert s % tq == 0 and tq % LANES == 0 and nh % hp == 0 and nqb % 2 == 0
    ng = nh // hp
    smem = pl.BlockSpec(memory_space=pltpu.SMEM)
    g = jnp.broadcast_to(subln.astype(F32)[:, None], (w, LANES))
    return pl.pallas_call(
        functools.partial(_diff_attn_kernel, tq=tq, hp=hp, nqb=nqb, out_scale=out_scale),
        grid=(b, ng),
        in_specs=[
            smem, smem,
            pl.BlockSpec((1, s, hp * w), lambda bi, h: (bi, 0, h)),
            pl.BlockSpec((1, s, hp * w), lambda bi, h: (bi, 0, ng + h)),
            pl.BlockSpec((1, hp * w, s), lambda bi, h: (bi, h, 0)),
            pl.BlockSpec((w, LANES), lambda bi, h: (0, 0)),
        ],
        out_specs=pl.BlockSpec((1, s, hp * w), lambda bi, h: (bi, 0, h)),
        out_shape=jax.ShapeDtypeStruct((b, s, MIX_WIDTH), BF16),
        scratch_shapes=[pltpu.VMEM((2, hp, 1, 2 * tq), F32),
                        pltpu.VMEM((2, hp, w + SUM_ROWS, 2 * tq), F32),
                        pltpu.VMEM((hp, tq, LANES), F32),
                        pltpu.VMEM((nqb, hp, 2 * tq, w), BF16),
                        pltpu.VMEM((2, hp, tq, 2 * tq), F32)],
        compiler_params=_params("parallel", "parallel"),
        name="diff_attn",
    )(slopes, lam, proj, proj, vt, g)


def _moba_attn_kernel(slopes_ref, q_ref, k_ref, vt_ref, o_ref, m_ref, acc_ref, bias_ref,
                      qs_ref, sel_ref, s_ref, *, nb, hp):
    hg = pl.program_id(1)
    blk = MOBA_BLOCK
    dh = MOBA_HEAD_DIM
    seq = nb * blk
    reps = blk // LANES

    krow = lax.broadcasted_iota(jnp.int32, (blk, LANES), 0).astype(F32)
    bidx = lax.broadcasted_iota(jnp.int32, (nb, seq), 0)
    qblk = lax.broadcasted_iota(jnp.int32, (nb, seq), 1) // blk
    slope2 = [slopes_ref[hg * hp + hh] * LOG2E for hh in range(hp)]
    for hh in range(hp):
        bias_ref[hh] = slope2[hh] * krow
        q = q_ref[0, :, hh * dh:(hh + 1) * dh]
        qs_ref[hh] = (q.astype(F32) * (dh ** -0.5 * LOG2E)).astype(BF16)

        kh = k_ref[0, :, hh * dh:(hh + 1) * dh].astype(F32)
        kmean = jnp.mean(kh.reshape(nb, blk, dh), axis=1)
        km_hi = kmean.astype(BF16)
        km_lo = (kmean - km_hi.astype(F32)).astype(BF16)
        gate = _dot_nt(km_hi, q) + _dot_nt(km_lo, q)

        rank = jnp.zeros((nb, seq), jnp.int32)
        for jp in range(nb):
            gj = gate[jp:jp + 1, :]
            before = (gj > gate) | ((gj == gate) & (jp < bidx))
            rank = rank + jnp.where(before, 1, 0) * jnp.where(jp < qblk, 1, 0)
        sel_ref[hh] = jnp.where((rank < MOBA_TOPK) & (bidx < qblk), 1.0, 0.0)

    def init():
        m_ref[...] = jnp.full_like(m_ref, NEG_INF)
        acc_ref[...] = jnp.zeros_like(acc_ref)

    def scores(qb, j, buf):
        keys = pl.ds(j * blk, blk)
        rows = pl.ds(qb * blk, blk)
        for hh in range(hp):
            k = k_ref[0, keys, hh * dh:(hh + 1) * dh]
            s_ref[buf, hh] = (_dot_nt(k, qs_ref[hh, rows, :])
                              + jnp.concatenate([bias_ref[hh]] * reps, axis=1))

    def finish(qb, j, slot, buf, diagonal):
        keys = pl.ds(j * blk, blk)
        cols = pl.ds(qb * blk, blk)
        for hh in range(hp):
            vt = vt_ref[0, hh * dh:(hh + 1) * dh, keys]
            c = slope2[hh] * float((j - qb) * blk)
            if diagonal:
                kr = lax.broadcasted_iota(jnp.int32, (blk, blk), 0)
                qc = lax.broadcasted_iota(jnp.int32, (blk, blk), 1)
                s_ref[buf, hh] = jnp.where(qc >= kr, s_ref[buf, hh], NEG_INF)
                ok = None
            else:
                tile_idx = lax.broadcasted_iota(jnp.int32, (nb, blk), 0)
                chosen = jnp.where(tile_idx == j, sel_ref[hh, :, cols], 0.0)
                ok = jnp.max(chosen, axis=0, keepdims=True) > 0.0
            _softmax_step_t(s_ref.at[buf, hh], c, ok, vt, m_ref.at[slot, hh],
                            acc_ref.at[slot, hh])

    def finalize(qb, slot):
        rows = pl.ds(qb * blk, blk)
        for hh in range(hp):
            o = acc_ref[slot, hh, :dh] / acc_ref[slot, hh, dh:dh + 1]
            o_ref[0, rows, hh * dh:(hh + 1) * dh] = o.T.astype(o_ref.dtype)

    _paired_causal_sweep(nb, init, scores, finish, finalize)


def _moba_attention(proj, vt, slopes, *, hp=4):
    b, s, _ = proj.shape
    nh = MOBA_HEADS
    dh = MOBA_HEAD_DIM
    blk = MOBA_BLOCK
    nb = s // blk
    assert s % blk == 0 and nh % hp == 0 and nb % 2 == 0
    ng = nh // hp
    smem = pl.BlockSpec(memory_space=pltpu.SMEM)
    return pl.pallas_call(
        functools.partial(_moba_attn_kernel, nb=nb, hp=hp),
        grid=(b, ng),
        in_specs=[
            smem,
            pl.BlockSpec((1, s, hp * dh), lambda bi, h: (bi, 0, h)),
            pl.BlockSpec((1, s, hp * dh), lambda bi, h: (bi, 0, ng + h)),
            pl.BlockSpec((1, hp * dh, s), lambda bi, h: (bi, h, 0)),
        ],
        out_specs=pl.BlockSpec((1, s, hp * dh), lambda bi, h: (bi, 0, h)),
        out_shape=jax.ShapeDtypeStruct((b, s, MIX_WIDTH), BF16),
        scratch_shapes=[pltpu.VMEM((2, hp, 1, blk), F32),
                        pltpu.VMEM((2, hp, dh + SUM_ROWS, blk), F32),
                        pltpu.VMEM((hp, blk, LANES), F32),
                        pltpu.VMEM((hp, s, dh), BF16),
                        pltpu.VMEM((hp, nb, s), F32),
                        pltpu.VMEM((2, hp, blk, blk), F32)],
        compiler_params=_params("parallel", "parallel"),
        name="moba_attn",
    )(slopes, proj, proj, vt)


def _swa_attn_kernel(slopes_ref, sinks_ref, q_ref, k_ref, v_ref, o_ref):
    i = pl.program_id(1)
    win = SWA_WINDOW
    dh = SWA_HEAD_DIM
    grp = SWA_Q_HEADS // SWA_KV_HEADS
    tq = win

    kstart = pl.multiple_of(jnp.maximum(i - 1, 0) * win, win)
    qpos = i * win + lax.broadcasted_iota(jnp.int32, (2 * tq, 1), 0) % tq
    kpos = kstart + lax.broadcasted_iota(jnp.int32, (1, 2 * win), 1)
    dist = qpos - kpos
    valid = (dist >= 0) & (dist < win)
    distf = dist.astype(F32)
    top = lax.broadcasted_iota(jnp.int32, (2 * tq, 1), 0) < tq
    lane = lax.broadcasted_iota(jnp.int32, (tq, 2 * dh), 1)

    for kvh in range(SWA_KV_HEADS):
        k = k_ref[0, pl.ds(kstart, 2 * win), kvh * 2 * dh:(kvh + 1) * 2 * dh]
        v = v_ref[0, pl.ds(kstart, 2 * win), kvh * 2 * dh:(kvh + 1) * 2 * dh]
        for pair in range(grp // 2):
            h0 = kvh * grp + 2 * pair
            q = q_ref[0, :, h0 * dh:(h0 + 2) * dh].astype(F32) * (dh ** -0.5)
            qq = jnp.concatenate([jnp.where(lane < dh, q, 0.0), jnp.where(lane >= dh, q, 0.0)],
                                 axis=0).astype(BF16)
            slope = jnp.where(top, slopes_ref[h0], slopes_ref[h0 + 1])
            sink = jnp.where(top, sinks_ref[h0], sinks_ref[h0 + 1])
            s = _dot_nt(qq, k) - slope * distf
            s = jnp.where(valid, s, NEG_INF)
            m = jnp.maximum(jnp.max(s, axis=-1, keepdims=True), sink)
            p = jnp.exp(s - m)
            denom = jnp.sum(p, axis=-1, keepdims=True) + jnp.exp(sink - m)
            o2 = jnp.dot(p.astype(BF16), v, preferred_element_type=F32) / denom
            o = jnp.where(lane < dh, o2[:tq], o2[tq:])
            o_ref[0, :, h0 * dh:(h0 + 2) * dh] = o.astype(o_ref.dtype)


def _swa_attention(proj, slopes, sinks):
    b, s, _ = proj.shape
    win = SWA_WINDOW
    e = MIX_WIDTH
    kvw = SWA_KV_HEADS * 2 * SWA_HEAD_DIM
    assert s % win == 0 and s >= 2 * win
    smem = pl.BlockSpec(memory_space=pltpu.SMEM)
    return pl.pallas_call(
        _swa_attn_kernel,
        grid=(b, s // win),
        in_specs=[
            smem, smem,
            pl.BlockSpec((1, win, e), lambda bi, i: (bi, i, 0)),
            pl.BlockSpec((1, s, kvw), lambda bi, i: (bi, 0, 2 * e // kvw)),
            pl.BlockSpec((1, s, kvw), lambda bi, i: (bi, 0, 2 * e // kvw + 1)),
        ],
        out_specs=pl.BlockSpec((1, win, e), lambda bi, i: (bi, i, 0)),
        out_shape=jax.ShapeDtypeStruct((b, s, e), BF16),
        compiler_params=_params("parallel", "arbitrary"),
        name="swa_attn",
    )(slopes, sinks, proj, proj, proj)


def _s5_ssm_kernel(u_ref, perm_ref, bb_ref, a_ref, cc_ref, d_ref, y_ref, st_ref, bu_ref, xb_ref,
                   *, bsz, tt, chunk_lanes):
    ns = S5_SLAB_STATE
    cw = chunk_lanes

    @pl.when(pl.program_id(1) == 0)
    def _():
        st_ref[...] = jnp.zeros_like(st_ref)

    tp = PERM_STEPS
    u32 = jnp.concatenate(
        [jnp.dot(perm_ref[0], u_ref[:, g * tp:(g + 1) * tp, :].reshape(bsz * tp, S5_SLAB),
                 preferred_element_type=F32) for g in range(tt // tp)], axis=0)
    u = u32.astype(BF16)
    y = d_ref[0] * u32
    for c in range(ns // cw):
        lo = c * cw
        bu_ref[2 * c] = jnp.dot(u, bb_ref[0, :, lo:lo + cw], preferred_element_type=F32)
        bu_ref[2 * c + 1] = jnp.dot(u, bb_ref[0, :, ns + lo:ns + lo + cw],
                                    preferred_element_type=F32)
        ar = jnp.broadcast_to(a_ref[0, 0:1, lo:lo + cw], (bsz, cw))
        ai = jnp.broadcast_to(a_ref[0, 1:2, lo:lo + cw], (bsz, cw))
        xr = st_ref[0, :, lo:lo + cw]
        xi = st_ref[1, :, lo:lo + cw]
        for t in range(tt):
            rows = slice(t * bsz, (t + 1) * bsz)
            xr, xi = (ar * xr - ai * xi + bu_ref[2 * c, rows, :],
                      ar * xi + ai * xr + bu_ref[2 * c + 1, rows, :])
            xb_ref[2 * c, rows, :] = xr.astype(BF16)
            xb_ref[2 * c + 1, rows, :] = xi.astype(BF16)
        st_ref[0, :, lo:lo + cw] = xr
        st_ref[1, :, lo:lo + cw] = xi
        y = (y + jnp.dot(xb_ref[2 * c], cc_ref[0, lo:lo + cw], preferred_element_type=F32)
             + jnp.dot(xb_ref[2 * c + 1], cc_ref[0, ns + lo:ns + lo + cw],
                       preferred_element_type=F32))
    y = y.astype(BF16)
    for g in range(tt // tp):
        y_bt = jnp.dot(perm_ref[1], y[g * bsz * tp:(g + 1) * bsz * tp],
                       preferred_element_type=F32)
        y_ref[:, g * tp:(g + 1) * tp, :] = y_bt.reshape(bsz, tp, S5_SLAB).astype(y_ref.dtype)


def _s5_ssm(proj, bd_b, a_bar, bd_c, d_skip, *, tt=64, chunk_lanes=256):
    b, s, _ = proj.shape
    tt = min(tt, s)
    assert s % tt == 0 and tt % PERM_STEPS == 0 and b % 16 == 0
    n_slab = MIX_WIDTH // S5_SLAB
    ns = S5_SLAB_STATE
    n_buf = 2 * ns // chunk_lanes
    rows = b * tt
    prow = jnp.arange(b * PERM_STEPS)
    fwd = jax.nn.one_hot((prow % b) * PERM_STEPS + prow // b, b * PERM_STEPS, dtype=BF16)
    perm = jnp.stack([fwd, fwd.T])
    return pl.pallas_call(
        functools.partial(_s5_ssm_kernel, bsz=b, tt=tt, chunk_lanes=chunk_lanes),
        grid=(n_slab, s // tt),
        in_specs=[
            pl.BlockSpec((b, tt, S5_SLAB), lambda sl, t: (0, t, sl)),
            pl.BlockSpec((2, b * PERM_STEPS, b * PERM_STEPS), lambda sl, t: (0, 0, 0)),
            pl.BlockSpec((1, S5_SLAB, 2 * ns), lambda sl, t: (sl, 0, 0)),
            pl.BlockSpec((1, 2, ns), lambda sl, t: (sl, 0, 0)),
            pl.BlockSpec((1, 2 * ns, S5_SLAB), lambda sl, t: (sl, 0, 0)),
            pl.BlockSpec((1, 1, S5_SLAB), lambda sl, t: (sl, 0, 0)),
        ],
        out_specs=pl.BlockSpec((b, tt, S5_SLAB), lambda sl, t: (0, t, sl)),
        out_shape=jax.ShapeDtypeStruct((b, s, MIX_WIDTH), BF16),
        scratch_shapes=[pltpu.VMEM((2, b, ns), F32),
                        pltpu.VMEM((n_buf, rows, chunk_lanes), F32),
                        pltpu.VMEM((n_buf, rows, chunk_lanes), BF16)],
        compiler_params=_params("parallel", "arbitrary"),
        name="s5_ssm",
    )(proj, perm, bd_b, a_bar, bd_c, d_skip)


def _s5_discretize(a_re, a_im, log_dt, b_re, b_im, c_re, c_im, d_skip):
    lr, li = a_re.astype(F32), a_im.astype(F32)
    dt = jnp.exp(log_dt.astype(F32))[:, None]
    mag = jnp.exp(lr * dt)
    ab_re, ab_im = mag * jnp.cos(li * dt), mag * jnp.sin(li * dt)
    den = lr * lr + li * li
    f_re = ((ab_re - 1.0) * lr + ab_im * li) / den
    f_im = (ab_im * lr - (ab_re - 1.0) * li) / den
    br, bi = b_re.astype(F32), b_im.astype(F32)
    bb_re = f_re[..., None] * br - f_im[..., None] * bi
    bb_im = f_re[..., None] * bi + f_im[..., None] * br
    n_slab = MIX_WIDTH // S5_SLAB
    gs = S5_SLAB // S5_GROUP
    eye = jnp.eye(gs, dtype=F32)

    def pack_b(m):
        m = m.reshape(n_slab, gs, S5_STATE, S5_GROUP)
        return jnp.einsum('sgpc,gh->sgchp', m, eye).reshape(n_slab, S5_SLAB, S5_SLAB_STATE)

    def pack_c(m):
        m = m.reshape(n_slab, gs, S5_GROUP, S5_STATE)
        return jnp.einsum('sgcp,gh->shpgc', m, eye).reshape(n_slab, S5_SLAB_STATE, S5_SLAB)

    bd_b = jnp.concatenate([pack_b(bb_re), pack_b(bb_im)], axis=2).astype(BF16)
    bd_c = jnp.concatenate([pack_c(c_re.astype(F32)), pack_c(-c_im.astype(F32))], axis=1).astype(BF16)
    a_bar = jnp.stack([ab_re.reshape(n_slab, S5_SLAB_STATE), ab_im.reshape(n_slab, S5_SLAB_STATE)],
                      axis=1)
    return bd_b, a_bar, bd_c, d_skip.astype(F32).reshape(n_slab, 1, S5_SLAB)


def _glu_kernel(y_ref, w_ref, b_ref, o_ref, *, chunks):
    cm = y_ref.shape[0] // chunks
    for c in range(chunks):
        rows = slice(c * cm, (c + 1) * cm)
        g = jax.nn.gelu(y_ref[rows, :].astype(F32))
        r = jnp.dot(g.astype(BF16), w_ref[...], preferred_element_type=F32) + b_ref[...]
        o_ref[rows, :] = (g * _sigmoid(r)).astype(o_ref.dtype)


def _glu(y2d, w, bias, *, tm=512, chunks=4):
    t, e = y2d.shape
    tm = min(tm, t)
    assert t % tm == 0 and tm % chunks == 0
    return pl.pallas_call(
        functools.partial(_glu_kernel, chunks=chunks),
        grid=(t // tm,),
        in_specs=[
            pl.BlockSpec((tm, e), lambda i: (i, 0)),
            pl.BlockSpec((e, e), lambda i: (0, 0), pipeline_mode=pl.Buffered(1)),
            pl.BlockSpec((1, e), lambda i: (0, 0)),
        ],
        out_specs=pl.BlockSpec((tm, e), lambda i: (i, 0)),
        out_shape=jax.ShapeDtypeStruct((t, e), BF16),
        compiler_params=_params("parallel"),
        name="s5_glu",
    )(y2d, w, bias.reshape(1, e).astype(F32))


def _s5_layer(x2d, h2d, bsz, seq, post_g, next_g, w_in, a_re, a_im, log_dt, b_re, b_im, c_re,
              c_im, d_skip, w_glu, b_glu, w_out):
    e = MIX_WIDTH
    proj = _in_proj(h2d, w_in.astype(BF16))
    bd_b, a_bar, bd_c, d3 = _s5_discretize(a_re, a_im, log_dt, b_re, b_im, c_re, c_im, d_skip)
    y = _s5_ssm(proj.reshape(bsz, seq, 2 * e), bd_b, a_bar, bd_c, d3)
    y = _glu(y.reshape(bsz * seq, e), w_glu.astype(BF16), b_glu)
    return _gate_out(y, proj, 1, x2d, w_out.astype(BF16), post_g, next_g)


def _attn_projections(h2d, seq, w_in):
    w = w_in.astype(BF16)
    proj = _in_proj(h2d, w, skip_block=2)
    vt = _in_proj_t(h2d, w, 2, seq)
    return proj, vt


def _diff_layer(x2d, h2d, bsz, seq, layer_idx, post_g, next_g, w_in, lq1, lk1, lq2, lk2, subln,
                w_out):
    e = MIX_WIDTH
    proj, vt = _attn_projections(h2d, seq, w_in)
    lam_init = 0.8 - 0.6 * math.exp(-0.3 * layer_idx)
    lam = (jnp.exp(jnp.sum(lq1.astype(F32) * lk1.astype(F32)))
           - jnp.exp(jnp.sum(lq2.astype(F32) * lk2.astype(F32))) + lam_init).reshape(1)
    o = _diff_attention(proj.reshape(bsz, seq, 3 * e), vt, _alibi_slopes(DIFF_HEADS), lam, subln,
                        1.0 - lam_init)
    return _gate_out(o.reshape(bsz * seq, e), proj, 2, x2d, w_out.astype(BF16), post_g, next_g)


def _moba_layer(x2d, h2d, bsz, seq, post_g, next_g, w_in, w_out):
    e = MIX_WIDTH
    proj, vt = _attn_projections(h2d, seq, w_in)
    o = _moba_attention(proj.reshape(bsz, seq, 3 * e), vt, _alibi_slopes(MOBA_HEADS))
    return _gate_out(o.reshape(bsz * seq, e), proj, 2, x2d, w_out.astype(BF16), post_g, next_g)


def _swa_layer(x2d, h2d, bsz, seq, post_g, next_g, w_in, sinks, w_out):
    e = MIX_WIDTH
    dh = SWA_HEAD_DIM
    nqc, nkc = SWA_Q_HEADS * dh, SWA_KV_HEADS * dh
    wq, wk = w_in[:, :nqc], w_in[:, nqc:nqc + nkc]
    wv, wz = w_in[:, nqc + nkc:nqc + 2 * nkc], w_in[:, nqc + 2 * nkc:]

    def dup(w):
        w = w.reshape(-1, SWA_KV_HEADS, 1, dh)
        return jnp.broadcast_to(w, (w.shape[0], SWA_KV_HEADS, 2, dh)).reshape(-1, 2 * nkc)

    w_cat = jnp.concatenate([wq, wz, dup(wk), dup(wv)], axis=1).astype(BF16)
    proj = _in_proj(h2d, w_cat)
    o = _swa_attention(proj.reshape(bsz, seq, w_cat.shape[1]), _alibi_slopes(SWA_Q_HEADS),
                       sinks.astype(F32))
    return _gate_out(o.reshape(bsz * seq, e), proj, 1, x2d, w_out.astype(BF16), post_g, next_g)


def kernel(x, pre_norm, post_norm, s5_w_in, s5_a_re, s5_a_im, s5_log_dt, s5_b_re, s5_b_im, s5_c_re, s5_c_im, s5_d, s5_w_glu, s5_b_glu, s5_w_out, diff_w_in, diff_lq1, diff_lk1, diff_lq2, diff_lk2, diff_subln, diff_w_out, moba_w_in, moba_w_out, swa_w_in, swa_sinks, swa_w_out):
    bsz, seq, d = x.shape
    depth = pre_norm.shape[0]
    x2d = x.reshape(bsz * seq, d)
    h2d = _prenorm(x2d, pre_norm[0])
    for i in range(depth):
        kind, j = i % 4, i // 4
        post_g = post_norm[i]
        next_g = pre_norm[i + 1] if i + 1 < depth else None
        if kind == 0:
            x2d, h2d = _s5_layer(x2d, h2d, bsz, seq, post_g, next_g, s5_w_in[j], s5_a_re[j],
                                 s5_a_im[j], s5_log_dt[j], s5_b_re[j], s5_b_im[j], s5_c_re[j],
                                 s5_c_im[j], s5_d[j], s5_w_glu[j], s5_b_glu[j], s5_w_out[j])
        elif kind == 1:
            x2d, h2d = _diff_layer(x2d, h2d, bsz, seq, i, post_g, next_g, diff_w_in[j],
                                   diff_lq1[j], diff_lk1[j], diff_lq2[j], diff_lk2[j],
                                   diff_subln[j], diff_w_out[j])
        elif kind == 2:
            x2d, h2d = _moba_layer(x2d, h2d, bsz, seq, post_g, next_g, moba_w_in[j],
                                   moba_w_out[j])
        else:
            x2d, h2d = _swa_layer(x2d, h2d, bsz, seq, post_g, next_g, swa_w_in[j], swa_sinks[j],
                                  swa_w_out[j])
    return x2d.reshape(bsz, seq, d)
```

```python
import functools
import math

import jax
import jax.numpy as jnp
from jax import lax
from jax.experimental import pallas as pl
from jax.experimental.pallas import tpu as pltpu

F32 = jnp.float32
BF16 = jnp.bfloat16

D_MODEL = 2048
MIX_WIDTH = D_MODEL
RMS_EPS = 1e-6
NEG_INF = -1e30
LOG2E = math.log2(math.e)
LANES = 128

S5_GROUP = 16
S5_GROUPS = MIX_WIDTH // S5_GROUP
S5_STATE = 64
S5_SLAB = 256
S5_SLAB_STATE = (S5_SLAB // S5_GROUP) * S5_STATE
PERM_STEPS = 16

DIFF_HEADS = 16
DIFF_HEAD_DIM = 64
MOBA_HEADS = 16
MOBA_HEAD_DIM = 128
MOBA_BLOCK = 256
MOBA_TOPK = 3
SWA_Q_HEADS = 32
SWA_KV_HEADS = 4
SWA_HEAD_DIM = 64
SWA_WINDOW = 128

VMEM_LIMIT = 56 * 1024 * 1024


def _params(*sem):
    return pltpu.CompilerParams(dimension_semantics=sem, vmem_limit_bytes=VMEM_LIMIT)


def _alibi_slopes(n_heads):
    return 2.0 ** (-8.0 * jnp.arange(1, n_heads + 1, dtype=F32) / n_heads)


def _dot_nt(a, b):
    return lax.dot_general(a, b, (((1,), (1,)), ((), ())), preferred_element_type=F32)


def _sigmoid(x):
    return 0.5 * jnp.tanh(0.5 * x) + 0.5


def _rms_normalize(x, gain):
    ms = jnp.mean(x * x, axis=-1, keepdims=True)
    return x * lax.rsqrt(ms + RMS_EPS) * gain


def _prenorm_kernel(x_ref, g_ref, h_ref):
    h_ref[...] = _rms_normalize(x_ref[...], g_ref[...]).astype(BF16)


def _prenorm(x2d, gain, *, tm=512):
    t, d = x2d.shape
    tm = min(tm, t)
    assert t % tm == 0
    return pl.pallas_call(
        _prenorm_kernel,
        grid=(t // tm,),
        in_specs=[pl.BlockSpec((tm, d), lambda i: (i, 0)), pl.BlockSpec((1, d), lambda i: (0, 0))],
        out_specs=pl.BlockSpec((tm, d), lambda i: (i, 0)),
        out_shape=jax.ShapeDtypeStruct((t, d), BF16),
        compiler_params=_params("parallel"),
        name="prenorm",
    )(x2d, gain.reshape(1, d).astype(F32))


def _col_tile(n, cap=2048):
    return max(t for t in range(256, cap + 1, 256) if n % t == 0)


def _matmul_kernel(h_ref, w_ref, o_ref):
    o_ref[...] = jnp.dot(h_ref[...], w_ref[...], preferred_element_type=F32).astype(o_ref.dtype)


def _in_proj(h2d, w, *, skip_block=None, tm=1024):
    t, d = h2d.shape
    tm = min(tm, t)
    tn = _col_tile(w.shape[1])
    assert t % tm == 0
    if skip_block is None:
        n = w.shape[1]
        w_map = lambda j, i: (0, j)
    else:
        n = w.shape[1] - tn
        w_map = lambda j, i: (0, j + (j >= skip_block).astype(jnp.int32))
    return pl.pallas_call(
        _matmul_kernel,
        grid=(n // tn, t // tm),
        in_specs=[
            pl.BlockSpec((tm, d), lambda j, i: (i, 0)),
            pl.BlockSpec((d, tn), w_map),
        ],
        out_specs=pl.BlockSpec((tm, tn), lambda j, i: (i, j)),
        out_shape=jax.ShapeDtypeStruct((t, n), BF16),
        compiler_params=_params("parallel", "parallel"),
        name="in_proj",
    )(h2d, w)


def _matmul_t_kernel(h_ref, w_ref, o_ref, wt_ref):
    @pl.when(pl.program_id(1) == 0)
    def _():
        wt_ref[...] = w_ref[...].T

    o_ref[0] = _dot_nt(wt_ref[...], h_ref[...]).astype(o_ref.dtype)


def _in_proj_t(h2d, w, col_block, seq, *, tm=1024):
    t, d = h2d.shape
    tm = min(tm, seq)
    tn = _col_tile(w.shape[1])
    assert seq % tm == 0
    tps = seq // tm
    return pl.pallas_call(
        _matmul_t_kernel,
        grid=(1, t // tm),
        in_specs=[
            pl.BlockSpec((tm, d), lambda j, i: (i, 0)),
            pl.BlockSpec((d, tn), lambda j, i: (0, col_block)),
        ],
        out_specs=pl.BlockSpec((1, tn, tm), lambda j, i: (i // tps, 0, i % tps)),
        out_shape=jax.ShapeDtypeStruct((t // seq, tn, seq), BF16),
        scratch_shapes=[pltpu.VMEM((tn, d), BF16)],
        compiler_params=_params("parallel", "arbitrary"),
        name="in_proj_t",
    )(h2d, w)


def _gate_out_kernel(o_ref, z_ref, x_ref, w_ref, g_ref, gn_ref, out_ref, *h_ref, chunks):
    cm = o_ref.shape[0] // chunks
    for c in range(chunks):
        rows = slice(c * cm, (c + 1) * cm)
        hz = z_ref[rows, :] * 0.5
        y = o_ref[rows, :] * (hz * jnp.tanh(hz) + hz)
        r = jnp.dot(y, w_ref[...], preferred_element_type=F32)
        xn = x_ref[rows, :] + _rms_normalize(r, g_ref[...])
        out_ref[rows, :] = xn
        if h_ref:
            h_ref[0][rows, :] = _rms_normalize(xn, gn_ref[...]).astype(BF16)


def _gate_out(o2d, zsrc, z_block, x2d, w, gain, next_gain, *, tm=512, chunks=4):
    t, e = o2d.shape
    d = w.shape[1]
    tm = min(tm, t)
    assert t % tm == 0 and tm % chunks == 0
    row_map = lambda i: (i, 0)
    const = lambda i: (0, 0)
    z_map = lambda i: (i, z_block)
    with_h = next_gain is not None
    gn = (next_gain if with_h else gain).reshape(1, d).astype(F32)
    out_specs = [pl.BlockSpec((tm, d), row_map)]
    out_shape = [jax.ShapeDtypeStruct((t, d), F32)]
    if with_h:
        out_specs.append(pl.BlockSpec((tm, d), row_map))
        out_shape.append(jax.ShapeDtypeStruct((t, d), BF16))
    res = pl.pallas_call(
        functools.partial(_gate_out_kernel, chunks=chunks),
        grid=(t // tm,),
        in_specs=[
            pl.BlockSpec((tm, e), row_map),
            pl.BlockSpec((tm, e), z_map),
            pl.BlockSpec((tm, d), row_map),
            pl.BlockSpec((e, d), const, pipeline_mode=pl.Buffered(1)),
            pl.BlockSpec((1, d), const),
            pl.BlockSpec((1, d), const),
        ],
        out_specs=out_specs,
        out_shape=out_shape,
        compiler_params=_params("parallel"),
        name="gate_out",
    )(o2d, zsrc, x2d, w, gain.reshape(1, d).astype(F32), gn)
    return (res[0], res[1]) if with_h else (res[0], None)


BIG = 1e30
SUM_ROWS = 16


def _softmax_step_t(s_ref, c, ok, vt, m_ref, acc_ref):
    m_old = m_ref[...]
    mt = jnp.max(s_ref[...], axis=0, keepdims=True) + c
    if ok is not None:
        mt = jnp.where(ok, mt, NEG_INF)
    m_new = jnp.maximum(m_old, mt)
    alpha = jnp.exp2(m_old - m_new)
    r = m_new - c
    if ok is not None:
        r = jnp.where(ok, r, BIG)
    p = jnp.exp2(s_ref[...] - r).astype(BF16)
    vt1 = jnp.concatenate([vt, jnp.ones((SUM_ROWS, vt.shape[1]), BF16)], axis=0)
    acc_ref[...] = alpha * acc_ref[...] + jnp.dot(vt1, p, preferred_element_type=F32)
    m_ref[...] = m_new


def _paired_causal_sweep(n_blocks, init, scores, finish, finalize):
    n_past = n_blocks - 1

    for i in range(n_blocks // 2):
        blk_a, blk_b = i, n_blocks - 1 - i
        init()
        tiles = []
        for p in range(n_past):
            tiles.append((blk_a, p, 0, False) if p < i else (blk_b, p - i, 1, False))
        tiles += [(blk_a, blk_a, 0, True), (blk_b, blk_b, 1, True)]
        scores(tiles[0][0], tiles[0][1], 0)
        for p, (qb, j, slot, diagonal) in enumerate(tiles):
            if p + 1 < len(tiles):
                scores(tiles[p + 1][0], tiles[p + 1][1], (p + 1) % 2)
            finish(qb, j, slot, p % 2, diagonal)
        finalize(blk_a, 0)
        finalize(blk_b, 1)


def _diff_attn_kernel(slopes_ref, lam_ref, q_ref, k_ref, vt_ref, g_ref, o_ref,
                      m_ref, acc_ref, bias_ref, qq_ref, s_ref, *, tq, hp, nqb, out_scale):
    hg = pl.program_id(1)
    lam = lam_ref[0]
    d = DIFF_HEAD_DIM
    w = 2 * d
    reps = 2 * tq // LANES

    lane = lax.broadcasted_iota(jnp.int32, (tq, w), 1)
    krow = lax.broadcasted_iota(jnp.int32, (tq, LANES), 0).astype(F32)
    slope2 = [slopes_ref[hg * hp + hh] * LOG2E for hh in range(hp)]
    for hh in range(hp):
        bias_ref[hh] = slope2[hh] * krow

    def prep(qb, carry):
        rows = pl.ds(pl.multiple_of(qb * tq, tq), tq)
        for hh in range(hp):
            q = q_ref[0, rows, hh * w:(hh + 1) * w].astype(F32) * (d ** -0.5 * LOG2E)
            qq_ref[qb, hh, :tq] = jnp.where(lane < d, q, 0.0).astype(BF16)
            qq_ref[qb, hh, tq:] = jnp.where(lane >= d, q, 0.0).astype(BF16)
        return carry

    lax.fori_loop(0, nqb, prep, 0)

    def init():
        m_ref[...] = jnp.full_like(m_ref, NEG_INF)
        acc_ref[...] = jnp.zeros_like(acc_ref)

    def scores(qb, j, buf):
        keys = pl.ds(j * tq, tq)
        for hh in range(hp):
            k = k_ref[0, keys, hh * w:(hh + 1) * w]
            s_ref[buf, hh] = (_dot_nt(k, qq_ref[qb, hh])
                              + jnp.concatenate([bias_ref[hh]] * reps, axis=1))

    def finish(qb, j, slot, buf, diagonal):
        keys = pl.ds(j * tq, tq)
        for hh in range(hp):
            vt = vt_ref[0, hh * w:(hh + 1) * w, keys]
            if diagonal:
                kr = lax.broadcasted_iota(jnp.int32, (tq, 2 * tq), 0)
                qc = lax.broadcasted_iota(jnp.int32, (tq, 2 * tq), 1)
                qc = jnp.where(qc >= tq, qc - tq, qc)
                s_ref[buf, hh] = jnp.where(qc >= kr, s_ref[buf, hh], NEG_INF)
            c = slope2[hh] * float((j - qb) * tq)
            _softmax_step_t(s_ref.at[buf, hh], c, None, vt, m_ref.at[slot, hh],
                            acc_ref.at[slot, hh])

    g = jnp.concatenate([g_ref[...]] * (tq // LANES), axis=1)

    def finalize(qb, slot):
        rows = pl.ds(qb * tq, tq)
        for hh in range(hp):
            a = acc_ref[slot, hh, :w] / acc_ref[slot, hh, w:w + 1]
            o = a[:, :tq] - lam * a[:, tq:]
            ms = jnp.mean(o * o, axis=0, keepdims=True)
            o = o * lax.rsqrt(ms + RMS_EPS) * g * out_scale
            o_ref[0, rows, hh * w:(hh + 1) * w] = o.T.astype(o_ref.dtype)

    _paired_causal_sweep(nqb, init, scores, finish, finalize)


def _diff_attention(proj, vt, slopes, lam, subln, out_scale, *, tq=256, hp=2):
    b, s, _ = proj.shape
    nh = DIFF_HEADS
    w = 2 * DIFF_HEAD_DIM
    tq = min(tq, s // 2)
    nqb = s // tq
    assert s % tq == 0 and tq % LANES == 0 and nh % hp == 0 and nqb % 2 == 0
    ng = nh // hp
    smem = pl.BlockSpec(memory_space=pltpu.SMEM)
    g = jnp.broadcast_to(subln.astype(F32)[:, None], (w, LANES))
    return pl.pallas_call(
        functools.partial(_diff_attn_kernel, tq=tq, hp=hp, nqb=nqb, out_scale=out_scale),
        grid=(b, ng),
        in_specs=[
            smem, smem,
            pl.BlockSpec((1, s, hp * w), lambda bi, h: (bi, 0, h)),
            pl.BlockSpec((1, s, hp * w), lambda bi, h: (bi, 0, ng + h)),
            pl.BlockSpec((1, hp * w, s), lambda bi, h: (bi, h, 0)),
            pl.BlockSpec((w, LANES), lambda bi, h: (0, 0)),
        ],
        out_specs=pl.BlockSpec((1, s, hp * w), lambda bi, h: (bi, 0, h)),
        out_shape=jax.ShapeDtypeStruct((b, s, MIX_WIDTH), BF16),
        scratch_shapes=[pltpu.VMEM((2, hp, 1, 2 * tq), F32),
                        pltpu.VMEM((2, hp, w + SUM_ROWS, 2 * tq), F32),
                        pltpu.VMEM((hp, tq, LANES), F32),
                        pltpu.VMEM((nqb, hp, 2 * tq, w), BF16),
                        pltpu.VMEM((2, hp, tq, 2 * tq), F32)],
        compiler_params=_params("parallel", "parallel"),
        name="diff_attn",
    )(slopes, lam, proj, proj, vt, g)


def _moba_attn_kernel(slopes_ref, q_ref, k_ref, vt_ref, o_ref, m_ref, acc_ref, bias_ref,
                      qs_ref, sel_ref, s_ref, *, nb, hp):
    hg = pl.program_id(1)
    blk = MOBA_BLOCK
    dh = MOBA_HEAD_DIM
    seq = nb * blk
    reps = blk // LANES

    krow = lax.broadcasted_iota(jnp.int32, (blk, LANES), 0).astype(F32)
    bidx = lax.broadcasted_iota(jnp.int32, (nb, seq), 0)
    qblk = lax.broadcasted_iota(jnp.int32, (nb, seq), 1) // blk
    slope2 = [slopes_ref[hg * hp + hh] * LOG2E for hh in range(hp)]
    for hh in range(hp):
        bias_ref[hh] = slope2[hh] * krow
        q = q_ref[0, :, hh * dh:(hh + 1) * dh]
        qs_ref[hh] = (q.astype(F32) * (dh ** -0.5 * LOG2E)).astype(BF16)

        kh = k_ref[0, :, hh * dh:(hh + 1) * dh].astype(F32)
        kmean = jnp.mean(kh.reshape(nb, blk, dh), axis=1)
        km_hi = kmean.astype(BF16)
        km_lo = (kmean - km_hi.astype(F32)).astype(BF16)
        gate = _dot_nt(km_hi, q) + _dot_nt(km_lo, q)

        rank = jnp.zeros((nb, seq), jnp.int32)
        for jp in range(nb):
            gj = gate[jp:jp + 1, :]
            before = (gj > gate) | ((gj == gate) & (jp < bidx))
            rank = rank + jnp.where(before, 1, 0) * jnp.where(jp < qblk, 1, 0)
        sel_ref[hh] = jnp.where((rank < MOBA_TOPK) & (bidx < qblk), 1.0, 0.0)

    def init():
        m_ref[...] = jnp.full_like(m_ref, NEG_INF)
        acc_ref[...] = jnp.zeros_like(acc_ref)

    def scores(qb, j, buf):
        keys = pl.ds(j * blk, blk)
        rows = pl.ds(qb * blk, blk)
        for hh in range(hp):
            k = k_ref[0, keys, hh * dh:(hh + 1) * dh]
            s_ref[buf, hh] = (_dot_nt(k, qs_ref[hh, rows, :])
                              + jnp.concatenate([bias_ref[hh]] * reps, axis=1))

    def finish(qb, j, slot, buf, diagonal):
        keys = pl.ds(j * blk, blk)
        cols = pl.ds(qb * blk, blk)
        for hh in range(hp):
            vt = vt_ref[0, hh * dh:(hh + 1) * dh, keys]
            c = slope2[hh] * float((j - qb) * blk)
            if diagonal:
                kr = lax.broadcasted_iota(jnp.int32, (blk, blk), 0)
                qc = lax.broadcasted_iota(jnp.int32, (blk, blk), 1)
                s_ref[buf, hh] = jnp.where(qc >= kr, s_ref[buf, hh], NEG_INF)
                ok = None
            else:
                tile_idx = lax.broadcasted_iota(jnp.int32, (nb, blk), 0)
                chosen = jnp.where(tile_idx == j, sel_ref[hh, :, cols], 0.0)
                ok = jnp.max(chosen, axis=0, keepdims=True) > 0.0
            _softmax_step_t(s_ref.at[buf, hh], c, ok, vt, m_ref.at[slot, hh],
                            acc_ref.at[slot, hh])

    def finalize(qb, slot):
        rows = pl.ds(qb * blk, blk)
        for hh in range(hp):
            o = acc_ref[slot, hh, :dh] / acc_ref[slot, hh, dh:dh + 1]
            o_ref[0, rows, hh * dh:(hh + 1) * dh] = o.T.astype(o_ref.dtype)

    _paired_causal_sweep(nb, init, scores, finish, finalize)


def _moba_attention(proj, vt, slopes, *, hp=4):
    b, s, _ = proj.shape
    nh = MOBA_HEADS
    dh = MOBA_HEAD_DIM
    blk = MOBA_BLOCK
    nb = s // blk
    assert s % blk == 0 and nh % hp == 0 and nb % 2 == 0
    ng = nh // hp
    smem = pl.BlockSpec(memory_space=pltpu.SMEM)
    return pl.pallas_call(
        functools.partial(_moba_attn_kernel, nb=nb, hp=hp),
        grid=(b, ng),
        in_specs=[
            smem,
            pl.BlockSpec((1, s, hp * dh), lambda bi, h: (bi, 0, h)),
            pl.BlockSpec((1, s, hp * dh), lambda bi, h: (bi, 0, ng + h)),
            pl.BlockSpec((1, hp * dh, s), lambda bi, h: (bi, h, 0)),
        ],
        out_specs=pl.BlockSpec((1, s, hp * dh), lambda bi, h: (bi, 0, h)),
        out_shape=jax.ShapeDtypeStruct((b, s, MIX_WIDTH), BF16),
        scratch_shapes=[pltpu.VMEM((2, hp, 1, blk), F32),
                        pltpu.VMEM((2, hp, dh + SUM_ROWS, blk), F32),
                        pltpu.VMEM((hp, blk, LANES), F32),
                        pltpu.VMEM((hp, s, dh), BF16),
                        pltpu.VMEM((hp, nb, s), F32),
                        pltpu.VMEM((2, hp, blk, blk), F32)],
        compiler_params=_params("parallel", "parallel"),
        name="moba_attn",
    )(slopes, proj, proj, vt)


def _swa_attn_kernel(slopes_ref, sinks_ref, q_ref, k_ref, v_ref, o_ref, bm_ref):
    i = pl.program_id(1)
    win = SWA_WINDOW
    dh = SWA_HEAD_DIM
    grp = SWA_Q_HEADS // SWA_KV_HEADS
    tq = win
    top = lax.broadcasted_iota(jnp.int32, (2 * tq, 1), 0) < tq

    @pl.when(i == 0)
    def _():
        qi = lax.broadcasted_iota(jnp.int32, (2 * tq, 2 * win), 0) % tq
        dist0 = qi - lax.broadcasted_iota(jnp.int32, (2 * tq, 2 * win), 1)
        for layout, dist in enumerate((dist0, dist0 + win)):
            valid = (dist >= 0) & (dist < win)
            distf = dist.astype(F32)
            for pair in range(SWA_Q_HEADS // 2):
                slope = jnp.where(top, slopes_ref[2 * pair], slopes_ref[2 * pair + 1])
                bm_ref[layout, pair] = jnp.where(valid, -slope * distf, NEG_INF)

    kstart = pl.multiple_of(jnp.maximum(i - 1, 0) * win, win)
    layout = jnp.where(i == 0, 0, 1)
    lane = lax.broadcasted_iota(jnp.int32, (tq, 2 * dh), 1)

    for kvh in range(SWA_KV_HEADS):
        k = k_ref[0, pl.ds(kstart, 2 * win), kvh * 2 * dh:(kvh + 1) * 2 * dh]
        v = v_ref[0, pl.ds(kstart, 2 * win), kvh * 2 * dh:(kvh + 1) * 2 * dh]
        for pair in range(grp // 2):
            h0 = kvh * grp + 2 * pair
            q = q_ref[0, :, h0 * dh:(h0 + 2) * dh].astype(F32) * (dh ** -0.5)
            qq = jnp.concatenate([jnp.where(lane < dh, q, 0.0), jnp.where(lane >= dh, q, 0.0)],
                                 axis=0).astype(BF16)
            sink = jnp.where(top, sinks_ref[h0], sinks_ref[h0 + 1])
            s = _dot_nt(qq, k) + bm_ref[layout, h0 // 2]
            m = jnp.maximum(jnp.max(s, axis=-1, keepdims=True), sink)
            p = jnp.exp(s - m)
            denom = jnp.sum(p, axis=-1, keepdims=True) + jnp.exp(sink - m)
            o2 = jnp.dot(p.astype(BF16), v, preferred_element_type=F32) / denom
            o = jnp.where(lane < dh, o2[:tq], o2[tq:])
            o_ref[0, :, h0 * dh:(h0 + 2) * dh] = o.astype(o_ref.dtype)


def _swa_attention(proj, slopes, sinks):
    b, s, _ = proj.shape
    win = SWA_WINDOW
    e = MIX_WIDTH
    kvw = SWA_KV_HEADS * 2 * SWA_HEAD_DIM
    assert s % win == 0 and s >= 2 * win
    smem = pl.BlockSpec(memory_space=pltpu.SMEM)
    return pl.pallas_call(
        _swa_attn_kernel,
        grid=(b, s // win),
        in_specs=[
            smem, smem,
            pl.BlockSpec((1, win, e), lambda bi, i: (bi, i, 0)),
            pl.BlockSpec((1, s, kvw), lambda bi, i: (bi, 0, 2 * e // kvw)),
            pl.BlockSpec((1, s, kvw), lambda bi, i: (bi, 0, 2 * e // kvw + 1)),
        ],
        out_specs=pl.BlockSpec((1, win, e), lambda bi, i: (bi, i, 0)),
        out_shape=jax.ShapeDtypeStruct((b, s, e), BF16),
        scratch_shapes=[pltpu.VMEM((2, SWA_Q_HEADS // 2, 2 * win, 2 * win), F32)],
        compiler_params=_params("parallel", "arbitrary"),
        name="swa_attn",
    )(slopes, sinks, proj, proj, proj)


def _s5_ssm_kernel(u_ref, perm_ref, bb_ref, a_ref, cc_ref, d_ref, y_ref, st_ref, bu_ref, xb_ref,
                   *, bsz, tt, chunk_lanes):
    ns = S5_SLAB_STATE
    cw = chunk_lanes

    @pl.when(pl.program_id(1) == 0)
    def _():
        st_ref[...] = jnp.zeros_like(st_ref)

    tp = PERM_STEPS
    u32 = jnp.concatenate(
        [jnp.dot(perm_ref[0], u_ref[:, g * tp:(g + 1) * tp, :].reshape(bsz * tp, S5_SLAB),
                 preferred_element_type=F32) for g in range(tt // tp)], axis=0)
    u = u32.astype(BF16)
    y = d_ref[0] * u32
    for c in range(ns // cw):
        lo = c * cw
        bu_ref[2 * c] = jnp.dot(u, bb_ref[0, :, lo:lo + cw], preferred_element_type=F32)
        bu_ref[2 * c + 1] = jnp.dot(u, bb_ref[0, :, ns + lo:ns + lo + cw],
                                    preferred_element_type=F32)
        ar = jnp.broadcast_to(a_ref[0, 0:1, lo:lo + cw], (bsz, cw))
        ai = jnp.broadcast_to(a_ref[0, 1:2, lo:lo + cw], (bsz, cw))
        xr = st_ref[0, :, lo:lo + cw]
        xi = st_ref[1, :, lo:lo + cw]
        for t in range(tt):
            rows = slice(t * bsz, (t + 1) * bsz)
            xr, xi = (ar * xr - ai * xi + bu_ref[2 * c, rows, :],
                      ar * xi + ai * xr + bu_ref[2 * c + 1, rows, :])
            xb_ref[2 * c, rows, :] = xr.astype(BF16)
            xb_ref[2 * c + 1, rows, :] = xi.astype(BF16)
        st_ref[0, :, lo:lo + cw] = xr
        st_ref[1, :, lo:lo + cw] = xi
        y = (y + jnp.dot(xb_ref[2 * c], cc_ref[0, lo:lo + cw], preferred_element_type=F32)
             + jnp.dot(xb_ref[2 * c + 1], cc_ref[0, ns + lo:ns + lo + cw],
                       preferred_element_type=F32))
    y = y.astype(BF16)
    for g in range(tt // tp):
        y_bt = jnp.dot(perm_ref[1], y[g * bsz * tp:(g + 1) * bsz * tp],
                       preferred_element_type=F32)
        y_ref[:, g * tp:(g + 1) * tp, :] = y_bt.reshape(bsz, tp, S5_SLAB).astype(y_ref.dtype)


def _s5_ssm(proj, bd_b, a_bar, bd_c, d_skip, *, tt=128, chunk_lanes=256):
    b, s, _ = proj.shape
    tt = min(tt, s)
    assert s % tt == 0 and tt % PERM_STEPS == 0 and b % 16 == 0
    n_slab = MIX_WIDTH // S5_SLAB
    ns = S5_SLAB_STATE
    n_buf = 2 * ns // chunk_lanes
    rows = b * tt
    prow = jnp.arange(b * PERM_STEPS)
    fwd = jax.nn.one_hot((prow % b) * PERM_STEPS + prow // b, b * PERM_STEPS, dtype=BF16)
    perm = jnp.stack([fwd, fwd.T])
    return pl.pallas_call(
        functools.partial(_s5_ssm_kernel, bsz=b, tt=tt, chunk_lanes=chunk_lanes),
        grid=(n_slab, s // tt),
        in_specs=[
            pl.BlockSpec((b, tt, S5_SLAB), lambda sl, t: (0, t, sl)),
            pl.BlockSpec((2, b * PERM_STEPS, b * PERM_STEPS), lambda sl, t: (0, 0, 0)),
            pl.BlockSpec((1, S5_SLAB, 2 * ns), lambda sl, t: (sl, 0, 0)),
            pl.BlockSpec((1, 2, ns), lambda sl, t: (sl, 0, 0)),
            pl.BlockSpec((1, 2 * ns, S5_SLAB), lambda sl, t: (sl, 0, 0)),
            pl.BlockSpec((1, 1, S5_SLAB), lambda sl, t: (sl, 0, 0)),
        ],
        out_specs=pl.BlockSpec((b, tt, S5_SLAB), lambda sl, t: (0, t, sl)),
        out_shape=jax.ShapeDtypeStruct((b, s, MIX_WIDTH), BF16),
        scratch_shapes=[pltpu.VMEM((2, b, ns), F32),
                        pltpu.VMEM((n_buf, rows, chunk_lanes), F32),
                        pltpu.VMEM((n_buf, rows, chunk_lanes), BF16)],
        compiler_params=_params("parallel", "arbitrary"),
        name="s5_ssm",
    )(proj, perm, bd_b, a_bar, bd_c, d_skip)


def _s5_discretize(a_re, a_im, log_dt, b_re, b_im, c_re, c_im, d_skip):
    lr, li = a_re.astype(F32), a_im.astype(F32)
    dt = jnp.exp(log_dt.astype(F32))[:, None]
    mag = jnp.exp(lr * dt)
    ab_re, ab_im = mag * jnp.cos(li * dt), mag * jnp.sin(li * dt)
    den = lr * lr + li * li
    f_re = ((ab_re - 1.0) * lr + ab_im * li) / den
    f_im = (ab_im * lr - (ab_re - 1.0) * li) / den
    br, bi = b_re.astype(F32), b_im.astype(F32)
    bb_re = f_re[..., None] * br - f_im[..., None] * bi
    bb_im = f_re[..., None] * bi + f_im[..., None] * br
    n_slab = MIX_WIDTH // S5_SLAB
    gs = S5_SLAB // S5_GROUP
    eye = jnp.eye(gs, dtype=F32)

    def pack_b(m):
        m = m.reshape(n_slab, gs, S5_STATE, S5_GROUP)
        return jnp.einsum('sgpc,gh->sgchp', m, eye).reshape(n_slab, S5_SLAB, S5_SLAB_STATE)

    def pack_c(m):
        m = m.reshape(n_slab, gs, S5_GROUP, S5_STATE)
        return jnp.einsum('sgcp,gh->shpgc', m, eye).reshape(n_slab, S5_SLAB_STATE, S5_SLAB)

    bd_b = jnp.concatenate([pack_b(bb_re), pack_b(bb_im)], axis=2).astype(BF16)
    bd_c = jnp.concatenate([pack_c(c_re.astype(F32)), pack_c(-c_im.astype(F32))], axis=1).astype(BF16)
    a_bar = jnp.stack([ab_re.reshape(n_slab, S5_SLAB_STATE), ab_im.reshape(n_slab, S5_SLAB_STATE)],
                      axis=1)
    return bd_b, a_bar, bd_c, d_skip.astype(F32).reshape(n_slab, 1, S5_SLAB)


def _glu_kernel(y_ref, w_ref, b_ref, o_ref, *, chunks):
    cm = y_ref.shape[0] // chunks
    for c in range(chunks):
        rows = slice(c * cm, (c + 1) * cm)
        g = jax.nn.gelu(y_ref[rows, :].astype(F32))
        r = jnp.dot(g.astype(BF16), w_ref[...], preferred_element_type=F32) + b_ref[...]
        o_ref[rows, :] = (g * _sigmoid(r)).astype(o_ref.dtype)


def _glu(y2d, w, bias, *, tm=512, chunks=4):
    t, e = y2d.shape
    tm = min(tm, t)
    assert t % tm == 0 and tm % chunks == 0
    return pl.pallas_call(
        functools.partial(_glu_kernel, chunks=chunks),
        grid=(t // tm,),
        in_specs=[
            pl.BlockSpec((tm, e), lambda i: (i, 0)),
            pl.BlockSpec((e, e), lambda i: (0, 0), pipeline_mode=pl.Buffered(1)),
            pl.BlockSpec((1, e), lambda i: (0, 0)),
        ],
        out_specs=pl.BlockSpec((tm, e), lambda i: (i, 0)),
        out_shape=jax.ShapeDtypeStruct((t, e), BF16),
        compiler_params=_params("parallel"),
        name="s5_glu",
    )(y2d, w, bias.reshape(1, e).astype(F32))


def _s5_layer(x2d, h2d, bsz, seq, post_g, next_g, w_in, a_re, a_im, log_dt, b_re, b_im, c_re,
              c_im, d_skip, w_glu, b_glu, w_out):
    e = MIX_WIDTH
    proj = _in_proj(h2d, w_in.astype(BF16))
    bd_b, a_bar, bd_c, d3 = _s5_discretize(a_re, a_im, log_dt, b_re, b_im, c_re, c_im, d_skip)
    y = _s5_ssm(proj.reshape(bsz, seq, 2 * e), bd_b, a_bar, bd_c, d3)
    y = _glu(y.reshape(bsz * seq, e), w_glu.astype(BF16), b_glu)
    return _gate_out(y, proj, 1, x2d, w_out.astype(BF16), post_g, next_g)


def _attn_projections(h2d, seq, w_in):
    w = w_in.astype(BF16)
    proj = _in_proj(h2d, w, skip_block=2)
    vt = _in_proj_t(h2d, w, 2, seq)
    return proj, vt


def _diff_layer(x2d, h2d, bsz, seq, layer_idx, post_g, next_g, w_in, lq1, lk1, lq2, lk2, subln,
                w_out):
    e = MIX_WIDTH
    proj, vt = _attn_projections(h2d, seq, w_in)
    lam_init = 0.8 - 0.6 * math.exp(-0.3 * layer_idx)
    lam = (jnp.exp(jnp.sum(lq1.astype(F32) * lk1.astype(F32)))
           - jnp.exp(jnp.sum(lq2.astype(F32) * lk2.astype(F32))) + lam_init).reshape(1)
    o = _diff_attention(proj.reshape(bsz, seq, 3 * e), vt, _alibi_slopes(DIFF_HEADS), lam, subln,
                        1.0 - lam_init)
    return _gate_out(o.reshape(bsz * seq, e), proj, 2, x2d, w_out.astype(BF16), post_g, next_g)


def _moba_layer(x2d, h2d, bsz, seq, post_g, next_g, w_in, w_out):
    e = MIX_WIDTH
    proj, vt = _attn_projections(h2d, seq, w_in)
    o = _moba_attention(proj.reshape(bsz, seq, 3 * e), vt, _alibi_slopes(MOBA_HEADS))
    return _gate_out(o.reshape(bsz * seq, e), proj, 2, x2d, w_out.astype(BF16), post_g, next_g)


def _swa_layer(x2d, h2d, bsz, seq, post_g, next_g, w_in, sinks, w_out):
    e = MIX_WIDTH
    dh = SWA_HEAD_DIM
    nqc, nkc = SWA_Q_HEADS * dh, SWA_KV_HEADS * dh
    wq, wk = w_in[:, :nqc], w_in[:, nqc:nqc + nkc]
    wv, wz = w_in[:, nqc + nkc:nqc + 2 * nkc], w_in[:, nqc + 2 * nkc:]

    def dup(w):
        w = w.reshape(-1, SWA_KV_HEADS, 1, dh)
        return jnp.broadcast_to(w, (w.shape[0], SWA_KV_HEADS, 2, dh)).reshape(-1, 2 * nkc)

    w_cat = jnp.concatenate([wq, wz, dup(wk), dup(wv)], axis=1).astype(BF16)
    proj = _in_proj(h2d, w_cat)
    o = _swa_attention(proj.reshape(bsz, seq, w_cat.shape[1]), _alibi_slopes(SWA_Q_HEADS),
                       sinks.astype(F32))
    return _gate_out(o.reshape(bsz * seq, e), proj, 1, x2d, w_out.astype(BF16), post_g, next_g)


def kernel(x, pre_norm, post_norm, s5_w_in, s5_a_re, s5_a_im, s5_log_dt, s5_b_re, s5_b_im, s5_c_re, s5_c_im, s5_d, s5_w_glu, s5_b_glu, s5_w_out, diff_w_in, diff_lq1, diff_lk1, diff_lq2, diff_lk2, diff_subln, diff_w_out, moba_w_in, moba_w_out, swa_w_in, swa_sinks, swa_w_out):
    bsz, seq, d = x.shape
    depth = pre_norm.shape[0]
    x2d = x.reshape(bsz * seq, d)
    h2d = _prenorm(x2d, pre_norm[0])
    for i in range(depth):
        kind, j = i % 4, i // 4
        post_g = post_norm[i]
        next_g = pre_norm[i + 1] if i + 1 < depth else None
        if kind == 0:
            x2d, h2d = _s5_layer(x2d, h2d, bsz, seq, post_g, next_g, s5_w_in[j], s5_a_re[j],
                                 s5_a_im[j], s5_log_dt[j], s5_b_re[j], s5_b_im[j], s5_c_re[j],
                                 s5_c_im[j], s5_d[j], s5_w_glu[j], s5_b_glu[j], s5_w_out[j])
        elif kind == 1:
            x2d, h2d = _diff_layer(x2d, h2d, bsz, seq, i, post_g, next_g, diff_w_in[j],
                                   diff_lq1[j], diff_lk1[j], diff_lq2[j], diff_lk2[j],
                                   diff_subln[j], diff_w_out[j])
        elif kind == 2:
            x2d, h2d = _moba_layer(x2d, h2d, bsz, seq, post_g, next_g, moba_w_in[j],
                                   moba_w_out[j])
        else:
            x2d, h2d = _swa_layer(x2d, h2d, bsz, seq, post_g, next_g, swa_w_in[j], swa_sinks[j],
                                  swa_w_out[j])
    return x2d.reshape(bsz, seq, d)
```

```python
import functools
import math

import jax
import jax.numpy as jnp
from jax import lax
from jax.experimental import pallas as pl
from jax.experimental.pallas import tpu as pltpu

F32 = jnp.float32
BF16 = jnp.bfloat16

D_MODEL = 2048
MIX_WIDTH = D_MODEL
RMS_EPS = 1e-6
NEG_INF = -1e30
LOG2E = math.log2(math.e)
LANES = 128

S5_GROUP = 16
S5_GROUPS = MIX_WIDTH // S5_GROUP
S5_STATE = 64
S5_SLAB = 256
S5_SLAB_STATE = (S5_SLAB // S5_GROUP) * S5_STATE
PERM_STEPS = 16

DIFF_HEADS = 16
DIFF_HEAD_DIM = 64
MOBA_HEADS = 16
MOBA_HEAD_DIM = 128
MOBA_BLOCK = 256
MOBA_TOPK = 3
SWA_Q_HEADS = 32
SWA_KV_HEADS = 4
SWA_HEAD_DIM = 64
SWA_WINDOW = 128

VMEM_LIMIT = 56 * 1024 * 1024


def _params(*sem):
    return pltpu.CompilerParams(dimension_semantics=sem, vmem_limit_bytes=VMEM_LIMIT)


def _alibi_slopes(n_heads):
    return 2.0 ** (-8.0 * jnp.arange(1, n_heads + 1, dtype=F32) / n_heads)


def _dot_nt(a, b):
    return lax.dot_general(a, b, (((1,), (1,)), ((), ())), preferred_element_type=F32)


def _sigmoid(x):
    return 0.5 * jnp.tanh(0.5 * x) + 0.5


def _rms_normalize(x, gain):
    ms = jnp.mean(x * x, axis=-1, keepdims=True)
    return x * lax.rsqrt(ms + RMS_EPS) * gain


def _prenorm_kernel(x_ref, g_ref, h_ref):
    h_ref[...] = _rms_normalize(x_ref[...], g_ref[...]).astype(BF16)


def _prenorm(x2d, gain, *, tm=512):
    t, d = x2d.shape
    tm = min(tm, t)
    assert t % tm == 0
    return pl.pallas_call(
        _prenorm_kernel,
        grid=(t // tm,),
        in_specs=[pl.BlockSpec((tm, d), lambda i: (i, 0)), pl.BlockSpec((1, d), lambda i: (0, 0))],
        out_specs=pl.BlockSpec((tm, d), lambda i: (i, 0)),
        out_shape=jax.ShapeDtypeStruct((t, d), BF16),
        compiler_params=_params("parallel"),
        name="prenorm",
    )(x2d, gain.reshape(1, d).astype(F32))


def _col_tile(n, cap=2048):
    return max(t for t in range(256, cap + 1, 256) if n % t == 0)


def _matmul_kernel(h_ref, w_ref, o_ref):
    o_ref[...] = jnp.dot(h_ref[...], w_ref[...], preferred_element_type=F32).astype(o_ref.dtype)


def _in_proj(h2d, w, *, skip_block=None, tm=1024):
    t, d = h2d.shape
    tm = min(tm, t)
    tn = _col_tile(w.shape[1])
    assert t % tm == 0
    if skip_block is None:
        n = w.shape[1]
        w_map = lambda j, i: (0, j)
    else:
        n = w.shape[1] - tn
        w_map = lambda j, i: (0, j + (j >= skip_block).astype(jnp.int32))
    return pl.pallas_call(
        _matmul_kernel,
        grid=(n // tn, t // tm),
        in_specs=[
            pl.BlockSpec((tm, d), lambda j, i: (i, 0)),
            pl.BlockSpec((d, tn), w_map),
        ],
        out_specs=pl.BlockSpec((tm, tn), lambda j, i: (i, j)),
        out_shape=jax.ShapeDtypeStruct((t, n), BF16),
        compiler_params=_params("parallel", "parallel"),
        name="in_proj",
    )(h2d, w)


def _matmul_t_kernel(h_ref, w_ref, o_ref, wt_ref):
    @pl.when(pl.program_id(1) == 0)
    def _():
        wt_ref[...] = w_ref[...].T

    o_ref[0] = _dot_nt(wt_ref[...], h_ref[...]).astype(o_ref.dtype)


def _in_proj_t(h2d, w, col_block, seq, *, tm=1024):
    t, d = h2d.shape
    tm = min(tm, seq)
    tn = _col_tile(w.shape[1])
    assert seq % tm == 0
    tps = seq // tm
    return pl.pallas_call(
        _matmul_t_kernel,
        grid=(1, t // tm),
        in_specs=[
            pl.BlockSpec((tm, d), lambda j, i: (i, 0)),
            pl.BlockSpec((d, tn), lambda j, i: (0, col_block)),
        ],
        out_specs=pl.BlockSpec((1, tn, tm), lambda j, i: (i // tps, 0, i % tps)),
        out_shape=jax.ShapeDtypeStruct((t // seq, tn, seq), BF16),
        scratch_shapes=[pltpu.VMEM((tn, d), BF16)],
        compiler_params=_params("parallel", "arbitrary"),
        name="in_proj_t",
    )(h2d, w)


def _gate_out_kernel(o_ref, z_ref, x_ref, w_ref, g_ref, gn_ref, out_ref, *h_ref, chunks):
    cm = o_ref.shape[0] // chunks
    for c in range(chunks):
        rows = slice(c * cm, (c + 1) * cm)
        hz = z_ref[rows, :] * 0.5
        y = o_ref[rows, :] * (hz * jnp.tanh(hz) + hz)
        r = jnp.dot(y, w_ref[...], preferred_element_type=F32)
        xn = x_ref[rows, :] + _rms_normalize(r, g_ref[...])
        out_ref[rows, :] = xn
        if h_ref:
            h_ref[0][rows, :] = _rms_normalize(xn, gn_ref[...]).astype(BF16)


def _gate_out(o2d, zsrc, z_block, x2d, w, gain, next_gain, *, tm=512, chunks=4):
    t, e = o2d.shape
    d = w.shape[1]
    tm = min(tm, t)
    assert t % tm == 0 and tm % chunks == 0
    row_map = lambda i: (i, 0)
    const = lambda i: (0, 0)
    z_map = lambda i: (i, z_block)
    with_h = next_gain is not None
    gn = (next_gain if with_h else gain).reshape(1, d).astype(F32)
    out_specs = [pl.BlockSpec((tm, d), row_map)]
    out_shape = [jax.ShapeDtypeStruct((t, d), F32)]
    if with_h:
        out_specs.append(pl.BlockSpec((tm, d), row_map))
        out_shape.append(jax.ShapeDtypeStruct((t, d), BF16))
    res = pl.pallas_call(
        functools.partial(_gate_out_kernel, chunks=chunks),
        grid=(t // tm,),
        in_specs=[
            pl.BlockSpec((tm, e), row_map),
            pl.BlockSpec((tm, e), z_map),
            pl.BlockSpec((tm, d), row_map),
            pl.BlockSpec((e, d), const, pipeline_mode=pl.Buffered(1)),
            pl.BlockSpec((1, d), const),
            pl.BlockSpec((1, d), const),
        ],
        out_specs=out_specs,
        out_shape=out_shape,
        compiler_params=_params("parallel"),
        name="gate_out",
    )(o2d, zsrc, x2d, w, gain.reshape(1, d).astype(F32), gn)
    return (res[0], res[1]) if with_h else (res[0], None)


BIG = 1e30
SUM_ROWS = 16


def _softmax_step_t(s_ref, c, ok, vt, m_ref, acc_ref):
    m_old = m_ref[...]
    mt = jnp.max(s_ref[...], axis=0, keepdims=True) + c
    if ok is not None:
        mt = jnp.where(ok, mt, NEG_INF)
    m_new = jnp.maximum(m_old, mt)
    alpha = jnp.exp2(m_old - m_new)
    r = m_new - c
    if ok is not None:
        r = jnp.where(ok, r, BIG)
    p = jnp.exp2(s_ref[...] - r).astype(BF16)
    vt1 = jnp.concatenate([vt, jnp.ones((SUM_ROWS, vt.shape[1]), BF16)], axis=0)
    acc_ref[...] = alpha * acc_ref[...] + jnp.dot(vt1, p, preferred_element_type=F32)
    m_ref[...] = m_new


def _paired_causal_sweep(n_blocks, init, scores, finish, finalize):
    n_past = n_blocks - 1

    for i in range(n_blocks // 2):
        blk_a, blk_b = i, n_blocks - 1 - i
        init()
        tiles = []
        for p in range(n_past):
            tiles.append((blk_a, p, 0, False) if p < i else (blk_b, p - i, 1, False))
        tiles += [(blk_a, blk_a, 0, True), (blk_b, blk_b, 1, True)]
        scores(tiles[0][0], tiles[0][1], 0)
        for p, (qb, j, slot, diagonal) in enumerate(tiles):
            if p + 1 < len(tiles):
                scores(tiles[p + 1][0], tiles[p + 1][1], (p + 1) % 2)
            finish(qb, j, slot, p % 2, diagonal)
        finalize(blk_a, 0)
        finalize(blk_b, 1)


def _diff_attn_kernel(slopes_ref, lam_ref, q_ref, k_ref, vt_ref, g_ref, o_ref,
                      m_ref, acc_ref, bias_ref, qq_ref, s_ref, *, tq, hp, nqb, out_scale):
    hg = pl.program_id(1)
    lam = lam_ref[0]
    d = DIFF_HEAD_DIM
    w = 2 * d
    reps = 2 * tq // LANES

    lane = lax.broadcasted_iota(jnp.int32, (tq, w), 1)
    krow = lax.broadcasted_iota(jnp.int32, (tq, LANES), 0).astype(F32)
    slope2 = [slopes_ref[hg * hp + hh] * LOG2E for hh in range(hp)]
    for hh in range(hp):
        bias_ref[hh] = slope2[hh] * krow

    def prep(qb, carry):
        rows = pl.ds(pl.multiple_of(qb * tq, tq), tq)
        for hh in range(hp):
            q = q_ref[0, rows, hh * w:(hh + 1) * w].astype(F32) * (d ** -0.5 * LOG2E)
            qq_ref[qb, hh, :tq] = jnp.where(lane < d, q, 0.0).astype(BF16)
            qq_ref[qb, hh, tq:] = jnp.where(lane >= d, q, 0.0).astype(BF16)
        return carry

    lax.fori_loop(0, nqb, prep, 0)

    def init():
        m_ref[...] = jnp.full_like(m_ref, NEG_INF)
        acc_ref[...] = jnp.zeros_like(acc_ref)

    def scores(qb, j, buf):
        keys = pl.ds(j * tq, tq)
        for hh in range(hp):
            k = k_ref[0, keys, hh * w:(hh + 1) * w]
            s_ref[buf, hh] = (_dot_nt(k, qq_ref[qb, hh])
                              + jnp.concatenate([bias_ref[hh]] * reps, axis=1))

    def finish(qb, j, slot, buf, diagonal):
        keys = pl.ds(j * tq, tq)
        for hh in range(hp):
            vt = vt_ref[0, hh * w:(hh + 1) * w, keys]
            if diagonal:
                kr = lax.broadcasted_iota(jnp.int32, (tq, 2 * tq), 0)
                qc = lax.broadcasted_iota(jnp.int32, (tq, 2 * tq), 1)
                qc = jnp.where(qc >= tq, qc - tq, qc)
                s_ref[buf, hh] = jnp.where(qc >= kr, s_ref[buf, hh], NEG_INF)
            c = slope2[hh] * float((j - qb) * tq)
            _softmax_step_t(s_ref.at[buf, hh], c, None, vt, m_ref.at[slot, hh],
                            acc_ref.at[slot, hh])

    g = jnp.concatenate([g_ref[...]] * (tq // LANES), axis=1)

    def finalize(qb, slot):
        rows = pl.ds(qb * tq, tq)
        for hh in range(hp):
            a = acc_ref[slot, hh, :w] / acc_ref[slot, hh, w:w + 1]
            o = a[:, :tq] - lam * a[:, tq:]
            ms = jnp.mean(o * o, axis=0, keepdims=True)
            o = o * lax.rsqrt(ms + RMS_EPS) * g * out_scale
            o_ref[0, rows, hh * w:(hh + 1) * w] = o.T.astype(o_ref.dtype)

    _paired_causal_sweep(nqb, init, scores, finish, finalize)


def _diff_attention(proj, vt, slopes, lam, subln, out_scale, *, tq=256, hp=2):
    b, s, _ = proj.shape
    nh = DIFF_HEADS
    w = 2 * DIFF_HEAD_DIM
    tq = min(tq, s // 2)
    nqb = s // tq
    assert s % tq == 0 and tq % LANES == 0 and nh % hp == 0 and nqb % 2 == 0
    ng = nh // hp
    smem = pl.BlockSpec(memory_space=pltpu.SMEM)
    g = jnp.broadcast_to(subln.astype(F32)[:, None], (w, LANES))
    return pl.pallas_call(
        functools.partial(_diff_attn_kernel, tq=tq, hp=hp, nqb=nqb, out_scale=out_scale),
        grid=(b, ng),
        in_specs=[
            smem, smem,
            pl.BlockSpec((1, s, hp * w), lambda bi, h: (bi, 0, h)),
            pl.BlockSpec((1, s, hp * w), lambda bi, h: (bi, 0, ng + h)),
            pl.BlockSpec((1, hp * w, s), lambda bi, h: (bi, h, 0)),
            pl.BlockSpec((w, LANES), lambda bi, h: (0, 0)),
        ],
        out_specs=pl.BlockSpec((1, s, hp * w), lambda bi, h: (bi, 0, h)),
        out_shape=jax.ShapeDtypeStruct((b, s, MIX_WIDTH), BF16),
        scratch_shapes=[pltpu.VMEM((2, hp, 1, 2 * tq), F32),
                        pltpu.VMEM((2, hp, w + SUM_ROWS, 2 * tq), F32),
                        pltpu.VMEM((hp, tq, LANES), F32),
                        pltpu.VMEM((nqb, hp, 2 * tq, w), BF16),
                        pltpu.VMEM((2, hp, tq, 2 * tq), F32)],
        compiler_params=_params("parallel", "parallel"),
        name="diff_attn",
    )(slopes, lam, proj, proj, vt, g)


def _moba_attn_kernel(slopes_ref, q_ref, k_ref, vt_ref, o_ref, m_ref, acc_ref, bias_ref,
                      qs_ref, sel_ref, s_ref, *, nb, hp):
    hg = pl.program_id(1)
    blk = MOBA_BLOCK
    dh = MOBA_HEAD_DIM
    seq = nb * blk
    reps = blk // LANES

    krow = lax.broadcasted_iota(jnp.int32, (blk, LANES), 0).astype(F32)
    bidx = lax.broadcasted_iota(jnp.int32, (nb, seq), 0)
    qblk = lax.broadcasted_iota(jnp.int32, (nb, seq), 1) // blk
    slope2 = [slopes_ref[hg * hp + hh] * LOG2E for hh in range(hp)]
    for hh in range(hp):
        bias_ref[hh] = slope2[hh] * krow
        q = q_ref[0, :, hh * dh:(hh + 1) * dh]
        qs_ref[hh] = (q.astype(F32) * (dh ** -0.5 * LOG2E)).astype(BF16)

        kh = k_ref[0, :, hh * dh:(hh + 1) * dh].astype(F32)
        kmean = jnp.mean(kh.reshape(nb, blk, dh), axis=1)
        km_hi = kmean.astype(BF16)
        km_lo = (kmean - km_hi.astype(F32)).astype(BF16)
        gate = _dot_nt(km_hi, q) + _dot_nt(km_lo, q)

        rank = jnp.zeros((nb, seq), jnp.int32)
        for jp in range(nb):
            gj = gate[jp:jp + 1, :]
            before = (gj > gate) | ((gj == gate) & (jp < bidx))
            rank = rank + jnp.where(before, 1, 0) * jnp.where(jp < qblk, 1, 0)
        sel_ref[hh] = jnp.where((rank < MOBA_TOPK) & (bidx < qblk), 1.0, 0.0)

    def init():
        m_ref[...] = jnp.full_like(m_ref, NEG_INF)
        acc_ref[...] = jnp.zeros_like(acc_ref)

    def scores(qb, j, buf):
        keys = pl.ds(j * blk, blk)
        rows = pl.ds(qb * blk, blk)
        for hh in range(hp):
            k = k_ref[0, keys, hh * dh:(hh + 1) * dh]
            s_ref[buf, hh] = (_dot_nt(k, qs_ref[hh, rows, :])
                              + jnp.concatenate([bias_ref[hh]] * reps, axis=1))

    def finish(qb, j, slot, buf, diagonal):
        keys = pl.ds(j * blk, blk)
        cols = pl.ds(qb * blk, blk)
        for hh in range(hp):
            vt = vt_ref[0, hh * dh:(hh + 1) * dh, keys]
            c = slope2[hh] * float((j - qb) * blk)
            if diagonal:
                kr = lax.broadcasted_iota(jnp.int32, (blk, blk), 0)
                qc = lax.broadcasted_iota(jnp.int32, (blk, blk), 1)
                s_ref[buf, hh] = jnp.where(qc >= kr, s_ref[buf, hh], NEG_INF)
                ok = None
            else:
                tile_idx = lax.broadcasted_iota(jnp.int32, (nb, blk), 0)
                chosen = jnp.where(tile_idx == j, sel_ref[hh, :, cols], 0.0)
                ok = jnp.max(chosen, axis=0, keepdims=True) > 0.0
            _softmax_step_t(s_ref.at[buf, hh], c, ok, vt, m_ref.at[slot, hh],
                            acc_ref.at[slot, hh])

    def finalize(qb, slot):
        rows = pl.ds(qb * blk, blk)
        for hh in range(hp):
            o = acc_ref[slot, hh, :dh] / acc_ref[slot, hh, dh:dh + 1]
            o_ref[0, rows, hh * dh:(hh + 1) * dh] = o.T.astype(o_ref.dtype)

    _paired_causal_sweep(nb, init, scores, finish, finalize)


def _moba_attention(proj, vt, slopes, *, hp=4):
    b, s, _ = proj.shape
    nh = MOBA_HEADS
    dh = MOBA_HEAD_DIM
    blk = MOBA_BLOCK
    nb = s // blk
    assert s % blk == 0 and nh % hp == 0 and nb % 2 == 0
    ng = nh // hp
    smem = pl.BlockSpec(memory_space=pltpu.SMEM)
    return pl.pallas_call(
        functools.partial(_moba_attn_kernel, nb=nb, hp=hp),
        grid=(b, ng),
        in_specs=[
            smem,
            pl.BlockSpec((1, s, hp * dh), lambda bi, h: (bi, 0, h)),
            pl.BlockSpec((1, s, hp * dh), lambda bi, h: (bi, 0, ng + h)),
            pl.BlockSpec((1, hp * dh, s), lambda bi, h: (bi, h, 0)),
        ],
        out_specs=pl.BlockSpec((1, s, hp * dh), lambda bi, h: (bi, 0, h)),
        out_shape=jax.ShapeDtypeStruct((b, s, MIX_WIDTH), BF16),
        scratch_shapes=[pltpu.VMEM((2, hp, 1, blk), F32),
                        pltpu.VMEM((2, hp, dh + SUM_ROWS, blk), F32),
                        pltpu.VMEM((hp, blk, LANES), F32),
                        pltpu.VMEM((hp, s, dh), BF16),
                        pltpu.VMEM((hp, nb, s), F32),
                        pltpu.VMEM((2, hp, blk, blk), F32)],
        compiler_params=_params("parallel", "parallel"),
        name="moba_attn",
    )(slopes, proj, proj, vt)


def _swa_attn_kernel(slopes_ref, sinks_ref, q_ref, k_ref, v_ref, o_ref, bm_ref, *, sub_blocks):
    i = pl.program_id(1)
    win = SWA_WINDOW
    dh = SWA_HEAD_DIM
    grp = SWA_Q_HEADS // SWA_KV_HEADS
    tq = win
    top = lax.broadcasted_iota(jnp.int32, (2 * tq, 1), 0) < tq

    @pl.when(i == 0)
    def _():
        qi = lax.broadcasted_iota(jnp.int32, (2 * tq, 2 * win), 0) % tq
        dist0 = qi - lax.broadcasted_iota(jnp.int32, (2 * tq, 2 * win), 1)
        for layout, dist in enumerate((dist0, dist0 + win)):
            valid = (dist >= 0) & (dist < win)
            distf = dist.astype(F32)
            for pair in range(SWA_Q_HEADS // 2):
                slope = jnp.where(top, slopes_ref[2 * pair], slopes_ref[2 * pair + 1])
                bm_ref[layout, pair] = jnp.where(valid, -slope * distf, NEG_INF)

    lane = lax.broadcasted_iota(jnp.int32, (tq, 2 * dh), 1)

    for sb, kvh in [(sb, kvh) for sb in range(sub_blocks) for kvh in range(SWA_KV_HEADS)]:
        blk = i * sub_blocks + sb
        kstart = pl.multiple_of(jnp.maximum(blk - 1, 0) * win, win)
        layout = jnp.where(blk == 0, 0, 1)
        rows = slice(sb * win, (sb + 1) * win)
        k = k_ref[0, pl.ds(kstart, 2 * win), kvh * 2 * dh:(kvh + 1) * 2 * dh]
        v = v_ref[0, pl.ds(kstart, 2 * win), kvh * 2 * dh:(kvh + 1) * 2 * dh]
        for pair in range(grp // 2):
            h0 = kvh * grp + 2 * pair
            q = q_ref[0, rows, h0 * dh:(h0 + 2) * dh].astype(F32) * (dh ** -0.5)
            qq = jnp.concatenate([jnp.where(lane < dh, q, 0.0), jnp.where(lane >= dh, q, 0.0)],
                                 axis=0).astype(BF16)
            sink = jnp.where(top, sinks_ref[h0], sinks_ref[h0 + 1])
            s = _dot_nt(qq, k) + bm_ref[layout, h0 // 2]
            m = jnp.maximum(jnp.max(s, axis=-1, keepdims=True), sink)
            p = jnp.exp(s - m)
            denom = jnp.sum(p, axis=-1, keepdims=True) + jnp.exp(sink - m)
            o2 = jnp.dot(p.astype(BF16), v, preferred_element_type=F32) / denom
            o = jnp.where(lane < dh, o2[:tq], o2[tq:])
            o_ref[0, rows, h0 * dh:(h0 + 2) * dh] = o.astype(o_ref.dtype)


def _swa_attention(proj, slopes, sinks, *, sub_blocks=4):
    b, s, _ = proj.shape
    win = SWA_WINDOW
    e = MIX_WIDTH
    kvw = SWA_KV_HEADS * 2 * SWA_HEAD_DIM
    tq = sub_blocks * win
    assert s % tq == 0 and s >= 2 * win
    smem = pl.BlockSpec(memory_space=pltpu.SMEM)
    return pl.pallas_call(
        functools.partial(_swa_attn_kernel, sub_blocks=sub_blocks),
        grid=(b, s // tq),
        in_specs=[
            smem, smem,
            pl.BlockSpec((1, tq, e), lambda bi, i: (bi, i, 0)),
            pl.BlockSpec((1, s, kvw), lambda bi, i: (bi, 0, 2 * e // kvw)),
            pl.BlockSpec((1, s, kvw), lambda bi, i: (bi, 0, 2 * e // kvw + 1)),
        ],
        out_specs=pl.BlockSpec((1, tq, e), lambda bi, i: (bi, i, 0)),
        out_shape=jax.ShapeDtypeStruct((b, s, e), BF16),
        scratch_shapes=[pltpu.VMEM((2, SWA_Q_HEADS // 2, 2 * win, 2 * win), F32)],
        compiler_params=_params("parallel", "arbitrary"),
        name="swa_attn",
    )(slopes, sinks, proj, proj, proj)


def _s5_ssm_kernel(u_ref, perm_ref, bb_ref, a_ref, cc_ref, d_ref, y_ref, st_ref, bu_ref, xb_ref,
                   *, bsz, tt, chunk_lanes):
    ns = S5_SLAB_STATE
    cw = chunk_lanes

    @pl.when(pl.program_id(1) == 0)
    def _():
        st_ref[...] = jnp.zeros_like(st_ref)

    tp = PERM_STEPS
    u32 = jnp.concatenate(
        [jnp.dot(perm_ref[0], u_ref[:, g * tp:(g + 1) * tp, :].reshape(bsz * tp, S5_SLAB),
                 preferred_element_type=F32) for g in range(tt // tp)], axis=0)
    u = u32.astype(BF16)
    y = d_ref[0] * u32
    for c in range(ns // cw):
        lo = c * cw
        bu_ref[2 * c] = jnp.dot(u, bb_ref[0, :, lo:lo + cw], preferred_element_type=F32)
        bu_ref[2 * c + 1] = jnp.dot(u, bb_ref[0, :, ns + lo:ns + lo + cw],
                                    preferred_element_type=F32)
        ar = jnp.broadcast_to(a_ref[0, 0:1, lo:lo + cw], (bsz, cw))
        ai = jnp.broadcast_to(a_ref[0, 1:2, lo:lo + cw], (bsz, cw))
        xr = st_ref[0, :, lo:lo + cw]
        xi = st_ref[1, :, lo:lo + cw]
        for t in range(tt):
            rows = slice(t * bsz, (t + 1) * bsz)
            xr, xi = (ar * xr - ai * xi + bu_ref[2 * c, rows, :],
                      ar * xi + ai * xr + bu_ref[2 * c + 1, rows, :])
            xb_ref[2 * c, rows, :] = xr.astype(BF16)
            xb_ref[2 * c + 1, rows, :] = xi.astype(BF16)
        st_ref[0, :, lo:lo + cw] = xr
        st_ref[1, :, lo:lo + cw] = xi
        y = (y + jnp.dot(xb_ref[2 * c], cc_ref[0, lo:lo + cw], preferred_element_type=F32)
             + jnp.dot(xb_ref[2 * c + 1], cc_ref[0, ns + lo:ns + lo + cw],
                       preferred_element_type=F32))
    y = y.astype(BF16)
    for g in range(tt // tp):
        y_bt = jnp.dot(perm_ref[1], y[g * bsz * tp:(g + 1) * bsz * tp],
                       preferred_element_type=F32)
        y_ref[:, g * tp:(g + 1) * tp, :] = y_bt.reshape(bsz, tp, S5_SLAB).astype(y_ref.dtype)


def _s5_ssm(proj, bd_b, a_bar, bd_c, d_skip, *, tt=128, chunk_lanes=256):
    b, s, _ = proj.shape
    tt = min(tt, s)
    assert s % tt == 0 and tt % PERM_STEPS == 0 and b % 16 == 0
    n_slab = MIX_WIDTH // S5_SLAB
    ns = S5_SLAB_STATE
    n_buf = 2 * ns // chunk_lanes
    rows = b * tt
    prow = jnp.arange(b * PERM_STEPS)
    fwd = jax.nn.one_hot((prow % b) * PERM_STEPS + prow // b, b * PERM_STEPS, dtype=BF16)
    perm = jnp.stack([fwd, fwd.T])
    return pl.pallas_call(
        functools.partial(_s5_ssm_kernel, bsz=b, tt=tt, chunk_lanes=chunk_lanes),
        grid=(n_slab, s // tt),
        in_specs=[
            pl.BlockSpec((b, tt, S5_SLAB), lambda sl, t: (0, t, sl)),
            pl.BlockSpec((2, b * PERM_STEPS, b * PERM_STEPS), lambda sl, t: (0, 0, 0)),
            pl.BlockSpec((1, S5_SLAB, 2 * ns), lambda sl, t: (sl, 0, 0)),
            pl.BlockSpec((1, 2, ns), lambda sl, t: (sl, 0, 0)),
            pl.BlockSpec((1, 2 * ns, S5_SLAB), lambda sl, t: (sl, 0, 0)),
            pl.BlockSpec((1, 1, S5_SLAB), lambda sl, t: (sl, 0, 0)),
        ],
        out_specs=pl.BlockSpec((b, tt, S5_SLAB), lambda sl, t: (0, t, sl)),
        out_shape=jax.ShapeDtypeStruct((b, s, MIX_WIDTH), BF16),
        scratch_shapes=[pltpu.VMEM((2, b, ns), F32),
                        pltpu.VMEM((n_buf, rows, chunk_lanes), F32),
                        pltpu.VMEM((n_buf, rows, chunk_lanes), BF16)],
        compiler_params=_params("parallel", "arbitrary"),
        name="s5_ssm",
    )(proj, perm, bd_b, a_bar, bd_c, d_skip)


def _s5_discretize(a_re, a_im, log_dt, b_re, b_im, c_re, c_im, d_skip):
    lr, li = a_re.astype(F32), a_im.astype(F32)
    dt = jnp.exp(log_dt.astype(F32))[:, None]
    mag = jnp.exp(lr * dt)
    ab_re, ab_im = mag * jnp.cos(li * dt), mag * jnp.sin(li * dt)
    den = lr * lr + li * li
    f_re = ((ab_re - 1.0) * lr + ab_im * li) / den
    f_im = (ab_im * lr - (ab_re - 1.0) * li) / den
    br, bi = b_re.astype(F32), b_im.astype(F32)
    bb_re = f_re[..., None] * br - f_im[..., None] * bi
    bb_im = f_re[..., None] * bi + f_im[..., None] * br
    n_slab = MIX_WIDTH // S5_SLAB
    gs = S5_SLAB // S5_GROUP
    eye = jnp.eye(gs, dtype=F32)

    def pack_b(m):
        m = m.reshape(n_slab, gs, S5_STATE, S5_GROUP)
        return jnp.einsum('sgpc,gh->sgchp', m, eye).reshape(n_slab, S5_SLAB, S5_SLAB_STATE)

    def pack_c(m):
        m = m.reshape(n_slab, gs, S5_GROUP, S5_STATE)
        return jnp.einsum('sgcp,gh->shpgc', m, eye).reshape(n_slab, S5_SLAB_STATE, S5_SLAB)

    bd_b = jnp.concatenate([pack_b(bb_re), pack_b(bb_im)], axis=2).astype(BF16)
    bd_c = jnp.concatenate([pack_c(c_re.astype(F32)), pack_c(-c_im.astype(F32))], axis=1).astype(BF16)
    a_bar = jnp.stack([ab_re.reshape(n_slab, S5_SLAB_STATE), ab_im.reshape(n_slab, S5_SLAB_STATE)],
                      axis=1)
    return bd_b, a_bar, bd_c, d_skip.astype(F32).reshape(n_slab, 1, S5_SLAB)


def _glu_kernel(y_ref, w_ref, b_ref, o_ref, *, chunks):
    cm = y_ref.shape[0] // chunks
    for c in range(chunks):
        rows = slice(c * cm, (c + 1) * cm)
        g = jax.nn.gelu(y_ref[rows, :].astype(F32))
        r = jnp.dot(g.astype(BF16), w_ref[...], preferred_element_type=F32) + b_ref[...]
        o_ref[rows, :] = (g * _sigmoid(r)).astype(o_ref.dtype)


def _glu(y2d, w, bias, *, tm=512, chunks=4):
    t, e = y2d.shape
    tm = min(tm, t)
    assert t % tm == 0 and tm % chunks == 0
    return pl.pallas_call(
        functools.partial(_glu_kernel, chunks=chunks),
        grid=(t // tm,),
        in_specs=[
            pl.BlockSpec((tm, e), lambda i: (i, 0)),
            pl.BlockSpec((e, e), lambda i: (0, 0), pipeline_mode=pl.Buffered(1)),
            pl.BlockSpec((1, e), lambda i: (0, 0)),
        ],
        out_specs=pl.BlockSpec((tm, e), lambda i: (i, 0)),
        out_shape=jax.ShapeDtypeStruct((t, e), BF16),
        compiler_params=_params("parallel"),
        name="s5_glu",
    )(y2d, w, bias.reshape(1, e).astype(F32))


def _s5_layer(x2d, h2d, bsz, seq, post_g, next_g, w_in, a_re, a_im, log_dt, b_re, b_im, c_re,
              c_im, d_skip, w_glu, b_glu, w_out):
    e = MIX_WIDTH
    proj = _in_proj(h2d, w_in.astype(BF16))
    bd_b, a_bar, bd_c, d3 = _s5_discretize(a_re, a_im, log_dt, b_re, b_im, c_re, c_im, d_skip)
    y = _s5_ssm(proj.reshape(bsz, seq, 2 * e), bd_b, a_bar, bd_c, d3)
    y = _glu(y.reshape(bsz * seq, e), w_glu.astype(BF16), b_glu)
    return _gate_out(y, proj, 1, x2d, w_out.astype(BF16), post_g, next_g)


def _attn_projections(h2d, seq, w_in):
    w = w_in.astype(BF16)
    proj = _in_proj(h2d, w, skip_block=2)
    vt = _in_proj_t(h2d, w, 2, seq)
    return proj, vt


def _diff_layer(x2d, h2d, bsz, seq, layer_idx, post_g, next_g, w_in, lq1, lk1, lq2, lk2, subln,
                w_out):
    e = MIX_WIDTH
    proj, vt = _attn_projections(h2d, seq, w_in)
    lam_init = 0.8 - 0.6 * math.exp(-0.3 * layer_idx)
    lam = (jnp.exp(jnp.sum(lq1.astype(F32) * lk1.astype(F32)))
           - jnp.exp(jnp.sum(lq2.astype(F32) * lk2.astype(F32))) + lam_init).reshape(1)
    o = _diff_attention(proj.reshape(bsz, seq, 3 * e), vt, _alibi_slopes(DIFF_HEADS), lam, subln,
                        1.0 - lam_init)
    return _gate_out(o.reshape(bsz * seq, e), proj, 2, x2d, w_out.astype(BF16), post_g, next_g)


def _moba_layer(x2d, h2d, bsz, seq, post_g, next_g, w_in, w_out):
    e = MIX_WIDTH
    proj, vt = _attn_projections(h2d, seq, w_in)
    o = _moba_attention(proj.reshape(bsz, seq, 3 * e), vt, _alibi_slopes(MOBA_HEADS))
    return _gate_out(o.reshape(bsz * seq, e), proj, 2, x2d, w_out.astype(BF16), post_g, next_g)


def _swa_layer(x2d, h2d, bsz, seq, post_g, next_g, w_in, sinks, w_out):
    e = MIX_WIDTH
    dh = SWA_HEAD_DIM
    nqc, nkc = SWA_Q_HEADS * dh, SWA_KV_HEADS * dh
    wq, wk = w_in[:, :nqc], w_in[:, nqc:nqc + nkc]
    wv, wz = w_in[:, nqc + nkc:nqc + 2 * nkc], w_in[:, nqc + 2 * nkc:]

    def dup(w):
        w = w.reshape(-1, SWA_KV_HEADS, 1, dh)
        return jnp.broadcast_to(w, (w.shape[0], SWA_KV_HEADS, 2, dh)).reshape(-1, 2 * nkc)

    w_cat = jnp.concatenate([wq, wz, dup(wk), dup(wv)], axis=1).astype(BF16)
    proj = _in_proj(h2d, w_cat)
    o = _swa_attention(proj.reshape(bsz, seq, w_cat.shape[1]), _alibi_slopes(SWA_Q_HEADS),
                       sinks.astype(F32))
    return _gate_out(o.reshape(bsz * seq, e), proj, 1, x2d, w_out.astype(BF16), post_g, next_g)


def kernel(x, pre_norm, post_norm, s5_w_in, s5_a_re, s5_a_im, s5_log_dt, s5_b_re, s5_b_im, s5_c_re, s5_c_im, s5_d, s5_w_glu, s5_b_glu, s5_w_out, diff_w_in, diff_lq1, diff_lk1, diff_lq2, diff_lk2, diff_subln, diff_w_out, moba_w_in, moba_w_out, swa_w_in, swa_sinks, swa_w_out):
    bsz, seq, d = x.shape
    depth = pre_norm.shape[0]
    x2d = x.reshape(bsz * seq, d)
    h2d = _prenorm(x2d, pre_norm[0])
    for i in range(depth):
        kind, j = i % 4, i // 4
        post_g = post_norm[i]
        next_g = pre_norm[i + 1] if i + 1 < depth else None
        if kind == 0:
            x2d, h2d = _s5_layer(x2d, h2d, bsz, seq, post_g, next_g, s5_w_in[j], s5_a_re[j],
                                 s5_a_im[j], s5_log_dt[j], s5_b_re[j], s5_b_im[j], s5_c_re[j],
                                 s5_c_im[j], s5_d[j], s5_w_glu[j], s5_b_glu[j], s5_w_out[j])
        elif kind == 1:
            x2d, h2d = _diff_layer(x2d, h2d, bsz, seq, i, post_g, next_g, diff_w_in[j],
                                   diff_lq1[j], diff_lk1[j], diff_lq2[j], diff_lk2[j],
                                   diff_subln[j], diff_w_out[j])
        elif kind == 2:
            x2d, h2d = _moba_layer(x2d, h2d, bsz, seq, post_g, next_g, moba_w_in[j],
                                   moba_w_out[j])
        else:
            x2d, h2d = _swa_layer(x2d, h2d, bsz, seq, post_g, next_g, swa_w_in[j], swa_sinks[j],
                                  swa_w_out[j])
    return x2d.reshape(bsz, seq, d)
```

```python
import functools
import math

import jax
import jax.numpy as jnp
from jax import lax
from jax.experimental import pallas as pl
from jax.experimental.pallas import tpu as pltpu

F32 = jnp.float32
BF16 = jnp.bfloat16

D_MODEL = 2048
MIX_WIDTH = D_MODEL
RMS_EPS = 1e-6
NEG_INF = -1e30
LOG2E = math.log2(math.e)
LANES = 128

S5_GROUP = 16
S5_STATE = 64
S5_SLAB = 256
S5_SLAB_STATE = (S5_SLAB // S5_GROUP) * S5_STATE
PERM_STEPS = 16

DIFF_HEADS = 16
DIFF_HEAD_DIM = 64
MOBA_HEADS = 16
MOBA_HEAD_DIM = 128
MOBA_BLOCK = 256
MOBA_TOPK = 3
SWA_Q_HEADS = 32
SWA_KV_HEADS = 4
SWA_HEAD_DIM = 64
SWA_WINDOW = 128

VMEM_LIMIT = 56 * 1024 * 1024


def _params(*sem):
    return pltpu.CompilerParams(dimension_semantics=sem, vmem_limit_bytes=VMEM_LIMIT)


def _alibi_slopes(n_heads):
    return 2.0 ** (-8.0 * jnp.arange(1, n_heads + 1, dtype=F32) / n_heads)


def _dot_nt(a, b):
    return lax.dot_general(a, b, (((1,), (1,)), ((), ())), preferred_element_type=F32)


def _sigmoid(x):
    return 0.5 * jnp.tanh(0.5 * x) + 0.5


def _rms_normalize(x, gain):
    ms = jnp.mean(x * x, axis=-1, keepdims=True)
    return x * lax.rsqrt(ms + RMS_EPS) * gain


def _prenorm_kernel(x_ref, g_ref, h_ref):
    h_ref[...] = _rms_normalize(x_ref[...], g_ref[...]).astype(BF16)


def _prenorm(x2d, gain, *, tm=512):
    t, d = x2d.shape
    tm = min(tm, t)
    assert t % tm == 0
    return pl.pallas_call(
        _prenorm_kernel,
        grid=(t // tm,),
        in_specs=[pl.BlockSpec((tm, d), lambda i: (i, 0)), pl.BlockSpec((1, d), lambda i: (0, 0))],
        out_specs=pl.BlockSpec((tm, d), lambda i: (i, 0)),
        out_shape=jax.ShapeDtypeStruct((t, d), BF16),
        compiler_params=_params("parallel"),
        name="prenorm",
    )(x2d, gain.reshape(1, d).astype(F32))


def _col_tile(n, cap=2048):
    return max(t for t in range(256, cap + 1, 256) if n % t == 0)


def _matmul_kernel(h_ref, w_ref, o_ref):
    o_ref[...] = jnp.dot(h_ref[...], w_ref[...], preferred_element_type=F32).astype(o_ref.dtype)


def _in_proj(h2d, w, *, skip_block=None, tm=1024):
    t, d = h2d.shape
    tm = min(tm, t)
    tn = _col_tile(w.shape[1])
    assert t % tm == 0
    if skip_block is None:
        n = w.shape[1]
        w_map = lambda j, i: (0, j)
    else:
        n = w.shape[1] - tn
        w_map = lambda j, i: (0, j + (j >= skip_block).astype(jnp.int32))
    return pl.pallas_call(
        _matmul_kernel,
        grid=(n // tn, t // tm),
        in_specs=[
            pl.BlockSpec((tm, d), lambda j, i: (i, 0)),
            pl.BlockSpec((d, tn), w_map),
        ],
        out_specs=pl.BlockSpec((tm, tn), lambda j, i: (i, j)),
        out_shape=jax.ShapeDtypeStruct((t, n), BF16),
        compiler_params=_params("parallel", "parallel"),
        name="in_proj",
    )(h2d, w)


def _matmul_t_kernel(h_ref, w_ref, o_ref, wt_ref):
    @pl.when(pl.program_id(1) == 0)
    def _():
        wt_ref[...] = w_ref[...].T

    o_ref[0] = _dot_nt(wt_ref[...], h_ref[...]).astype(o_ref.dtype)


def _in_proj_t(h2d, w, col_block, seq, *, tm=1024):
    t, d = h2d.shape
    tm = min(tm, seq)
    tn = _col_tile(w.shape[1])
    assert seq % tm == 0
    tps = seq // tm
    return pl.pallas_call(
        _matmul_t_kernel,
        grid=(1, t // tm),
        in_specs=[
            pl.BlockSpec((tm, d), lambda j, i: (i, 0)),
            pl.BlockSpec((d, tn), lambda j, i: (0, col_block)),
        ],
        out_specs=pl.BlockSpec((1, tn, tm), lambda j, i: (i // tps, 0, i % tps)),
        out_shape=jax.ShapeDtypeStruct((t // seq, tn, seq), BF16),
        scratch_shapes=[pltpu.VMEM((tn, d), BF16)],
        compiler_params=_params("parallel", "arbitrary"),
        name="in_proj_t",
    )(h2d, w)


def _gate_out_kernel(o_ref, z_ref, x_ref, w_ref, g_ref, gn_ref, out_ref, *h_ref, chunks):
    cm = o_ref.shape[0] // chunks
    for c in range(chunks):
        rows = slice(c * cm, (c + 1) * cm)
        hz = z_ref[rows, :] * 0.5
        y = o_ref[rows, :] * (hz * jnp.tanh(hz) + hz)
        r = jnp.dot(y, w_ref[...], preferred_element_type=F32)
        xn = x_ref[rows, :] + _rms_normalize(r, g_ref[...])
        out_ref[rows, :] = xn
        if h_ref:
            h_ref[0][rows, :] = _rms_normalize(xn, gn_ref[...]).astype(BF16)


def _gate_out(o2d, zsrc, z_block, x2d, w, gain, next_gain, *, tm=512, chunks=4):
    t, e = o2d.shape
    d = w.shape[1]
    tm = min(tm, t)
    assert t % tm == 0 and tm % chunks == 0
    row_map = lambda i: (i, 0)
    const = lambda i: (0, 0)
    z_map = lambda i: (i, z_block)
    with_h = next_gain is not None
    gn = (next_gain if with_h else gain).reshape(1, d).astype(F32)
    out_specs = [pl.BlockSpec((tm, d), row_map)]
    out_shape = [jax.ShapeDtypeStruct((t, d), F32)]
    if with_h:
        out_specs.append(pl.BlockSpec((tm, d), row_map))
        out_shape.append(jax.ShapeDtypeStruct((t, d), BF16))
    res = pl.pallas_call(
        functools.partial(_gate_out_kernel, chunks=chunks),
        grid=(t // tm,),
        in_specs=[
            pl.BlockSpec((tm, e), row_map),
            pl.BlockSpec((tm, e), z_map),
            pl.BlockSpec((tm, d), row_map),
            pl.BlockSpec((e, d), const, pipeline_mode=pl.Buffered(1)),
            pl.BlockSpec((1, d), const),
            pl.BlockSpec((1, d), const),
        ],
        out_specs=out_specs,
        out_shape=out_shape,
        compiler_params=_params("parallel"),
        name="gate_out",
    )(o2d, zsrc, x2d, w, gain.reshape(1, d).astype(F32), gn)
    return (res[0], res[1]) if with_h else (res[0], None)


BIG = 1e30
SUM_ROWS = 16


def _softmax_step_t(s_ref, c, ok, vt, m_ref, acc_ref):
    m_old = m_ref[...]
    mt = jnp.max(s_ref[...], axis=0, keepdims=True) + c
    if ok is not None:
        mt = jnp.where(ok, mt, NEG_INF)
    m_new = jnp.maximum(m_old, mt)
    alpha = jnp.exp2(m_old - m_new)
    r = m_new - c
    if ok is not None:
        r = jnp.where(ok, r, BIG)
    p = jnp.exp2(s_ref[...] - r).astype(BF16)
    vt1 = jnp.concatenate([vt, jnp.ones((SUM_ROWS, vt.shape[1]), BF16)], axis=0)
    acc_ref[...] = alpha * acc_ref[...] + jnp.dot(vt1, p, preferred_element_type=F32)
    m_ref[...] = m_new


def _paired_causal_sweep(n_blocks, init, scores, finish, finalize):
    n_past = n_blocks - 1

    for i in range(n_blocks // 2):
        blk_a, blk_b = i, n_blocks - 1 - i
        init()
        tiles = []
        for p in range(n_past):
            tiles.append((blk_a, p, 0, False) if p < i else (blk_b, p - i, 1, False))
        tiles += [(blk_a, blk_a, 0, True), (blk_b, blk_b, 1, True)]
        scores(tiles[0][0], tiles[0][1], 0)
        for p, (qb, j, slot, diagonal) in enumerate(tiles):
            if p + 1 < len(tiles):
                scores(tiles[p + 1][0], tiles[p + 1][1], (p + 1) % 2)
            finish(qb, j, slot, p % 2, diagonal)
        finalize(blk_a, 0)
        finalize(blk_b, 1)


def _diff_attn_kernel(slopes_ref, lam_ref, q_ref, k_ref, vt_ref, g_ref, o_ref,
                      m_ref, acc_ref, bias_ref, qq_ref, s_ref, *, tq, hp, nqb, out_scale):
    hg = pl.program_id(1)
    lam = lam_ref[0]
    d = DIFF_HEAD_DIM
    w = 2 * d
    reps = 2 * tq // LANES

    lane = lax.broadcasted_iota(jnp.int32, (tq, w), 1)
    krow = lax.broadcasted_iota(jnp.int32, (tq, LANES), 0).astype(F32)
    slope2 = [slopes_ref[hg * hp + hh] * LOG2E for hh in range(hp)]
    for hh in range(hp):
        bias_ref[hh] = slope2[hh] * krow

    def prep(qb, carry):
        rows = pl.ds(pl.multiple_of(qb * tq, tq), tq)
        for hh in range(hp):
            q = q_ref[0, rows, hh * w:(hh + 1) * w].astype(F32) * (d ** -0.5 * LOG2E)
            qq_ref[qb, hh, :tq] = jnp.where(lane < d, q, 0.0).astype(BF16)
            qq_ref[qb, hh, tq:] = jnp.where(lane >= d, q, 0.0).astype(BF16)
        return carry

    lax.fori_loop(0, nqb, prep, 0)

    def init():
        m_ref[...] = jnp.full_like(m_ref, NEG_INF)
        acc_ref[...] = jnp.zeros_like(acc_ref)

    def scores(qb, j, buf):
        keys = pl.ds(j * tq, tq)
        for hh in range(hp):
            k = k_ref[0, keys, hh * w:(hh + 1) * w]
            s_ref[buf, hh] = (_dot_nt(k, qq_ref[qb, hh])
                              + jnp.concatenate([bias_ref[hh]] * reps, axis=1))

    def finish(qb, j, slot, buf, diagonal):
        keys = pl.ds(j * tq, tq)
        for hh in range(hp):
            vt = vt_ref[0, hh * w:(hh + 1) * w, keys]
            if diagonal:
                kr = lax.broadcasted_iota(jnp.int32, (tq, 2 * tq), 0)
                qc = lax.broadcasted_iota(jnp.int32, (tq, 2 * tq), 1)
                qc = jnp.where(qc >= tq, qc - tq, qc)
                s_ref[buf, hh] = jnp.where(qc >= kr, s_ref[buf, hh], NEG_INF)
            c = slope2[hh] * float((j - qb) * tq)
            _softmax_step_t(s_ref.at[buf, hh], c, None, vt, m_ref.at[slot, hh],
                            acc_ref.at[slot, hh])

    g = jnp.concatenate([g_ref[...]] * (tq // LANES), axis=1)

    def finalize(qb, slot):
        rows = pl.ds(qb * tq, tq)
        for hh in range(hp):
            a = acc_ref[slot, hh, :w] / acc_ref[slot, hh, w:w + 1]
            o = a[:, :tq] - lam * a[:, tq:]
            ms = jnp.mean(o * o, axis=0, keepdims=True)
            o = o * lax.rsqrt(ms + RMS_EPS) * g * out_scale
            o_ref[0, rows, hh * w:(hh + 1) * w] = o.T.astype(o_ref.dtype)

    _paired_causal_sweep(nqb, init, scores, finish, finalize)


def _diff_attention(proj, vt, slopes, lam, subln, out_scale, *, tq=256, hp=2):
    b, s, _ = proj.shape
    nh = DIFF_HEADS
    w = 2 * DIFF_HEAD_DIM
    tq = min(tq, s // 2)
    nqb = s // tq
    assert s % tq == 0 and tq % LANES == 0 and nh % hp == 0 and nqb % 2 == 0
    ng = nh // hp
    smem = pl.BlockSpec(memory_space=pltpu.SMEM)
    g = jnp.broadcast_to(subln.astype(F32)[:, None], (w, LANES))
    return pl.pallas_call(
        functools.partial(_diff_attn_kernel, tq=tq, hp=hp, nqb=nqb, out_scale=out_scale),
        grid=(b, ng),
        in_specs=[
            smem, smem,
            pl.BlockSpec((1, s, hp * w), lambda bi, h: (bi, 0, h)),
            pl.BlockSpec((1, s, hp * w), lambda bi, h: (bi, 0, ng + h)),
            pl.BlockSpec((1, hp * w, s), lambda bi, h: (bi, h, 0)),
            pl.BlockSpec((w, LANES), lambda bi, h: (0, 0)),
        ],
        out_specs=pl.BlockSpec((1, s, hp * w), lambda bi, h: (bi, 0, h)),
        out_shape=jax.ShapeDtypeStruct((b, s, MIX_WIDTH), BF16),
        scratch_shapes=[pltpu.VMEM((2, hp, 1, 2 * tq), F32),
                        pltpu.VMEM((2, hp, w + SUM_ROWS, 2 * tq), F32),
                        pltpu.VMEM((hp, tq, LANES), F32),
                        pltpu.VMEM((nqb, hp, 2 * tq, w), BF16),
                        pltpu.VMEM((2, hp, tq, 2 * tq), F32)],
        compiler_params=_params("parallel", "parallel"),
        name="diff_attn",
    )(slopes, lam, proj, proj, vt, g)


def _moba_attn_kernel(slopes_ref, q_ref, k_ref, vt_ref, o_ref, m_ref, acc_ref, bias_ref,
                      qs_ref, sel_ref, s_ref, *, nb, hp):
    hg = pl.program_id(1)
    blk = MOBA_BLOCK
    dh = MOBA_HEAD_DIM
    seq = nb * blk
    reps = blk // LANES

    krow = lax.broadcasted_iota(jnp.int32, (blk, LANES), 0).astype(F32)
    bidx = lax.broadcasted_iota(jnp.int32, (nb, seq), 0)
    qblk = lax.broadcasted_iota(jnp.int32, (nb, seq), 1) // blk
    slope2 = [slopes_ref[hg * hp + hh] * LOG2E for hh in range(hp)]
    for hh in range(hp):
        bias_ref[hh] = slope2[hh] * krow
        q = q_ref[0, :, hh * dh:(hh + 1) * dh]
        qs_ref[hh] = (q.astype(F32) * (dh ** -0.5 * LOG2E)).astype(BF16)

        kh = k_ref[0, :, hh * dh:(hh + 1) * dh].astype(F32)
        kmean = jnp.mean(kh.reshape(nb, blk, dh), axis=1)
        km_hi = kmean.astype(BF16)
        km_lo = (kmean - km_hi.astype(F32)).astype(BF16)
        gate = _dot_nt(km_hi, q) + _dot_nt(km_lo, q)

        rank = jnp.zeros((nb, seq), jnp.int32)
        for jp in range(nb):
            gj = gate[jp:jp + 1, :]
            before = (gj > gate) | ((gj == gate) & (jp < bidx))
            rank = rank + jnp.where(before, 1, 0) * jnp.where(jp < qblk, 1, 0)
        sel_ref[hh] = jnp.where((rank < MOBA_TOPK) & (bidx < qblk), 1.0, 0.0)

    def init():
        m_ref[...] = jnp.full_like(m_ref, NEG_INF)
        acc_ref[...] = jnp.zeros_like(acc_ref)

    def scores(qb, j, buf):
        keys = pl.ds(j * blk, blk)
        rows = pl.ds(qb * blk, blk)
        for hh in range(hp):
            k = k_ref[0, keys, hh * dh:(hh + 1) * dh]
            s_ref[buf, hh] = (_dot_nt(k, qs_ref[hh, rows, :])
                              + jnp.concatenate([bias_ref[hh]] * reps, axis=1))

    def finish(qb, j, slot, buf, diagonal):
        keys = pl.ds(j * blk, blk)
        cols = pl.ds(qb * blk, blk)
        for hh in range(hp):
            vt = vt_ref[0, hh * dh:(hh + 1) * dh, keys]
            c = slope2[hh] * float((j - qb) * blk)
            if diagonal:
                kr = lax.broadcasted_iota(jnp.int32, (blk, blk), 0)
                qc = lax.broadcasted_iota(jnp.int32, (blk, blk), 1)
                s_ref[buf, hh] = jnp.where(qc >= kr, s_ref[buf, hh], NEG_INF)
                ok = None
            else:
                tile_idx = lax.broadcasted_iota(jnp.int32, (nb, blk), 0)
                chosen = jnp.where(tile_idx == j, sel_ref[hh, :, cols], 0.0)
                ok = jnp.max(chosen, axis=0, keepdims=True) > 0.0
            _softmax_step_t(s_ref.at[buf, hh], c, ok, vt, m_ref.at[slot, hh],
                            acc_ref.at[slot, hh])

    def finalize(qb, slot):
        rows = pl.ds(qb * blk, blk)
        for hh in range(hp):
            o = acc_ref[slot, hh, :dh] / acc_ref[slot, hh, dh:dh + 1]
            o_ref[0, rows, hh * dh:(hh + 1) * dh] = o.T.astype(o_ref.dtype)

    _paired_causal_sweep(nb, init, scores, finish, finalize)


def _moba_attention(proj, vt, slopes, *, hp=4):
    b, s, _ = proj.shape
    nh = MOBA_HEADS
    dh = MOBA_HEAD_DIM
    blk = MOBA_BLOCK
    nb = s // blk
    assert s % blk == 0 and nh % hp == 0 and nb % 2 == 0
    ng = nh // hp
    smem = pl.BlockSpec(memory_space=pltpu.SMEM)
    return pl.pallas_call(
        functools.partial(_moba_attn_kernel, nb=nb, hp=hp),
        grid=(b, ng),
        in_specs=[
            smem,
            pl.BlockSpec((1, s, hp * dh), lambda bi, h: (bi, 0, h)),
            pl.BlockSpec((1, s, hp * dh), lambda bi, h: (bi, 0, ng + h)),
            pl.BlockSpec((1, hp * dh, s), lambda bi, h: (bi, h, 0)),
        ],
        out_specs=pl.BlockSpec((1, s, hp * dh), lambda bi, h: (bi, 0, h)),
        out_shape=jax.ShapeDtypeStruct((b, s, MIX_WIDTH), BF16),
        scratch_shapes=[pltpu.VMEM((2, hp, 1, blk), F32),
                        pltpu.VMEM((2, hp, dh + SUM_ROWS, blk), F32),
                        pltpu.VMEM((hp, blk, LANES), F32),
                        pltpu.VMEM((hp, s, dh), BF16),
                        pltpu.VMEM((hp, nb, s), F32),
                        pltpu.VMEM((2, hp, blk, blk), F32)],
        compiler_params=_params("parallel", "parallel"),
        name="moba_attn",
    )(slopes, proj, proj, vt)


def _swa_attn_kernel(slopes_ref, sinks_ref, q_ref, k_ref, v_ref, o_ref, bm_ref, *, sub_blocks):
    i = pl.program_id(1)
    win = SWA_WINDOW
    dh = SWA_HEAD_DIM
    grp = SWA_Q_HEADS // SWA_KV_HEADS
    tq = win
    top = lax.broadcasted_iota(jnp.int32, (2 * tq, 1), 0) < tq

    @pl.when(i == 0)
    def _():
        qi = lax.broadcasted_iota(jnp.int32, (2 * tq, 2 * win), 0) % tq
        dist0 = qi - lax.broadcasted_iota(jnp.int32, (2 * tq, 2 * win), 1)
        for layout, dist in enumerate((dist0, dist0 + win)):
            valid = (dist >= 0) & (dist < win)
            distf = dist.astype(F32)
            for pair in range(SWA_Q_HEADS // 2):
                slope = jnp.where(top, slopes_ref[2 * pair], slopes_ref[2 * pair + 1])
                bm_ref[layout, pair] = jnp.where(valid, -slope * distf, NEG_INF)

    lane = lax.broadcasted_iota(jnp.int32, (tq, 2 * dh), 1)

    for sb, kvh in [(sb, kvh) for sb in range(sub_blocks) for kvh in range(SWA_KV_HEADS)]:
        blk = i * sub_blocks + sb
        kstart = pl.multiple_of(jnp.maximum(blk - 1, 0) * win, win)
        layout = jnp.where(blk == 0, 0, 1)
        rows = slice(sb * win, (sb + 1) * win)
        k = k_ref[0, pl.ds(kstart, 2 * win), kvh * 2 * dh:(kvh + 1) * 2 * dh]
        v = v_ref[0, pl.ds(kstart, 2 * win), kvh * 2 * dh:(kvh + 1) * 2 * dh]
        for pair in range(grp // 2):
            h0 = kvh * grp + 2 * pair
            q = q_ref[0, rows, h0 * dh:(h0 + 2) * dh].astype(F32) * (dh ** -0.5)
            qq = jnp.concatenate([jnp.where(lane < dh, q, 0.0), jnp.where(lane >= dh, q, 0.0)],
                                 axis=0).astype(BF16)
            sink = jnp.where(top, sinks_ref[h0], sinks_ref[h0 + 1])
            s = _dot_nt(qq, k) + bm_ref[layout, h0 // 2]
            m = jnp.maximum(jnp.max(s, axis=-1, keepdims=True), sink)
            p = jnp.exp(s - m)
            denom = jnp.sum(p, axis=-1, keepdims=True) + jnp.exp(sink - m)
            o2 = jnp.dot(p.astype(BF16), v, preferred_element_type=F32) / denom
            o = jnp.where(lane < dh, o2[:tq], o2[tq:])
            o_ref[0, rows, h0 * dh:(h0 + 2) * dh] = o.astype(o_ref.dtype)


def _swa_attention(proj, slopes, sinks, *, sub_blocks=4):
    b, s, _ = proj.shape
    win = SWA_WINDOW
    e = MIX_WIDTH
    kvw = SWA_KV_HEADS * 2 * SWA_HEAD_DIM
    tq = sub_blocks * win
    assert s % tq == 0 and s >= 2 * win
    smem = pl.BlockSpec(memory_space=pltpu.SMEM)
    return pl.pallas_call(
        functools.partial(_swa_attn_kernel, sub_blocks=sub_blocks),
        grid=(b, s // tq),
        in_specs=[
            smem, smem,
            pl.BlockSpec((1, tq, e), lambda bi, i: (bi, i, 0)),
            pl.BlockSpec((1, s, kvw), lambda bi, i: (bi, 0, 2 * e // kvw)),
            pl.BlockSpec((1, s, kvw), lambda bi, i: (bi, 0, 2 * e // kvw + 1)),
        ],
        out_specs=pl.BlockSpec((1, tq, e), lambda bi, i: (bi, i, 0)),
        out_shape=jax.ShapeDtypeStruct((b, s, e), BF16),
        scratch_shapes=[pltpu.VMEM((2, SWA_Q_HEADS // 2, 2 * win, 2 * win), F32)],
        compiler_params=_params("parallel", "arbitrary"),
        name="swa_attn",
    )(slopes, sinks, proj, proj, proj)


def _s5_ssm_kernel(u_ref, perm_ref, bb_ref, a_ref, cc_ref, d_ref, y_ref, st_ref, bu_ref, xb_ref,
                   *, bsz, tt, chunk_lanes):
    ns = S5_SLAB_STATE
    cw = chunk_lanes

    @pl.when(pl.program_id(1) == 0)
    def _():
        st_ref[...] = jnp.zeros_like(st_ref)

    tp = PERM_STEPS
    u32 = jnp.concatenate(
        [jnp.dot(perm_ref[0], u_ref[:, g * tp:(g + 1) * tp, :].reshape(bsz * tp, S5_SLAB),
                 preferred_element_type=F32) for g in range(tt // tp)], axis=0)
    u = u32.astype(BF16)
    y = d_ref[0] * u32
    for c in range(ns // cw):
        lo = c * cw
        bu_ref[2 * c] = jnp.dot(u, bb_ref[0, :, lo:lo + cw], preferred_element_type=F32)
        bu_ref[2 * c + 1] = jnp.dot(u, bb_ref[0, :, ns + lo:ns + lo + cw],
                                    preferred_element_type=F32)
        ar = jnp.broadcast_to(a_ref[0, 0:1, lo:lo + cw], (bsz, cw))
        ai = jnp.broadcast_to(a_ref[0, 1:2, lo:lo + cw], (bsz, cw))
        xr = st_ref[0, :, lo:lo + cw]
        xi = st_ref[1, :, lo:lo + cw]
        for t in range(tt):
            rows = slice(t * bsz, (t + 1) * bsz)
            xr, xi = (ar * xr - ai * xi + bu_ref[2 * c, rows, :],
                      ar * xi + ai * xr + bu_ref[2 * c + 1, rows, :])
            xb_ref[2 * c, rows, :] = xr.astype(BF16)
            xb_ref[2 * c + 1, rows, :] = xi.astype(BF16)
        st_ref[0, :, lo:lo + cw] = xr
        st_ref[1, :, lo:lo + cw] = xi
        y = (y + jnp.dot(xb_ref[2 * c], cc_ref[0, lo:lo + cw], preferred_element_type=F32)
             + jnp.dot(xb_ref[2 * c + 1], cc_ref[0, ns + lo:ns + lo + cw],
                       preferred_element_type=F32))
    y = y.astype(BF16)
    for g in range(tt // tp):
        y_bt = jnp.dot(perm_ref[1], y[g * bsz * tp:(g + 1) * bsz * tp],
                       preferred_element_type=F32)
        y_ref[:, g * tp:(g + 1) * tp, :] = y_bt.reshape(bsz, tp, S5_SLAB).astype(y_ref.dtype)


def _s5_ssm(proj, bd_b, a_bar, bd_c, d_skip, *, tt=128, chunk_lanes=256):
    b, s, _ = proj.shape
    tt = min(tt, s)
    assert s % tt == 0 and tt % PERM_STEPS == 0 and b % 16 == 0
    n_slab = MIX_WIDTH // S5_SLAB
    ns = S5_SLAB_STATE
    n_buf = 2 * ns // chunk_lanes
    rows = b * tt
    prow = jnp.arange(b * PERM_STEPS)
    fwd = jax.nn.one_hot((prow % b) * PERM_STEPS + prow // b, b * PERM_STEPS, dtype=BF16)
    perm = jnp.stack([fwd, fwd.T])
    return pl.pallas_call(
        functools.partial(_s5_ssm_kernel, bsz=b, tt=tt, chunk_lanes=chunk_lanes),
        grid=(n_slab, s // tt),
        in_specs=[
            pl.BlockSpec((b, tt, S5_SLAB), lambda sl, t: (0, t, sl)),
            pl.BlockSpec((2, b * PERM_STEPS, b * PERM_STEPS), lambda sl, t: (0, 0, 0)),
            pl.BlockSpec((1, S5_SLAB, 2 * ns), lambda sl, t: (sl, 0, 0)),
            pl.BlockSpec((1, 2, ns), lambda sl, t: (sl, 0, 0)),
            pl.BlockSpec((1, 2 * ns, S5_SLAB), lambda sl, t: (sl, 0, 0)),
            pl.BlockSpec((1, 1, S5_SLAB), lambda sl, t: (sl, 0, 0)),
        ],
        out_specs=pl.BlockSpec((b, tt, S5_SLAB), lambda sl, t: (0, t, sl)),
        out_shape=jax.ShapeDtypeStruct((b, s, MIX_WIDTH), BF16),
        scratch_shapes=[pltpu.VMEM((2, b, ns), F32),
                        pltpu.VMEM((n_buf, rows, chunk_lanes), F32),
                        pltpu.VMEM((n_buf, rows, chunk_lanes), BF16)],
        compiler_params=_params("parallel", "arbitrary"),
        name="s5_ssm",
    )(proj, perm, bd_b, a_bar, bd_c, d_skip)


def _s5_discretize(a_re, a_im, log_dt, b_re, b_im, c_re, c_im, d_skip):
    lr, li = a_re.astype(F32), a_im.astype(F32)
    dt = jnp.exp(log_dt.astype(F32))[:, None]
    mag = jnp.exp(lr * dt)
    ab_re, ab_im = mag * jnp.cos(li * dt), mag * jnp.sin(li * dt)
    den = lr * lr + li * li
    f_re = ((ab_re - 1.0) * lr + ab_im * li) / den
    f_im = (ab_im * lr - (ab_re - 1.0) * li) / den
    br, bi = b_re.astype(F32), b_im.astype(F32)
    bb_re = f_re[..., None] * br - f_im[..., None] * bi
    bb_im = f_re[..., None] * bi + f_im[..., None] * br
    n_slab = MIX_WIDTH // S5_SLAB
    gs = S5_SLAB // S5_GROUP
    eye = jnp.eye(gs, dtype=F32)

    def pack_b(m):
        m = m.reshape(n_slab, gs, S5_STATE, S5_GROUP)
        return jnp.einsum('sgpc,gh->sgchp', m, eye).reshape(n_slab, S5_SLAB, S5_SLAB_STATE)

    def pack_c(m):
        m = m.reshape(n_slab, gs, S5_GROUP, S5_STATE)
        return jnp.einsum('sgcp,gh->shpgc', m, eye).reshape(n_slab, S5_SLAB_STATE, S5_SLAB)

    bd_b = jnp.concatenate([pack_b(bb_re), pack_b(bb_im)], axis=2).astype(BF16)
    bd_c = jnp.concatenate([pack_c(c_re.astype(F32)), pack_c(-c_im.astype(F32))], axis=1).astype(BF16)
    a_bar = jnp.stack([ab_re.reshape(n_slab, S5_SLAB_STATE), ab_im.reshape(n_slab, S5_SLAB_STATE)],
                      axis=1)
    return bd_b, a_bar, bd_c, d_skip.astype(F32).reshape(n_slab, 1, S5_SLAB)


def _glu_kernel(y_ref, w_ref, b_ref, o_ref, *, chunks):
    cm = y_ref.shape[0] // chunks
    for c in range(chunks):
        rows = slice(c * cm, (c + 1) * cm)
        g = jax.nn.gelu(y_ref[rows, :].astype(F32))
        r = jnp.dot(g.astype(BF16), w_ref[...], preferred_element_type=F32) + b_ref[...]
        o_ref[rows, :] = (g * _sigmoid(r)).astype(o_ref.dtype)


def _glu(y2d, w, bias, *, tm=512, chunks=4):
    t, e = y2d.shape
    tm = min(tm, t)
    assert t % tm == 0 and tm % chunks == 0
    return pl.pallas_call(
        functools.partial(_glu_kernel, chunks=chunks),
        grid=(t // tm,),
        in_specs=[
            pl.BlockSpec((tm, e), lambda i: (i, 0)),
            pl.BlockSpec((e, e), lambda i: (0, 0), pipeline_mode=pl.Buffered(1)),
            pl.BlockSpec((1, e), lambda i: (0, 0)),
        ],
        out_specs=pl.BlockSpec((tm, e), lambda i: (i, 0)),
        out_shape=jax.ShapeDtypeStruct((t, e), BF16),
        compiler_params=_params("parallel"),
        name="s5_glu",
    )(y2d, w, bias.reshape(1, e).astype(F32))


def _s5_layer(x2d, h2d, bsz, seq, post_g, next_g, w_in, a_re, a_im, log_dt, b_re, b_im, c_re,
              c_im, d_skip, w_glu, b_glu, w_out):
    e = MIX_WIDTH
    proj = _in_proj(h2d, w_in.astype(BF16))
    bd_b, a_bar, bd_c, d3 = _s5_discretize(a_re, a_im, log_dt, b_re, b_im, c_re, c_im, d_skip)
    y = _s5_ssm(proj.reshape(bsz, seq, 2 * e), bd_b, a_bar, bd_c, d3)
    y = _glu(y.reshape(bsz * seq, e), w_glu.astype(BF16), b_glu)
    return _gate_out(y, proj, 1, x2d, w_out.astype(BF16), post_g, next_g)


def _attn_projections(h2d, seq, w_in):
    w = w_in.astype(BF16)
    proj = _in_proj(h2d, w, skip_block=2)
    vt = _in_proj_t(h2d, w, 2, seq)
    return proj, vt


def _diff_layer(x2d, h2d, bsz, seq, layer_idx, post_g, next_g, w_in, lq1, lk1, lq2, lk2, subln,
                w_out):
    e = MIX_WIDTH
    proj, vt = _attn_projections(h2d, seq, w_in)
    lam_init = 0.8 - 0.6 * math.exp(-0.3 * layer_idx)
    lam = (jnp.exp(jnp.sum(lq1.astype(F32) * lk1.astype(F32)))
           - jnp.exp(jnp.sum(lq2.astype(F32) * lk2.astype(F32))) + lam_init).reshape(1)
    o = _diff_attention(proj.reshape(bsz, seq, 3 * e), vt, _alibi_slopes(DIFF_HEADS), lam, subln,
                        1.0 - lam_init)
    return _gate_out(o.reshape(bsz * seq, e), proj, 2, x2d, w_out.astype(BF16), post_g, next_g)


def _moba_layer(x2d, h2d, bsz, seq, post_g, next_g, w_in, w_out):
    e = MIX_WIDTH
    proj, vt = _attn_projections(h2d, seq, w_in)
    o = _moba_attention(proj.reshape(bsz, seq, 3 * e), vt, _alibi_slopes(MOBA_HEADS))
    return _gate_out(o.reshape(bsz * seq, e), proj, 2, x2d, w_out.astype(BF16), post_g, next_g)


def _swa_layer(x2d, h2d, bsz, seq, post_g, next_g, w_in, sinks, w_out):
    e = MIX_WIDTH
    dh = SWA_HEAD_DIM
    nqc, nkc = SWA_Q_HEADS * dh, SWA_KV_HEADS * dh
    wq, wk = w_in[:, :nqc], w_in[:, nqc:nqc + nkc]
    wv, wz = w_in[:, nqc + nkc:nqc + 2 * nkc], w_in[:, nqc + 2 * nkc:]

    def dup(w):
        w = w.reshape(-1, SWA_KV_HEADS, 1, dh)
        return jnp.broadcast_to(w, (w.shape[0], SWA_KV_HEADS, 2, dh)).reshape(-1, 2 * nkc)

    w_cat = jnp.concatenate([wq, wz, dup(wk), dup(wv)], axis=1).astype(BF16)
    proj = _in_proj(h2d, w_cat)
    o = _swa_attention(proj.reshape(bsz, seq, w_cat.shape[1]), _alibi_slopes(SWA_Q_HEADS),
                       sinks.astype(F32))
    return _gate_out(o.reshape(bsz * seq, e), proj, 1, x2d, w_out.astype(BF16), post_g, next_g)


def kernel(x, pre_norm, post_norm, s5_w_in, s5_a_re, s5_a_im, s5_log_dt, s5_b_re, s5_b_im, s5_c_re, s5_c_im, s5_d, s5_w_glu, s5_b_glu, s5_w_out, diff_w_in, diff_lq1, diff_lk1, diff_lq2, diff_lk2, diff_subln, diff_w_out, moba_w_in, moba_w_out, swa_w_in, swa_sinks, swa_w_out):
    bsz, seq, d = x.shape
    depth = pre_norm.shape[0]
    x2d = x.reshape(bsz * seq, d)
    h2d = _prenorm(x2d, pre_norm[0])
    for i in range(depth):
        kind, j = i % 4, i // 4
        post_g = post_norm[i]
        next_g = pre_norm[i + 1] if i + 1 < depth else None
        if kind == 0:
            x2d, h2d = _s5_layer(x2d, h2d, bsz, seq, post_g, next_g, s5_w_in[j], s5_a_re[j],
                                 s5_a_im[j], s5_log_dt[j], s5_b_re[j], s5_b_im[j], s5_c_re[j],
                                 s5_c_im[j], s5_d[j], s5_w_glu[j], s5_b_glu[j], s5_w_out[j])
        elif kind == 1:
            x2d, h2d = _diff_layer(x2d, h2d, bsz, seq, i, post_g, next_g, diff_w_in[j],
                                   diff_lq1[j], diff_lk1[j], diff_lq2[j], diff_lk2[j],
                                   diff_subln[j], diff_w_out[j])
        elif kind == 2:
            x2d, h2d = _moba_layer(x2d, h2d, bsz, seq, post_g, next_g, moba_w_in[j],
                                   moba_w_out[j])
        else:
            x2d, h2d = _swa_layer(x2d, h2d, bsz, seq, post_g, next_g, swa_w_in[j], swa_sinks[j],
                                  swa_w_out[j])
    return x2d.reshape(bsz, seq, d)
```
